```python
import math
import jax
import jax.numpy as jnp
from jax import lax
import numpy as np

D_MODEL = 1024
BATCH = 32
SEQ = 256
DEPTH = 2
DEC_BATCH = 2
DEC_SEQ = 4096
PAST_LEN = 512

GRID_W = 64
CHUNK = 64
EPS = 1e-6
H_A = 4
DK_A = 128
DV_A = 128
W_A = H_A * DV_A
QKV_CONV = 3
W_B = 512
GC_B = 16
G_B = W_B // GC_B
P_B = 64
DT_MIN = 1e-3
DT_MAX = 1e-1
H_C = 4
DK_C = 128
DV_C = 128
W_C = H_C * DV_C
ROPE_BASE = 10000.0
ROPE_PAIRS = DK_C // 4
D_FF = 2816
FFN_CONV = 3
N_BRANCH = 3
N_MOD = 6
IN_SIZES = (2 * H_A * DK_A + H_A * DV_A, 2 * H_A, 2 * H_A, W_A, W_B, H_C * DK_C, H_C * DK_C, W_C, W_C, N_BRANCH * D_MODEL)
N_IN = sum(IN_SIZES)

kernel_name = 'hybrid_gdn_s5_retention_prefix_dit'


def rms_normalize(x):
    xf = x.astype(jnp.float32)
    return xf * lax.rsqrt(jnp.mean(xf * xf, axis=-1, keepdims=True) + EPS)


def rmsnorm(x, w):
    return rms_normalize(x) * w


def l2normalize(x):
    xf = x.astype(jnp.float32)
    return xf * lax.rsqrt(jnp.sum(xf * xf, axis=-1, keepdims=True) + EPS)


def flip_seq(t):
    return jnp.flip(t, axis=1)


def centred_dwconv(x, w):
    k = w.shape[0]
    return lax.conv_general_dilated(x, w[:, None, :].astype(x.dtype), window_strides=(1,),
                                    padding=[(k // 2, k // 2)],
                                    dimension_numbers=('NWC', 'WIO', 'NWC'),
                                    feature_group_count=x.shape[-1])


def to_blocks(t):
    b, l, h = t.shape[:3]
    t = t.reshape((b, l // CHUNK, CHUNK, h) + t.shape[3:])
    return jnp.moveaxis(jnp.swapaxes(t, 2, 3), 1, 0)


def from_blocks(o):
    n, b, h, c = o.shape[:4]
    o = jnp.swapaxes(jnp.moveaxis(o, 0, 1), 2, 3)
    return o.reshape((b, n * c, h) + o.shape[4:])


def grid_rope(n_tokens):
    n_rows = n_tokens // GRID_W
    rows = jnp.repeat(jnp.arange(n_rows), GRID_W).astype(jnp.float32)
    cols = jnp.tile(jnp.arange(GRID_W), n_rows).astype(jnp.float32)
    freqs = ROPE_BASE ** (-jnp.arange(ROPE_PAIRS, dtype=jnp.float32) / ROPE_PAIRS)
    ang = jnp.concatenate([rows[:, None] * freqs, cols[:, None] * freqs], axis=-1)
    return jnp.cos(ang)[None, :, None, :], jnp.sin(ang)[None, :, None, :]


def apply_rope(x, cos, sin):
    x1, x2 = x[..., :DK_C // 2], x[..., DK_C // 2:]
    return jnp.concatenate([x1 * cos - x2 * sin, x1 * sin + x2 * cos], axis=-1)


def gated_delta_chunked(q, k, v, g, beta, s0):
    qc, kc, vc = to_blocks(q), to_blocks(k), to_blocks(v)
    gc = jnp.cumsum(to_blocks(g), axis=-1)
    bc = to_blocks(beta)
    idx = jnp.arange(CHUNK)
    lower = idx[:, None] >= idx[None, :]
    strict = idx[:, None] > idx[None, :]
    decay = jnp.exp(jnp.where(lower, gc[..., :, None] - gc[..., None, :], -jnp.inf))
    kb = kc * bc[..., None]
    lmat = jnp.where(strict, jnp.einsum('nbhcd,nbhmd->nbhcm', kb, kc) * decay, 0.0)
    tmat = lmat + jnp.eye(CHUNK, dtype=lmat.dtype)
    u = lax.linalg.triangular_solve(tmat, vc * bc[..., None], left_side=True, lower=True, unit_diagonal=True)
    w = lax.linalg.triangular_solve(tmat, kb * jnp.exp(gc)[..., None], left_side=True, lower=True, unit_diagonal=True)
    qk = jnp.einsum('nbhcd,nbhmd->nbhcm', qc, kc) * decay
    qg = qc * jnp.exp(gc)[..., None]
    kg = kc * jnp.exp(gc[..., -1:] - gc)[..., None]
    glast = jnp.exp(gc[..., -1])

    def step(s, inp):
        qg_i, kg_i, u_i, w_i, qk_i, gl_i = inp
        v_new = u_i - jnp.einsum('bhcd,bhde->bhce', w_i, s)
        o = jnp.einsum('bhcd,bhde->bhce', qg_i, s) + jnp.einsum('bhcm,bhme->bhce', qk_i, v_new)
        s = s * gl_i[..., None, None] + jnp.einsum('bhcd,bhce->bhde', kg_i, v_new)
        return s, o

    s, o = lax.scan(step, s0, (qg, kg, u, w, qk, glast))
    return from_blocks(o), s


def retention_log_decay():
    return jnp.log1p(-jnp.exp2(-5.0 - jnp.arange(H_C, dtype=jnp.float32)))


def retention_chunked(q, k, v, log_gamma, s0):
    qc, kc, vc = to_blocks(q), to_blocks(k), to_blocks(v)
    idx = jnp.arange(CHUNK, dtype=jnp.float32)
    dist = idx[:, None] - idx[None, :]
    lg = log_gamma[:, None, None]
    dmat = jnp.where(dist >= 0, jnp.exp(lg * jnp.maximum(dist, 0.0)), 0.0)
    inner = jnp.einsum('nbhcm,nbhme->nbhce', jnp.einsum('nbhcd,nbhmd->nbhcm', qc, kc) * dmat, vc)
    q_dec = jnp.exp(log_gamma[:, None] * (idx + 1.0))
    k_dec = jnp.exp(log_gamma[:, None] * (CHUNK - 1.0 - idx))
    chunk_dec = jnp.exp(log_gamma * CHUNK)
    qx = qc * q_dec[..., None]
    kx = kc * k_dec[..., None]

    def step(s, inp):
        qx_i, kx_i, v_i = inp
        o = jnp.einsum('bhcd,bhde->bhce', qx_i, s)
        s = s * chunk_dec[:, None, None] + jnp.einsum('bhcd,bhce->bhde', kx_i, v_i)
        return s, o

    s, cross = lax.scan(step, s0, (qx, kx, vc))
    return from_blocks(inner + cross), s


def s5_combine(earlier, later):
    a1r, a1i, b1r, b1i = earlier
    a2r, a2i, b2r, b2i = later
    return (a2r * a1r - a2i * a1i, a2r * a1i + a2i * a1r,
            a2r * b1r - a2i * b1i + b2r, a2r * b1i + a2i * b1r + b2i)


def s5_scan(in_re, in_im, a_re, a_im, x0_re, x0_im):
    in_re = in_re.at[:, 0].add(a_re * x0_re - a_im * x0_im)
    in_im = in_im.at[:, 0].add(a_re * x0_im + a_im * x0_re)
    ar = jnp.broadcast_to(a_re, in_re.shape)
    ai = jnp.broadcast_to(a_im, in_im.shape)
    _, _, xr, xi = lax.associative_scan(s5_combine, (ar, ai, in_re, in_im), axis=1)
    return xr, xi


def s5_branch(u, lp, x0_re, x0_im):
    b, l, _ = u.shape
    uf = u.astype(jnp.float32)
    ug = uf.reshape(b, l, G_B, GC_B)
    bu_re = jnp.einsum('blgc,gpc->blgp', ug, lp['ssm_b_re'])
    bu_im = jnp.einsum('blgc,gpc->blgp', ug, lp['ssm_b_im'])
    y = uf * lp['ssm_d']
    fin_re, fin_im = [], []
    for d in range(2):
        lam_re = lp['ssm_lam_re'][d].astype(jnp.float32)
        lam_im = lp['ssm_lam_im'][d].astype(jnp.float32)
        step = jnp.exp(lp['ssm_log_dt'][d].astype(jnp.float32))[:, None]
        mag = jnp.exp(lam_re * step)
        ang = lam_im * step
        a_re = mag * jnp.cos(ang)
        a_im = mag * jnp.sin(ang)
        den = lam_re * lam_re + lam_im * lam_im
        z_re = ((a_re - 1.0) * lam_re + a_im * lam_im) / den
        z_im = (a_im * lam_re - (a_re - 1.0) * lam_im) / den
        in_re = z_re * bu_re - z_im * bu_im
        in_im = z_re * bu_im + z_im * bu_re
        if d == 1:
            in_re, in_im = flip_seq(in_re), flip_seq(in_im)
        xr, xi = s5_scan(in_re, in_im, a_re, a_im, x0_re[:, d], x0_im[:, d])
        fin_re.append(xr[:, -1])
        fin_im.append(xi[:, -1])
        yd = (jnp.einsum('blgp,gcp->blgc', xr, lp['ssm_c_re'])
              - jnp.einsum('blgp,gcp->blgc', xi, lp['ssm_c_im']))
        if d == 1:
            yd = flip_seq(yd)
        y = y + yd.reshape(b, l, W_B)
    y = jax.nn.gelu(y)
    y = y * jax.nn.sigmoid(y @ lp['w_glu'] + lp['b_glu'])
    return y, jnp.stack(fin_re, axis=1), jnp.stack(fin_im, axis=1)


def token_mixers(h, lp, rope, s_delta, s_re, s_im, s_ret):
    b, l, _ = h.shape
    proj = h @ lp['w_in']
    splits = [int(s) for s in np.cumsum(IN_SIZES)[:-1]]
    qkv_a, alpha_a, beta_a, z_a, u_b, q_c, k_c, v_c, g_c, gates = jnp.split(proj, splits, axis=-1)

    s_delta = s_delta.astype(jnp.float32)
    qkv = jax.nn.silu(centred_dwconv(qkv_a, lp['w_conv_qkv']))
    qa, ka, va = jnp.split(qkv, [H_A * DK_A, 2 * H_A * DK_A], axis=-1)
    qa = l2normalize(qa.reshape(b, l, H_A, DK_A)) * (DK_A ** -0.5)
    ka = l2normalize(ka.reshape(b, l, H_A, DK_A))
    va = va.reshape(b, l, H_A, DV_A).astype(jnp.float32)
    log_a = -jnp.exp(lp['a_log']) * jax.nn.softplus(alpha_a.reshape(b, l, 2, H_A).astype(jnp.float32) + lp['dt_bias'])
    bt = jax.nn.sigmoid(beta_a.reshape(b, l, 2, H_A).astype(jnp.float32))
    o_f, sd_f = gated_delta_chunked(qa, ka, va, log_a[:, :, 0], bt[:, :, 0], s_delta[:, 0])
    o_b, sd_b = gated_delta_chunked(flip_seq(qa), flip_seq(ka), flip_seq(va), flip_seq(log_a[:, :, 1]),
                                    flip_seq(bt[:, :, 1]), s_delta[:, 1])
    o_a = rmsnorm(o_f + flip_seq(o_b), lp['norm_a']) * jax.nn.silu(z_a.reshape(b, l, H_A, DV_A).astype(jnp.float32))
    br_a = o_a.reshape(b, l, W_A) @ lp['w_br_a']

    y_b, ss_re, ss_im = s5_branch(u_b, lp, s_re.astype(jnp.float32), s_im.astype(jnp.float32))
    br_b = y_b @ lp['w_br_b']

    s_ret = s_ret.astype(jnp.float32)
    qr = q_c.reshape(b, l, H_C, DK_C).astype(jnp.float32)
    kr = k_c.reshape(b, l, H_C, DK_C).astype(jnp.float32)
    if rope is not None:
        qr = apply_rope(qr, rope[0], rope[1])
        kr = apply_rope(kr, rope[0], rope[1])
    kr = kr * (DK_C ** -0.5)
    vr = v_c.reshape(b, l, H_C, DV_C).astype(jnp.float32)
    lg = retention_log_decay()
    r_f, sr_f = retention_chunked(qr, kr, vr, lg, s_ret[:, 0])
    r_b, sr_b = retention_chunked(flip_seq(qr), flip_seq(kr), flip_seq(vr), lg[::-1], s_ret[:, 1])
    o_c = rms_normalize(r_f + flip_seq(r_b)) * jax.nn.silu(g_c.reshape(b, l, H_C, DV_C).astype(jnp.float32))
    br_c = o_c.reshape(b, l, W_C) @ lp['w_br_c']

    gt = jax.nn.sigmoid(gates.astype(jnp.float32)).reshape(b, l, N_BRANCH, D_MODEL)
    merged = gt[:, :, 0] * br_a + gt[:, :, 1] * br_b + gt[:, :, 2] * br_c
    out = merged @ lp['w_o']
    new_states = (jnp.stack([sd_f, sd_b], axis=1), ss_re, ss_im, jnp.stack([sr_f, sr_b], axis=1))
    return out, new_states


def conv_glu_ffn(h, lp):
    hu = centred_dwconv(h @ lp['w_up'], lp['w_conv_ffn']) + lp['b_conv_ffn']
    gate_part, value_part = jnp.split(hu, 2, axis=-1)
    return (jax.nn.silu(gate_part) * value_part) @ lp['w_down']


def trunk_layer(x, cond, lp, rope, states):
    mod = (jax.nn.silu(cond.astype(jnp.float32)) @ lp['w_mod'] + lp['b_mod'])[:, None, :]
    sh1, sc1, g1, sh2, sc2, g2 = jnp.split(mod, N_MOD, axis=-1)
    h = rmsnorm(x, lp['norm1']) * (1.0 + sc1) + sh1
    mix, new_states = token_mixers(h, lp, rope, states[0], states[1], states[2], states[3])
    x = x + g1 * mix
    h = rmsnorm(x, lp['norm2']) * (1.0 + sc2) + sh2
    x = x + g2 * conv_glu_ffn(h, lp)
    return x, new_states


def setup_inputs(seed: int = 0) -> dict:
    key = jax.random.key(seed)
    ks = iter(jax.random.split(key, 64))
    f32 = jnp.float32

    def nrm(shape, scale):
        return scale * jax.random.normal(next(ks), shape, f32)

    def unif(shape, lo, hi):
        return jax.random.uniform(next(ks), shape, f32, lo, hi)

    x_prompt = nrm((BATCH, SEQ, D_MODEL), 1.0)
    x_sample = nrm((DEC_BATCH, DEC_SEQ, D_MODEL), 1.0)
    state_delta = nrm((DEC_BATCH, DEPTH, 2, H_A, DK_A, DV_A), 0.05)
    state_ssm_re = nrm((DEC_BATCH, DEPTH, 2, G_B, P_B), 0.1)
    state_ssm_im = nrm((DEC_BATCH, DEPTH, 2, G_B, P_B), 0.1)
    state_ret = nrm((DEC_BATCH, DEPTH, 2, H_C, DK_C, DV_C), 0.3)
    c = nrm((DEC_BATCH, D_MODEL), 1.0)
    c_ctx = nrm((D_MODEL,), 1.0)
    final_norm = 1.0 + nrm((D_MODEL,), 0.02)
    norm1 = 1.0 + nrm((DEPTH, D_MODEL), 0.02)
    norm2 = 1.0 + nrm((DEPTH, D_MODEL), 0.02)
    w_mod = nrm((DEPTH, D_MODEL, N_MOD * D_MODEL), D_MODEL ** -0.5)
    b_mod = nrm((DEPTH, N_MOD * D_MODEL), 0.02)
    w_in = nrm((DEPTH, D_MODEL, N_IN), D_MODEL ** -0.5)
    w_conv_qkv = nrm((DEPTH, QKV_CONV, IN_SIZES[0]), QKV_CONV ** -0.5)
    a_log = jnp.log(unif((DEPTH, 2, H_A), 1.0, 16.0))
    dt0 = jnp.exp(unif((DEPTH, 2, H_A), math.log(DT_MIN), math.log(DT_MAX)))
    dt_bias = dt0 + jnp.log(-jnp.expm1(-dt0))
    norm_a = 1.0 + nrm((DEPTH, DV_A), 0.02)
    w_br_a = nrm((DEPTH, W_A, D_MODEL), W_A ** -0.5)
    ssm_lam_re = -0.5 + nrm((DEPTH, 2, G_B, P_B), 0.01)
    ssm_lam_im = math.pi * jnp.arange(P_B, dtype=f32) + nrm((DEPTH, 2, G_B, P_B), 0.01)
    ssm_log_dt = unif((DEPTH, 2, G_B), math.log(DT_MIN), math.log(DT_MAX))
    ssm_b_re = nrm((DEPTH, G_B, P_B, GC_B), (2 * GC_B) ** -0.5)
    ssm_b_im = nrm((DEPTH, G_B, P_B, GC_B), (2 * GC_B) ** -0.5)
    ssm_c_re = nrm((DEPTH, G_B, GC_B, P_B), P_B ** -0.5)
    ssm_c_im = nrm((DEPTH, G_B, GC_B, P_B), P_B ** -0.5)
    ssm_d = nrm((DEPTH, W_B), 1.0)
    w_glu = nrm((DEPTH, W_B, W_B), W_B ** -0.5)
    b_glu = nrm((DEPTH, W_B), 0.02)
    w_br_b = nrm((DEPTH, W_B, D_MODEL), W_B ** -0.5)
    w_br_c = nrm((DEPTH, W_C, D_MODEL), W_C ** -0.5)
    w_o = nrm((DEPTH, D_MODEL, D_MODEL), D_MODEL ** -0.5)
    w_up = nrm((DEPTH, D_MODEL, 2 * D_FF), D_MODEL ** -0.5)
    w_conv_ffn = nrm((DEPTH, FFN_CONV, 2 * D_FF), FFN_CONV ** -0.5)
    b_conv_ffn = nrm((DEPTH, 2 * D_FF), 0.02)
    w_down = nrm((DEPTH, D_FF, D_MODEL), D_FF ** -0.5)
    return {'x_prompt': x_prompt, 'x_sample': x_sample, 'state_delta': state_delta,
            'state_ssm_re': state_ssm_re, 'state_ssm_im': state_ssm_im, 'state_ret': state_ret,
            'c': c, 'c_ctx': c_ctx, 'final_norm': final_norm, 'norm1': norm1, 'norm2': norm2,
            'w_mod': w_mod, 'b_mod': b_mod, 'w_in': w_in, 'w_conv_qkv': w_conv_qkv, 'a_log': a_log,
            'dt_bias': dt_bias, 'norm_a': norm_a, 'w_br_a': w_br_a, 'ssm_lam_re': ssm_lam_re,
            'ssm_lam_im': ssm_lam_im, 'ssm_log_dt': ssm_log_dt, 'ssm_b_re': ssm_b_re, 'ssm_b_im': ssm_b_im,
            'ssm_c_re': ssm_c_re, 'ssm_c_im': ssm_c_im, 'ssm_d': ssm_d, 'w_glu': w_glu, 'b_glu': b_glu,
            'w_br_b': w_br_b, 'w_br_c': w_br_c, 'w_o': w_o, 'w_up': w_up, 'w_conv_ffn': w_conv_ffn,
            'b_conv_ffn': b_conv_ffn, 'w_down': w_down}


def reference(x_prompt, x_sample, state_delta, state_ssm_re, state_ssm_im, state_ret, c, c_ctx,
              final_norm, norm1, norm2, w_mod, b_mod, w_in, w_conv_qkv, a_log, dt_bias, norm_a, w_br_a,
              ssm_lam_re, ssm_lam_im, ssm_log_dt, ssm_b_re, ssm_b_im, ssm_c_re, ssm_c_im, ssm_d,
              w_glu, b_glu, w_br_b, w_br_c, w_o, w_up, w_conv_ffn, b_conv_ffn, w_down):
    layers = [dict(norm1=norm1[i], norm2=norm2[i], w_mod=w_mod[i], b_mod=b_mod[i], w_in=w_in[i],
                   w_conv_qkv=w_conv_qkv[i], a_log=a_log[i], dt_bias=dt_bias[i], norm_a=norm_a[i],
                   w_br_a=w_br_a[i], ssm_lam_re=ssm_lam_re[i], ssm_lam_im=ssm_lam_im[i],
                   ssm_log_dt=ssm_log_dt[i], ssm_b_re=ssm_b_re[i], ssm_b_im=ssm_b_im[i],
                   ssm_c_re=ssm_c_re[i], ssm_c_im=ssm_c_im[i], ssm_d=ssm_d[i], w_glu=w_glu[i],
                   b_glu=b_glu[i], w_br_b=w_br_b[i], w_br_c=w_br_c[i], w_o=w_o[i], w_up=w_up[i],
                   w_conv_ffn=w_conv_ffn[i], b_conv_ffn=b_conv_ffn[i], w_down=w_down[i])
              for i in range(DEPTH)]

    bp = x_prompt.shape[0]
    zero_states = (jnp.zeros((bp, 2, H_A, DK_A, DV_A), jnp.float32),
                   jnp.zeros((bp, 2, G_B, P_B), jnp.float32),
                   jnp.zeros((bp, 2, G_B, P_B), jnp.float32),
                   jnp.zeros((bp, 2, H_C, DK_C, DV_C), jnp.float32))
    cond_ctx = c_ctx[None, :]
    xp = x_prompt
    ctx_states = []
    for i in range(DEPTH):
        xp, st = trunk_layer(xp, cond_ctx, layers[i], None, zero_states)
        ctx_states.append(st)
    y_prompt = rmsnorm(xp, final_norm)

    rope = grid_rope(x_sample.shape[1])
    xs = x_sample
    for i in range(DEPTH):
        xs, _ = trunk_layer(xs, c, layers[i], rope,
                            (state_delta[:, i], state_ssm_re[:, i], state_ssm_im[:, i], state_ret[:, i]))
    y_sample = rmsnorm(xs, final_norm)

    new_state_delta = jnp.stack([st[0] for st in ctx_states], axis=1)
    new_state_ssm_re = jnp.stack([st[1] for st in ctx_states], axis=1)
    new_state_ssm_im = jnp.stack([st[2] for st in ctx_states], axis=1)
    new_state_ret = jnp.stack([st[3] for st in ctx_states], axis=1)
    return (y_prompt, y_sample, new_state_delta, new_state_ssm_re, new_state_ssm_im, new_state_ret)
```

```python
import functools
import math

import jax
import jax.numpy as jnp
from jax import lax
from jax.experimental import pallas as pl
from jax.experimental.pallas import tpu as pltpu

F32 = jnp.float32
BF16 = jnp.bfloat16

D_MODEL = 1024
EPS = 1e-6
N_HEADS = 4
D_HEAD = 128
W_MIX = N_HEADS * D_HEAD
GDN_CHUNK = 64
SEG = 256
GRID_W = 64
ROPE_BASE = 10000.0
G_B, P_B, GC_B = 32, 64, 16
S5_TILE = 8
S5_BLOCKS = 4
S5_W = S5_TILE * 128
D_FF = 2816
LANE = 128
VMEM_LIMIT = 56 * 1024 * 1024

U_GATES = 0
U_QA, U_KA, U_VA, U_ZA = 24, 28, 32, 36
U_UB = 40
U_QC, U_KC, U_VC, U_GC = 44, 48, 52, 56
N_PROJ = 60 * LANE
PROJ_TN = 1280
UB_BLOCK = (U_UB * LANE) // PROJ_TN
UB_LOCAL = U_UB * LANE - UB_BLOCK * PROJ_TN


def _cparams(sem):
    return pltpu.CompilerParams(dimension_semantics=sem, vmem_limit_bytes=VMEM_LIMIT)


def _dot(a, b):
    return jnp.dot(a.astype(BF16), b.astype(BF16), preferred_element_type=F32)


def _dot_nt(a, b):
    return lax.dot_general(a.astype(BF16), b.astype(BF16), (((1,), (1,)), ((), ())),
                           preferred_element_type=F32)


def _dot_tn(a, b):
    return lax.dot_general(a.astype(BF16), b.astype(BF16), (((0,), (0,)), ((), ())),
                           preferred_element_type=F32)


def _sigmoid(x):
    return jax.nn.sigmoid(x)


def _silu(x):
    return x * _sigmoid(x)


def _softplus(x):
    return jnp.maximum(x, 0.0) + jnp.log1p(jnp.exp(-jnp.abs(x)))


def _gelu_tanh(x):
    return 0.5 * x * (1.0 + jnp.tanh(math.sqrt(2.0 / math.pi) * (x + 0.044715 * x * x * x)))


def _rms(x):
    return x * lax.rsqrt(jnp.mean(x * x, axis=-1, keepdims=True) + EPS)


def _split3(x):
    hi = x.astype(BF16).astype(F32)
    r1 = x - hi
    mid = r1.astype(BF16).astype(F32)
    lo = (r1 - mid).astype(BF16).astype(F32)
    return hi, mid, lo


def _mod_kernel(c_ref, w_ref, b_ref, o_ref):
    o_ref[...] = _dot(_silu(c_ref[...]), w_ref[...]) + b_ref[...]


def _modulation(cond8, w_mod, b_mod):
    depth, _, n = w_mod.shape
    tn = 1536
    return pl.pallas_call(
        _mod_kernel,
        grid=(depth, n // tn),
        in_specs=[pl.BlockSpec((8, D_MODEL), lambda l, j: (0, 0)),
                  pl.BlockSpec((None, D_MODEL, tn), lambda l, j: (l, 0, j)),
                  pl.BlockSpec((None, 1, tn), lambda l, j: (l, 0, j))],
        out_specs=pl.BlockSpec((None, 8, tn), lambda l, j: (l, 0, j)),
        out_shape=jax.ShapeDtypeStruct((depth, 8, n), F32),
        compiler_params=_cparams(("parallel", "parallel")),
        name="modulation",
    )(cond8, w_mod, b_mod.reshape(depth, 1, n))


def _mod_row_map(tm, rows_per_cond):
    if rows_per_cond is None:
        return lambda i: 0
    return lambda i: 1 + (i * tm) // rows_per_cond


def _inproj_kernel(x_ref, mod_ref, nw_ref, w_ref, wab_ref, proj_ref, ab_ref, u4_ref, h_scr):
    j = pl.program_id(1)

    @pl.when(j == 0)
    def _():
        h = _rms(x_ref[...]) * nw_ref[...]
        h = h * (1.0 + mod_ref[1:2, :]) + mod_ref[0:1, :]
        hb = h.astype(BF16)
        h_scr[...] = hb
        ab_ref[...] = jnp.dot(hb, wab_ref[...], preferred_element_type=F32)

    acc = jnp.dot(h_scr[...], w_ref[...], preferred_element_type=F32)
    proj_ref[...] = acc

    @pl.when(j == UB_BLOCK)
    def _():
        for b in range(S5_BLOCKS):
            u4_ref[b] = acc[:, UB_LOCAL + LANE * b:UB_LOCAL + LANE * (b + 1)]


def _in_projection(x2d, mod6, norm_w, w_main, w_ab, rows_per_cond):
    n = x2d.shape[0]
    tm, tn = 512, PROJ_TN
    row = _mod_row_map(tm, rows_per_cond)
    return pl.pallas_call(
        _inproj_kernel,
        grid=(n // tm, N_PROJ // tn),
        in_specs=[pl.BlockSpec((tm, D_MODEL), lambda i, j: (i, 0)),
                  pl.BlockSpec((None, 6, D_MODEL), lambda i, j: (row(i), 0, 0)),
                  pl.BlockSpec((1, D_MODEL), lambda i, j: (0, 0)),
                  pl.BlockSpec((D_MODEL, tn), lambda i, j: (0, j)),
                  pl.BlockSpec((D_MODEL, LANE), lambda i, j: (0, 0))],
        out_specs=[pl.BlockSpec((tm, tn), lambda i, j: (i, j)),
                   pl.BlockSpec((tm, LANE), lambda i, j: (i, 0)),
                   pl.BlockSpec((S5_BLOCKS, tm, LANE), lambda i, j: (0, i, 0))],
        out_shape=[jax.ShapeDtypeStruct((n, N_PROJ), F32),
                   jax.ShapeDtypeStruct((n, LANE), F32),
                   jax.ShapeDtypeStruct((S5_BLOCKS, n, LANE), F32)],
        scratch_shapes=[pltpu.VMEM((tm, D_MODEL), BF16)],
        compiler_params=_cparams(("parallel", "arbitrary")),
        name="in_projection",
    )(x2d, mod6, norm_w, w_main, w_ab)


M_INCL, M_STRICT, M_EYE, M_LEVEL = 0, 2, 4, 4
N_MASKS = 11


def _gdn_masks():
    r = jnp.arange(SEG)[:, None]
    c = jnp.arange(SEG)[None, :]
    same = (r // GDN_CHUNK) == (c // GDN_CHUNK)
    incl = [same & (r >= c), same & (r <= c)]
    strict = [same & (r > c), same & (r < c)]
    eye = [r == c]
    levels = [((r >> k) == (c >> k)) & ((r >> (k - 1)) != (c >> (k - 1))) for k in range(1, 7)]
    return jnp.stack(incl + strict + eye + levels).astype(F32)


def _gdn_pre_kernel(*refs, segs_per_seq):
    has_halo = segs_per_seq > 1
    q_ref, k_ref, v_ref = refs[:3]
    pos = 3
    halos = None
    if has_halo:
        halos = refs[3:9]
        pos = 9
    (ab_ref, wq_ref, wk_ref, wv_ref, gp_ref, msk_ref, tri_ref,
     u_ref, w_ref, qg_ref, kg_ref, qkm_ref, eg_ref) = refs[pos:]
    seg = pl.program_id(0)
    h = pl.program_id(1)
    rowi = lax.broadcasted_iota(jnp.int32, (SEG, LANE), 0)
    lane = lax.broadcasted_iota(jnp.int32, (SEG, LANE), 1)

    def conv_act(x_ref, cw_ref, prev_ref, next_ref):
        x = x_ref[...]
        xm = pltpu.roll(x, 1, 0)
        xp = pltpu.roll(x, SEG - 1, 0)
        if has_halo:
            p = seg % segs_per_seq
            prev = jnp.where(p > 0, prev_ref[7:8, :], 0.0)
            nxt = jnp.where(p < segs_per_seq - 1, next_ref[0:1, :], 0.0)
        else:
            prev = jnp.zeros((1, LANE), F32)
            nxt = prev
        xm = jnp.where(rowi == 0, prev, xm)
        xp = jnp.where(rowi == SEG - 1, nxt, xp)
        return _silu(cw_ref[0:1, :] * xm + cw_ref[1:2, :] * x + cw_ref[2:3, :] * xp)

    hq = halos[0:2] if has_halo else (None, None)
    hk = halos[2:4] if has_halo else (None, None)
    hv = halos[4:6] if has_halo else (None, None)
    qa = conv_act(q_ref, wq_ref, *hq)
    ka = conv_act(k_ref, wk_ref, *hk)
    va = conv_act(v_ref, wv_ref, *hv)
    qn = qa * lax.rsqrt(jnp.sum(qa * qa, axis=-1, keepdims=True) + EPS) * (D_HEAD ** -0.5)
    kn = ka * lax.rsqrt(jnp.sum(ka * ka, axis=-1, keepdims=True) + EPS)

    abv = ab_ref[...]
    log_a = -jnp.exp(gp_ref[0:1, :]) * _softplus(abv + gp_ref[1:2, :])
    beta_all = _sigmoid(abv)

    def col(x, idx):
        return jnp.sum(jnp.where(lane == idx, x, 0.0), axis=1, keepdims=True)

    kk = _dot_nt(kn, kn)
    qk = _dot_nt(qn, kn)

    for d in range(2):
        g = jnp.broadcast_to(col(log_a, d * N_HEADS + h), (SEG, LANE))
        beta = col(beta_all, 2 * N_HEADS + d * N_HEADS + h)
        ghi, gmid, glo = _split3(g)
        pieces = jnp.where(lane == 0, ghi, jnp.where(lane == 1, gmid, jnp.where(lane == 2, glo, 0.0)))
        cs = jnp.dot(tri_ref[d], pieces.astype(BF16), preferred_element_type=F32)
        gc = jnp.broadcast_to(jnp.sum(jnp.where(lane < 3, cs, 0.0), axis=1, keepdims=True), (SEG, LANE))
        chi, cmid, clo = _split3(gc)
        a_mat = jnp.where(lane == 0, chi, jnp.where(lane == 1, cmid, jnp.where(lane == 2, clo,
                          jnp.where(lane < 6, 1.0, 0.0))))
        b_mat = jnp.where(lane < 3, 1.0, jnp.where(lane == 3, -chi, jnp.where(lane == 4, -cmid,
                          jnp.where(lane == 5, -clo, 0.0))))
        diff = _dot_nt(a_mat, b_mat)
        e = jnp.exp(jnp.minimum(diff, 0.0))
        decay = e * msk_ref[M_INCL + d]
        lmat = (beta * kk) * (e * msk_ref[M_STRICT + d])
        x = msk_ref[M_EYE] - lmat * msk_ref[M_LEVEL + 1]
        for lvl in range(2, 7):
            c = lmat * msk_ref[M_LEVEL + lvl]
            x = x - _dot(x, _dot(c, x))
        egc = jnp.exp(gc)
        rhs = jnp.concatenate([va * beta, kn * (beta * egc)], axis=1)
        uw = _dot(x, rhs)
        u_ref[d] = uw[:, :D_HEAD]
        w_ref[d] = uw[:, D_HEAD:].astype(BF16)
        qg_ref[d] = (qn * egc).astype(BF16)
        qkm = qk * decay
        edge = GDN_CHUNK - 1 if d == 0 else 0
        for c4 in range(SEG // GDN_CHUNK):
            lo_r = c4 * GDN_CHUNK
            g_edge = gc[lo_r + edge:lo_r + edge + 1, :]
            kg_ref[d, lo_r:lo_r + GDN_CHUNK, :] = (
                kn[lo_r:lo_r + GDN_CHUNK, :] * jnp.exp(g_edge - gc[lo_r:lo_r + GDN_CHUNK, :])).astype(BF16)
            eg_ref[d, c4] = jnp.exp(g_edge)
            qkm_ref[d, lo_r:lo_r + GDN_CHUNK, :] = qkm[lo_r:lo_r + GDN_CHUNK,
                                                       lo_r:lo_r + GDN_CHUNK].astype(BF16)


def _gdn_precompute(proj, ab, wconv, gate_par, masks, tri, seq_len):
    n = proj.shape[0]
    segs_per_seq = seq_len // SEG
    nseg = n // SEG
    blk = lambda unit: pl.BlockSpec((SEG, LANE), lambda s, h, unit=unit: (s, unit + h))
    in_specs = [blk(U_QA), blk(U_KA), blk(U_VA)]
    args = [proj, proj, proj]
    if segs_per_seq > 1:
        r8 = SEG // 8
        last8 = n // 8 - 1
        for unit in (U_QA, U_KA, U_VA):
            in_specs.append(pl.BlockSpec((8, LANE), lambda s, h, unit=unit: (jnp.maximum(s * r8 - 1, 0), unit + h)))
            in_specs.append(pl.BlockSpec((8, LANE), lambda s, h, unit=unit: (jnp.minimum((s + 1) * r8, last8), unit + h)))
            args += [proj, proj]
    in_specs += [pl.BlockSpec((SEG, LANE), lambda s, h: (s, 0))]
    in_specs += [pl.BlockSpec((3, LANE), lambda s, h, o=o: (0, o + h)) for o in (0, 4, 8)]
    in_specs += [pl.BlockSpec((8, LANE), lambda s, h: (0, 0)),
                 pl.BlockSpec((N_MASKS, SEG, SEG), lambda s, h: (0, 0, 0)),
                 pl.BlockSpec((2, SEG, SEG), lambda s, h: (0, 0, 0))]
    args += [ab, wconv, wconv, wconv, gate_par, masks, tri]
    big = lambda width: pl.BlockSpec((2, None, SEG, width), lambda s, h: (0, h, s, 0))
    nchunk = n // GDN_CHUNK
    return pl.pallas_call(
        functools.partial(_gdn_pre_kernel, segs_per_seq=segs_per_seq),
        grid=(nseg, N_HEADS),
        in_specs=in_specs,
        out_specs=[big(D_HEAD), big(D_HEAD), big(D_HEAD), big(D_HEAD), big(GDN_CHUNK),
                   pl.BlockSpec((2, None, SEG // GDN_CHUNK, 1, LANE), lambda s, h: (0, h, s, 0, 0))],
        out_shape=[jax.ShapeDtypeStruct((2, N_HEADS, n, D_HEAD), F32),
                   jax.ShapeDtypeStruct((2, N_HEADS, n, D_HEAD), BF16),
                   jax.ShapeDtypeStruct((2, N_HEADS, n, D_HEAD), BF16),
                   jax.ShapeDtypeStruct((2, N_HEADS, n, D_HEAD), BF16),
                   jax.ShapeDtypeStruct((2, N_HEADS, n, GDN_CHUNK), BF16),
                   jax.ShapeDtypeStruct((2, N_HEADS, nchunk, 1, LANE), F32)],
        compiler_params=_cparams(("parallel", "parallel")),
        name="gdn_precompute",
    )(*args)


def _gdn_scan_kernel(*refs, chunks_per_seq, has_init):
    u_ref, w_ref, qg_ref, kg_ref, qkm_ref, eg_ref, z_ref, na_ref = refs[:8]
    if has_init:
        s0_ref, o_ref, s_scr, of_scr, ob_scr = refs[8:]
        sfin_ref = None
    else:
        o_ref, sfin_ref, s_scr, of_scr, ob_scr = refs[8:]
    rows = z_ref.shape[0]
    nchunk = rows // GDN_CHUNK
    cps = chunks_per_seq
    if has_init:
        s_scr[...] = s0_ref[...]
    else:
        s_scr[...] = jnp.zeros_like(s_scr)

    def body(i, carry):
        for d in range(2):
            c = i if d == 0 else nchunk - 1 - i
            rsl = pl.ds(pl.multiple_of(c * GDN_CHUNK, GDN_CHUNK), GDN_CHUNK)
            s = s_scr[d]
            pos = c % cps
            if not has_init:
                s = jnp.where(pos == (0 if d == 0 else cps - 1), 0.0, s)
            sb = s.astype(BF16)
            vnew = u_ref[d, rsl, :] - jnp.dot(w_ref[d, rsl, :], sb, preferred_element_type=F32)
            vb = vnew.astype(BF16)
            o = (jnp.dot(qg_ref[d, rsl, :], sb, preferred_element_type=F32)
                 + jnp.dot(qkm_ref[d, rsl, :], vb, preferred_element_type=F32))
            s_new = s * eg_ref[d, c] + _dot_tn(kg_ref[d, rsl, :], vb)
            s_scr[d] = s_new
            if d == 0:
                of_scr[rsl, :] = o
            else:
                ob_scr[rsl, :] = o
            if not has_init:
                @pl.when(pos == (cps - 1 if d == 0 else 0))
                def _():
                    sfin_ref[c // cps, d] = s_new
        return carry

    lax.fori_loop(0, nchunk, body, 0)
    tot = of_scr[...] + ob_scr[...]
    o_ref[...] = _rms(tot) * na_ref[...] * _silu(z_ref[...])


def _gdn_scan(pre, proj, norm_a, s0, seq_len, step_rows):
    u, w, qg, kg, qkm, eg = pre
    n = proj.shape[0]
    nsteps = n // step_rows
    seq_per_step = step_rows // seq_len
    has_init = s0 is not None
    big = lambda width: pl.BlockSpec((2, None, step_rows, width), lambda s, h: (0, h, s, 0))
    in_specs = [big(D_HEAD), big(D_HEAD), big(D_HEAD), big(D_HEAD), big(GDN_CHUNK),
                pl.BlockSpec((2, None, step_rows // GDN_CHUNK, 1, LANE), lambda s, h: (0, h, s, 0, 0)),
                pl.BlockSpec((step_rows, LANE), lambda s, h: (s, U_ZA + h)),
                pl.BlockSpec((1, LANE), lambda s, h: (0, 0))]
    args = [u, w, qg, kg, qkm, eg, proj, norm_a]
    out_specs = [pl.BlockSpec((step_rows, LANE), lambda s, h: (s, h))]
    out_shape = [jax.ShapeDtypeStruct((n, W_MIX), F32)]
    if has_init:
        in_specs.append(pl.BlockSpec((None, 2, None, D_HEAD, D_HEAD), lambda s, h: (s, 0, h, 0, 0)))
        args.append(s0)
    else:
        out_specs.append(pl.BlockSpec((seq_per_step, 2, None, D_HEAD, D_HEAD), lambda s, h: (s, 0, h, 0, 0)))
        out_shape.append(jax.ShapeDtypeStruct((n // seq_len, 2, N_HEADS, D_HEAD, D_HEAD), F32))
    res = pl.pallas_call(
        functools.partial(_gdn_scan_kernel, chunks_per_seq=seq_len // GDN_CHUNK, has_init=has_init),
        grid=(nsteps, N_HEADS),
        in_specs=in_specs, out_specs=out_specs, out_shape=out_shape,
        scratch_shapes=[pltpu.VMEM((2, D_HEAD, D_HEAD), F32),
                        pltpu.VMEM((step_rows, D_HEAD), F32),
                        pltpu.VMEM((step_rows, D_HEAD), F32)],
        compiler_params=_cparams(("parallel", "parallel")),
        name="gdn_scan",
    )(*args)
    return res[0], (None if has_init else res[1])


def _s5_matrices(lam_re, lam_im, log_dt, b_re, b_im, c_re, c_im):
    t = S5_TILE
    step = jnp.exp(log_dt)[:, :, None]
    lr, li = lam_re * step, lam_im * step
    js = jnp.arange(t + 1, dtype=F32)[:, None, None, None]
    mag = jnp.exp(lr[None] * js)
    pw_re, pw_im = mag * jnp.cos(li[None] * js), mag * jnp.sin(li[None] * js)
    a_re, a_im = pw_re[1], pw_im[1]
    den = lam_re * lam_re + lam_im * lam_im
    z_re = ((a_re - 1.0) * lam_re + a_im * lam_im) / den
    z_im = (a_im * lam_re - (a_re - 1.0) * lam_im) / den
    zb_re = z_re[..., None] * b_re[None] - z_im[..., None] * b_im[None]
    zb_im = z_re[..., None] * b_im[None] + z_im[..., None] * b_re[None]
    w_re = pw_re[..., None] * zb_re[None] - pw_im[..., None] * zb_im[None]
    w_im = pw_re[..., None] * zb_im[None] + pw_im[..., None] * zb_re[None]
    taps = (jnp.einsum('gcp,jdgpi->jdgic', c_re, w_re[:t]) - jnp.einsum('gcp,jdgpi->jdgic', c_im, w_im[:t]))
    s_idx = jnp.arange(t)[:, None]
    t_idx = jnp.arange(t)[None, :]
    eye8 = jnp.eye(8, dtype=F32)
    nb = S5_BLOCKS

    def blocked(x, axis):
        return x.reshape(x.shape[:axis] + (nb, 8) + x.shape[axis + 1:])

    tm, gm, cm = [], [], []
    for d in range(2):
        lag = (t_idx - s_idx) if d == 0 else (s_idx - t_idx)
        kst = jnp.where((lag >= 0)[:, :, None, None, None], taps[jnp.clip(lag, 0, t - 1), d], 0.0)
        kst = blocked(kst, 2)
        tm.append(jnp.einsum('stbgic,gh->bsgithc', kst, eye8).reshape(nb, S5_W, S5_W))
        e_in = (t - 1 - jnp.arange(t)) if d == 0 else jnp.arange(t)
        ws = jnp.stack([w_re[e_in, d], w_im[e_in, d]], axis=2)
        ws = blocked(ws, 1)
        gm.append(jnp.einsum('sbgrpi,gh->bsgirhp', ws, eye8).reshape(nb, S5_W, S5_W))
        e_out = (jnp.arange(t) + 1) if d == 0 else (t - jnp.arange(t))
        pr, pi = pw_re[e_out, d], pw_im[e_out, d]
        wo_re = c_re[None] * pr[:, :, None, :] - c_im[None] * pi[:, :, None, :]
        wo_im = c_re[None] * pi[:, :, None, :] + c_im[None] * pr[:, :, None, :]
        wo = blocked(jnp.stack([wo_re, -wo_im], axis=0), 2)
        cm.append(jnp.einsum('rtbgcp,gh->brgpthc', wo, eye8).reshape(nb, S5_W, S5_W))
    a8r = pw_re[t].reshape(2, nb, 4, LANE)
    a8i = pw_im[t].reshape(2, nb, 4, LANE)
    a1 = jnp.concatenate([a8r, a8r], axis=2)
    a2 = jnp.concatenate([-a8i, a8i], axis=2)
    stack = lambda xs: jnp.stack(xs).astype(BF16)
    return stack(tm), stack(gm), stack(cm), a1, a2


def _s5_in_kernel(u_ref, gm_ref, g_ref):
    g_ref[...] = _dot(u_ref[...], gm_ref[...])


def _s5_tile_inputs(u4f, gm):
    nb, rows, _ = u4f.shape
    return pl.pallas_call(
        _s5_in_kernel,
        grid=(nb, 2),
        in_specs=[pl.BlockSpec((None, rows, S5_W), lambda b, d: (b, 0, 0)),
                  pl.BlockSpec((None, None, S5_W, S5_W), lambda b, d: (d, b, 0, 0))],
        out_specs=pl.BlockSpec((None, None, rows, S5_W), lambda b, d: (d, b, 0, 0)),
        out_shape=jax.ShapeDtypeStruct((2, nb, rows, S5_W), F32),
        compiler_params=_cparams(("parallel", "parallel")),
        name="s5_tile_inputs",
    )(u4f, gm)


def _s5_scan_kernel(*refs, nseq, rows_per_seq, has_init):
    g_ref, a1_ref, a2_ref = refs[:3]
    if has_init:
        x0_ref, xp_ref = refs[3:]
        fin_ref = None
    else:
        xp_ref, fin_ref = refs[3:]
    nblk = g_ref.shape[0]
    backward = pl.program_id(0) == 1
    chains = [(b, s) for b in range(nblk) for s in range(nseq)]

    def body(i, xs):
        n = jnp.where(backward, rows_per_seq - 1 - i, i)
        out = []
        for (b, s), x in zip(chains, xs):
            r = s * rows_per_seq + n
            xp_ref[b, r] = x
            out.append(a1_ref[b] * x + a2_ref[b] * pltpu.roll(x, 4, 0) + g_ref[b, r])
        return tuple(out)

    if has_init:
        init = tuple(x0_ref[b] for (b, s) in chains)
    else:
        init = tuple(jnp.zeros((8, LANE), F32) for _ in chains)
    xs = lax.fori_loop(0, rows_per_seq, body, init)
    if not has_init:
        for (b, s), x in zip(chains, xs):
            fin_ref[s] = x


def _s5_scan(g5, a1, a2, x0, seq_rows):
    _, nb, rows, _, _ = g5.shape
    nseq = rows // seq_rows
    if x0 is None:
        return pl.pallas_call(
            functools.partial(_s5_scan_kernel, nseq=nseq, rows_per_seq=seq_rows, has_init=False),
            grid=(2, nb),
            in_specs=[pl.BlockSpec((None, 1, rows, 8, LANE), lambda d, b: (d, b, 0, 0, 0)),
                      pl.BlockSpec((None, 1, 8, LANE), lambda d, b: (d, b, 0, 0)),
                      pl.BlockSpec((None, 1, 8, LANE), lambda d, b: (d, b, 0, 0))],
            out_specs=[pl.BlockSpec((None, 1, rows, 8, LANE), lambda d, b: (d, b, 0, 0, 0)),
                       pl.BlockSpec((None, None, nseq, 8, LANE), lambda d, b: (d, b, 0, 0, 0))],
            out_shape=[jax.ShapeDtypeStruct(g5.shape, F32),
                       jax.ShapeDtypeStruct((2, nb, nseq, 8, LANE), F32)],
            compiler_params=_cparams(("parallel", "parallel")),
            name="s5_scan_ctx",
        )(g5, a1, a2)
    xprev = pl.pallas_call(
        functools.partial(_s5_scan_kernel, nseq=1, rows_per_seq=seq_rows, has_init=True),
        grid=(2, nseq),
        in_specs=[pl.BlockSpec((None, nb, seq_rows, 8, LANE), lambda d, s: (d, 0, s, 0, 0)),
                  pl.BlockSpec((None, nb, 8, LANE), lambda d, s: (d, 0, 0, 0)),
                  pl.BlockSpec((None, nb, 8, LANE), lambda d, s: (d, 0, 0, 0)),
                  pl.BlockSpec((None, None, nb, 8, LANE), lambda d, s: (s, d, 0, 0, 0))],
        out_specs=pl.BlockSpec((None, nb, seq_rows, 8, LANE), lambda d, s: (d, 0, s, 0, 0)),
        out_shape=jax.ShapeDtypeStruct(g5.shape, F32),
        compiler_params=_cparams(("parallel", "parallel")),
        name="s5_scan_latent",
    )(g5, a1, a2, x0)
    return xprev, None


def _s5_out_kernel(u_ref, xp_ref, tm_ref, cm_ref, y_ref):
    u = u_ref[...].astype(BF16)
    acc = jnp.dot(u, tm_ref[0], preferred_element_type=F32)
    acc += jnp.dot(u, tm_ref[1], preferred_element_type=F32)
    acc += _dot(xp_ref[0], cm_ref[0])
    acc += _dot(xp_ref[1], cm_ref[1])
    y_ref[...] = acc


def _s5_outputs(u4f, xprev, tm, cm):
    nb, rows, _ = u4f.shape
    tr = 512
    return pl.pallas_call(
        _s5_out_kernel,
        grid=(nb, rows // tr),
        in_specs=[pl.BlockSpec((None, tr, S5_W), lambda b, r: (b, r, 0)),
                  pl.BlockSpec((2, None, tr, S5_W), lambda b, r: (0, b, r, 0)),
                  pl.BlockSpec((2, None, S5_W, S5_W), lambda b, r: (0, b, 0, 0)),
                  pl.BlockSpec((2, None, S5_W, S5_W), lambda b, r: (0, b, 0, 0))],
        out_specs=pl.BlockSpec((None, tr, S5_W), lambda b, r: (b, r, 0)),
        out_shape=jax.ShapeDtypeStruct((nb, rows, S5_W), F32),
        compiler_params=_cparams(("parallel", "parallel")),
        name="s5_outputs",
    )(u4f, xprev, tm, cm)


def _s5_mixer(u4, mats, x0, seq_len):
    tm, gm, cm, a1, a2 = mats
    nb, n, _ = u4.shape
    rows = n // S5_TILE
    u4f = u4.reshape(nb, rows, S5_W)
    g = _s5_tile_inputs(u4f, gm)
    xprev, fin = _s5_scan(g.reshape(2, nb, rows, 8, LANE), a1, a2, x0, seq_len // S5_TILE)
    yd = _s5_outputs(u4f, xprev.reshape(2, nb, rows, S5_W), tm, cm)
    return yd.reshape(nb, n, LANE), fin


def _ret_tables():
    h = jnp.arange(N_HEADS, dtype=F32)
    lg = jnp.log1p(-jnp.exp2(-5.0 - h))
    lgf, lgb = lg[:, None], lg[::-1][:, None]
    i = jnp.arange(SEG, dtype=F32)
    dist = i[:, None] - i[None, :]
    d_f = jnp.where(dist >= 0, jnp.exp(lgf[:, :, None] * jnp.maximum(dist, 0.0)), 0.0)
    d_b = jnp.where(dist <= 0, jnp.exp(lgb[:, :, None] * jnp.maximum(-dist, 0.0)), 0.0)
    dsum = d_f + d_b
    dec = jnp.stack([jnp.exp(lgf * (i + 1.0)), jnp.exp(lgf * (SEG - 1.0 - i)),
                     jnp.exp(lgb * (SEG - i)), jnp.exp(lgb * i)], axis=1)
    dec = jnp.broadcast_to(dec[..., None], (N_HEADS, 4, SEG, LANE))
    cd = jnp.stack([jnp.exp(lgf[:, 0] * SEG), jnp.exp(lgb[:, 0] * SEG)], axis=1)
    cd = jnp.broadcast_to(jnp.pad(cd, ((0, 0), (0, 6)))[..., None], (N_HEADS, 8, LANE))
    return dsum, dec, cd


def _rope_tables(n_tokens):
    n_rows = n_tokens // GRID_W
    rows = jnp.repeat(jnp.arange(n_rows), GRID_W).astype(F32)
    cols = jnp.tile(jnp.arange(GRID_W), n_rows).astype(F32)
    pairs = D_HEAD // 4
    freqs = ROPE_BASE ** (-jnp.arange(pairs, dtype=F32) / pairs)
    ang = jnp.concatenate([rows[:, None] * freqs, cols[:, None] * freqs], axis=-1)
    cos, sin = jnp.cos(ang), jnp.sin(ang)
    return jnp.concatenate([cos, cos], axis=-1), jnp.concatenate([-sin, sin], axis=-1)


def _ret_kernel(*refs, is_ctx):
    q_ref, k_ref, v_ref, g_ref, dsum_ref, dec_ref, cd_ref = refs[:7]
    if is_ctx:
        o_ref, sfin_ref = refs[7:]
    else:
        c2_ref, s2_ref, s0_ref, o_ref, sbin_scr = refs[7:]
    rows = q_ref.shape[0]
    nchunk = rows // SEG
    scale = D_HEAD ** -0.5

    def chunk_rows(c):
        return pl.ds(pl.multiple_of(c * SEG, SEG), SEG)

    def load_qk(rsl):
        q, k = q_ref[rsl, :], k_ref[rsl, :]
        if not is_ctx:
            c2, s2 = c2_ref[rsl, :], s2_ref[rsl, :]
            q = q * c2 + pltpu.roll(q, D_HEAD // 2, 1) * s2
            k = k * c2 + pltpu.roll(k, D_HEAD // 2, 1) * s2
        return q, k * scale

    def finish(rsl, r):
        o_ref[rsl, :] = _rms(r) * _silu(g_ref[rsl, :])

    if is_ctx:
        def body(c, carry):
            rsl = chunk_rows(c)
            q, k = load_qk(rsl)
            v = v_ref[rsl, :]
            finish(rsl, _dot(_dot_nt(q, k) * dsum_ref[...], v))
            sfin_ref[c, 0] = _dot_tn(k * dec_ref[1], v)
            sfin_ref[c, 1] = _dot_tn(k * dec_ref[3], v)
            return carry
        lax.fori_loop(0, nchunk, body, 0)
    else:
        cd_f, cd_b = cd_ref[0:1, :], cd_ref[1:2, :]

        def back(i, s_b):
            c = nchunk - 1 - i
            rsl = chunk_rows(c)
            _, k = load_qk(rsl)
            sbin_scr[c] = s_b
            return s_b * cd_b + _dot_tn(k * dec_ref[3], v_ref[rsl, :])
        lax.fori_loop(0, nchunk, back, s0_ref[1])

        def fwd(c, s_f):
            rsl = chunk_rows(c)
            q, k = load_qk(rsl)
            v = v_ref[rsl, :]
            r = (_dot(_dot_nt(q, k) * dsum_ref[...], v) + _dot(q * dec_ref[0], s_f)
                 + _dot(q * dec_ref[2], sbin_scr[c]))
            finish(rsl, r)
            return s_f * cd_f + _dot_tn(k * dec_ref[1], v)
        lax.fori_loop(0, nchunk, fwd, s0_ref[0])


def _retention(proj, tables, rope, s0, seq_len, step_rows):
    dsum, dec, cd = tables
    n = proj.shape[0]
    is_ctx = s0 is None
    blk = lambda unit: pl.BlockSpec((step_rows, LANE), lambda s, h, unit=unit: (s, unit + h))
    in_specs = [blk(U_QC), blk(U_KC), blk(U_VC), blk(U_GC),
                pl.BlockSpec((None, SEG, SEG), lambda s, h: (h, 0, 0)),
                pl.BlockSpec((None, 4, SEG, LANE), lambda s, h: (h, 0, 0, 0)),
                pl.BlockSpec((None, 8, LANE), lambda s, h: (h, 0, 0))]
    args = [proj, proj, proj, proj, dsum, dec, cd]
    out_specs = [pl.BlockSpec((step_rows, LANE), lambda s, h: (s, h))]
    out_shape = [jax.ShapeDtypeStruct((n, W_MIX), F32)]
    scratch = []
    if is_ctx:
        assert seq_len == SEG
        out_specs.append(pl.BlockSpec((step_rows // SEG, 2, None, D_HEAD, D_HEAD), lambda s, h: (s, 0, h, 0, 0)))
        out_shape.append(jax.ShapeDtypeStruct((n // SEG, 2, N_HEADS, D_HEAD, D_HEAD), F32))
    else:
        assert step_rows == seq_len
        in_specs += [pl.BlockSpec((step_rows, LANE), lambda s, h: (0, 0)),
                     pl.BlockSpec((step_rows, LANE), lambda s, h: (0, 0)),
                     pl.BlockSpec((None, 2, None, D_HEAD, D_HEAD), lambda s, h: (s, 0, h, 0, 0))]
        args += [rope[0], rope[1], s0]
        scratch = [pltpu.VMEM((step_rows // SEG, D_HEAD, D_HEAD), F32)]
    res = pl.pallas_call(
        functools.partial(_ret_kernel, is_ctx=is_ctx),
        grid=(n // step_rows, N_HEADS),
        in_specs=in_specs, out_specs=out_specs, out_shape=out_shape, scratch_shapes=scratch,
        compiler_params=_cparams(("parallel", "parallel")),
        name="retention",
    )(*args)
    return res[0], (res[1] if is_ctx else None)


def _merge_kernel(x_ref, oa_ref, yd_ref, u4_ref, oc_ref, gates_ref, mod_ref, d_ref, wglu_ref, bglu_ref,
                  wa_ref, wb_ref, wc_ref, wo_ref, o_ref):
    y = jnp.concatenate([u4_ref[b] * d_ref[b] + yd_ref[b] for b in range(S5_BLOCKS)], axis=1)
    y = _gelu_tanh(y)
    y = y * _sigmoid(_dot(y, wglu_ref[...]) + bglu_ref[...])
    gt = _sigmoid(gates_ref[...])
    merged = (gt[:, :D_MODEL] * _dot(oa_ref[...], wa_ref[...])
              + gt[:, D_MODEL:2 * D_MODEL] * _dot(y, wb_ref[...])
              + gt[:, 2 * D_MODEL:] * _dot(oc_ref[...], wc_ref[...]))
    o_ref[...] = x_ref[...] + mod_ref[2:3, :] * _dot(merged, wo_ref[...])


def _merge(x2d, o_a, yd, u4, o_c, proj, mod6, lw, rows_per_cond):
    n = x2d.shape[0]
    tm = 256
    row = _mod_row_map(tm, rows_per_cond)
    full = lambda shape: pl.BlockSpec(shape, lambda i: (0,) * len(shape))
    return pl.pallas_call(
        _merge_kernel,
        grid=(n // tm,),
        in_specs=[pl.BlockSpec((tm, D_MODEL), lambda i: (i, 0)),
                  pl.BlockSpec((tm, W_MIX), lambda i: (i, 0)),
                  pl.BlockSpec((S5_BLOCKS, tm, LANE), lambda i: (0, i, 0)),
                  pl.BlockSpec((S5_BLOCKS, tm, LANE), lambda i: (0, i, 0)),
                  pl.BlockSpec((tm, W_MIX), lambda i: (i, 0)),
                  pl.BlockSpec((tm, 3 * D_MODEL), lambda i: (i, 0)),
                  pl.BlockSpec((None, 6, D_MODEL), lambda i: (row(i), 0, 0)),
                  full((S5_BLOCKS, 1, LANE)), full((W_MIX, W_MIX)), full((1, W_MIX)),
                  full((W_MIX, D_MODEL)), full((W_MIX, D_MODEL)), full((W_MIX, D_MODEL)),
                  full((D_MODEL, D_MODEL))],
        out_specs=pl.BlockSpec((tm, D_MODEL), lambda i: (i, 0)),
        out_shape=jax.ShapeDtypeStruct((n, D_MODEL), F32),
        compiler_params=_cparams(("parallel",)),
        name="merge",
    )(x2d, o_a, yd, u4, o_c, proj, mod6, lw['ssm_d'], lw['w_glu'], lw['b_glu'],
      lw['w_br_a'], lw['w_br_b'], lw['w_br_c'], lw['w_o'])


def _ffn_up_kernel(x_ref, mod_ref, nw_ref, w_ref, o_ref, h_scr):
    @pl.when(pl.program_id(1) == 0)
    def _():
        h = _rms(x_ref[...]) * nw_ref[...]
        h_scr[...] = (h * (1.0 + mod_ref[4:5, :]) + mod_ref[3:4, :]).astype(BF16)
    o_ref[...] = jnp.dot(h_scr[...], w_ref[...], preferred_element_type=F32)


def _ffn_up(x2d, mod6, norm_w, w_up, rows_per_cond):
    n = x2d.shape[0]
    tm, tn = 512, 1408
    row = _mod_row_map(tm, rows_per_cond)
    return pl.pallas_call(
        _ffn_up_kernel,
        grid=(n // tm, (2 * D_FF) // tn),
        in_specs=[pl.BlockSpec((tm, D_MODEL), lambda i, j: (i, 0)),
                  pl.BlockSpec((None, 6, D_MODEL), lambda i, j: (row(i), 0, 0)),
                  pl.BlockSpec((1, D_MODEL), lambda i, j: (0, 0)),
                  pl.BlockSpec((D_MODEL, tn), lambda i, j: (0, j))],
        out_specs=pl.BlockSpec((tm, tn), lambda i, j: (i, j)),
        out_shape=jax.ShapeDtypeStruct((n, 2 * D_FF), F32),
        scratch_shapes=[pltpu.VMEM((tm, D_MODEL), BF16)],
        compiler_params=_cparams(("parallel", "arbitrary")),
        name="ffn_up",
    )(x2d, mod6, norm_w, w_up)


def _ffn_down_kernel(*refs, blocks_per_seq, final):
    has_halo = blocks_per_seq > 1
    hu_ref = refs[0]
    pos = 1
    if has_halo:
        prev_ref, next_ref = refs[1:3]
        pos = 3
    cw_ref, cb_ref, wd_ref, x_ref, mod_ref, fn_ref, o_ref = refs[pos:]
    tm = hu_ref.shape[0]
    hu = hu_ref[...]
    rowi = lax.broadcasted_iota(jnp.int32, hu.shape, 0)
    hm = pltpu.roll(hu, 1, 0)
    hp = pltpu.roll(hu, tm - 1, 0)
    if has_halo:
        p = pl.program_id(0) % blocks_per_seq
        prev = jnp.where(p > 0, prev_ref[7:8, :], 0.0)
        nxt = jnp.where(p < blocks_per_seq - 1, next_ref[0:1, :], 0.0)
    else:
        prev = jnp.zeros((1, hu.shape[1]), F32)
        nxt = prev
    hm = jnp.where(rowi == 0, prev, hm)
    hp = jnp.where(rowi == tm - 1, nxt, hp)
    c = cw_ref[0:1, :] * hm + cw_ref[1:2, :] * hu + cw_ref[2:3, :] * hp + cb_ref[...]
    act = _silu(c[:, :D_FF]) * c[:, D_FF:]
    x2 = x_ref[...] + mod_ref[5:6, :] * _dot(act, wd_ref[...])
    if final:
        x2 = _rms(x2) * fn_ref[...]
    o_ref[...] = x2


def _ffn_down(hu, x2d, mod6, lw, final_norm, seq_len, rows_per_cond, final):
    n = x2d.shape[0]
    tm = 256
    blocks_per_seq = seq_len // tm
    row = _mod_row_map(tm, rows_per_cond)
    in_specs = [pl.BlockSpec((tm, 2 * D_FF), lambda i: (i, 0))]
    args = [hu]
    if blocks_per_seq > 1:
        r8 = tm // 8
        last8 = n // 8 - 1
        in_specs += [pl.BlockSpec((8, 2 * D_FF), lambda i: (jnp.maximum(i * r8 - 1, 0), 0)),
                     pl.BlockSpec((8, 2 * D_FF), lambda i: (jnp.minimum((i + 1) * r8, last8), 0))]
        args += [hu, hu]
    full = lambda shape: pl.BlockSpec(shape, lambda i: (0,) * len(shape))
    in_specs += [full((3, 2 * D_FF)), full((1, 2 * D_FF)), full((D_FF, D_MODEL)),
                 pl.BlockSpec((tm, D_MODEL), lambda i: (i, 0)),
                 pl.BlockSpec((None, 6, D_MODEL), lambda i: (row(i), 0, 0)),
                 full((1, D_MODEL))]
    args += [lw['w_conv_ffn'], lw['b_conv_ffn'], lw['w_down'], x2d, mod6, final_norm]
    return pl.pallas_call(
        functools.partial(_ffn_down_kernel, blocks_per_seq=blocks_per_seq, final=final),
        grid=(n // tm,),
        in_specs=in_specs,
        out_specs=pl.BlockSpec((tm, D_MODEL), lambda i: (i, 0)),
        out_shape=jax.ShapeDtypeStruct((n, D_MODEL), F32),
        compiler_params=_cparams(("parallel",)),
        name="ffn_down",
    )(*args)


def _run_pass(x, mod, layers, consts, init_states, final_norm):
    b, seq_len, _ = x.shape
    n = b * seq_len
    is_ctx = init_states is None
    rows_per_cond = None if is_ctx else seq_len
    step_rows = 4096
    x2d = x.reshape(n, D_MODEL)
    depth = len(layers)
    finals = []
    for li, lw in enumerate(layers):
        mod6 = mod[li]
        proj, ab, u4 = _in_projection(x2d, mod6, lw['norm1'], lw['w_in_main'], lw['w_in_ab'], rows_per_cond)
        pre = _gdn_precompute(proj, ab, lw['w_conv_qkv'], lw['gate_par'], consts['gdn_masks'], consts['gdn_tri'],
                              seq_len)
        o_a, sd = _gdn_scan(pre, proj, lw['norm_a'], None if is_ctx else init_states[0][:, li], seq_len, step_rows)
        yd, sfin = _s5_mixer(u4, lw['s5_mats'], None if is_ctx else init_states[1][:, li], seq_len)
        o_c, sr = _retention(proj, consts['ret_tables'], consts.get('rope'),
                             None if is_ctx else init_states[2][:, li], seq_len, step_rows)
        x1 = _merge(x2d, o_a, yd, u4, o_c, proj, mod6, lw, rows_per_cond)
        hu = _ffn_up(x1, mod6, lw['norm2'], lw['w_up'], rows_per_cond)
        x2d = _ffn_down(hu, x1, mod6, lw, final_norm, seq_len, rows_per_cond, final=(li == depth - 1))
        finals.append((sd, sfin, sr))
    return x2d.reshape(b, seq_len, D_MODEL), finals


def _s5_state_to_rows(s_re, s_im):
    shp = s_re.shape[:3]
    re = s_re.reshape(shp + (S5_BLOCKS, 4, LANE))
    im = s_im.reshape(shp + (S5_BLOCKS, 4, LANE))
    return jnp.concatenate([re, im], axis=-2)


def _s5_rows_to_state(fin):
    nb = fin.shape[2]
    re = fin[:, :, :, 0:4, :].reshape(2, S5_BLOCKS, nb, 8, P_B)
    im = fin[:, :, :, 4:8, :].reshape(2, S5_BLOCKS, nb, 8, P_B)
    perm = lambda t: jnp.transpose(t, (2, 0, 1, 3, 4)).reshape(nb, 2, G_B, P_B)
    return perm(re), perm(im)


def kernel(x_prompt, x_sample, state_delta, state_ssm_re, state_ssm_im, state_ret, c, c_ctx,
           final_norm, norm1, norm2, w_mod, b_mod, w_in, w_conv_qkv, a_log, dt_bias, norm_a, w_br_a,
           ssm_lam_re, ssm_lam_im, ssm_log_dt, ssm_b_re, ssm_b_im, ssm_c_re, ssm_c_im, ssm_d,
           w_glu, b_glu, w_br_b, w_br_c, w_o, w_up, w_conv_ffn, b_conv_ffn, w_down):
    depth = w_in.shape[0]
    dec_b = x_sample.shape[0]
    n_qkv = 3 * W_MIX
    layers = []
    for i in range(depth):
        wi = w_in[i]
        gates_w = wi[:, -3 * D_MODEL:]
        rest = wi[:, n_qkv + 16:-3 * D_MODEL]
        w_main = jnp.concatenate([gates_w, wi[:, :n_qkv], rest], axis=1).astype(BF16)
        w_ab = jnp.pad(wi[:, n_qkv:n_qkv + 16], ((0, 0), (0, LANE - 16))).astype(BF16)
        gate_par = jnp.zeros((8, LANE), F32)
        gate_par = gate_par.at[0, :8].set(a_log[i].reshape(8)).at[1, :8].set(dt_bias[i].reshape(8))
        layers.append(dict(
            norm1=norm1[i][None], norm2=norm2[i][None], w_in_main=w_main, w_in_ab=w_ab,
            w_conv_qkv=w_conv_qkv[i], gate_par=gate_par, norm_a=norm_a[i][None],
            s5_mats=_s5_matrices(ssm_lam_re[i], ssm_lam_im[i], ssm_log_dt[i], ssm_b_re[i], ssm_b_im[i],
                                 ssm_c_re[i], ssm_c_im[i]),
            ssm_d=ssm_d[i].reshape(S5_BLOCKS, 1, LANE), w_glu=w_glu[i].astype(BF16), b_glu=b_glu[i][None],
            w_br_a=w_br_a[i].astype(BF16), w_br_b=w_br_b[i].astype(BF16), w_br_c=w_br_c[i].astype(BF16),
            w_o=w_o[i].astype(BF16), w_up=w_up[i].astype(BF16), w_conv_ffn=w_conv_ffn[i],
            b_conv_ffn=b_conv_ffn[i][None], w_down=w_down[i].astype(BF16)))
    masks = _gdn_masks()
    consts = dict(gdn_masks=masks, gdn_tri=masks[M_INCL:M_INCL + 2].astype(BF16), ret_tables=_ret_tables())
    fnorm = final_norm[None]

    cond8 = jnp.concatenate([c_ctx[None], c, jnp.zeros((8 - 1 - dec_b, D_MODEL), F32)], axis=0)
    mod = _modulation(cond8, w_mod, b_mod).reshape(depth, 8, 6, D_MODEL)

    y_prompt, ctx_finals = _run_pass(x_prompt, mod, layers, consts, None, fnorm)
    consts_lat = dict(consts, rope=_rope_tables(x_sample.shape[1]))
    x0_rows = _s5_state_to_rows(state_ssm_re, state_ssm_im)
    y_sample, _ = _run_pass(x_sample, mod, layers, consts_lat, (state_delta, x0_rows, state_ret), fnorm)

    new_delta = jnp.stack([f[0] for f in ctx_finals], axis=1)
    s5 = [_s5_rows_to_state(f[1]) for f in ctx_finals]
    new_re = jnp.stack([s[0] for s in s5], axis=1)
    new_im = jnp.stack([s[1] for s in s5], axis=1)
    new_ret = jnp.stack([f[2] for f in ctx_finals], axis=1)
    return (y_prompt, y_sample, new_delta, new_re, new_im, new_ret)
```

```python
import functools
import math

import jax
import jax.numpy as jnp
from jax import lax
from jax.experimental import pallas as pl
from jax.experimental.pallas import tpu as pltpu

F32 = jnp.float32
BF16 = jnp.bfloat16

D_MODEL = 1024
EPS = 1e-6
N_HEADS = 4
D_HEAD = 128
W_MIX = N_HEADS * D_HEAD
GDN_CHUNK = 64
SEG = 256
GRID_W = 64
ROPE_BASE = 10000.0
G_B, P_B, GC_B = 32, 64, 16
S5_TILE = 8
S5_BLOCKS = 4
S5_W = S5_TILE * 128
D_FF = 2816
LANE = 128
VMEM_LIMIT = 56 * 1024 * 1024

U_GATES = 0
U_QA, U_KA, U_VA, U_ZA = 24, 28, 32, 36
U_UB = 40
U_QC, U_KC, U_VC, U_GC = 44, 48, 52, 56
N_PROJ = 60 * LANE
PROJ_TN = 1280
UB_BLOCK = (U_UB * LANE) // PROJ_TN
UB_LOCAL = U_UB * LANE - UB_BLOCK * PROJ_TN


def _cparams(sem):
    return pltpu.CompilerParams(dimension_semantics=sem, vmem_limit_bytes=VMEM_LIMIT)


def _dot(a, b):
    return jnp.dot(a.astype(BF16), b.astype(BF16), preferred_element_type=F32)


def _dot_nt(a, b):
    return lax.dot_general(a.astype(BF16), b.astype(BF16), (((1,), (1,)), ((), ())),
                           preferred_element_type=F32)


def _dot_tn(a, b):
    return lax.dot_general(a.astype(BF16), b.astype(BF16), (((0,), (0,)), ((), ())),
                           preferred_element_type=F32)


def _sigmoid(x):
    return jax.nn.sigmoid(x)


def _silu(x):
    return x * _sigmoid(x)


def _softplus(x):
    return jnp.maximum(x, 0.0) + jnp.log1p(jnp.exp(-jnp.abs(x)))


def _gelu_tanh(x):
    return 0.5 * x * (1.0 + jnp.tanh(math.sqrt(2.0 / math.pi) * (x + 0.044715 * x * x * x)))


def _rms(x):
    return x * lax.rsqrt(jnp.mean(x * x, axis=-1, keepdims=True) + EPS)


def _split3(x):
    hi = x.astype(BF16).astype(F32)
    r1 = x - hi
    mid = r1.astype(BF16).astype(F32)
    lo = (r1 - mid).astype(BF16).astype(F32)
    return hi, mid, lo


def _mod_kernel(c_ref, w_ref, b_ref, o_ref):
    o_ref[...] = _dot(_silu(c_ref[...]), w_ref[...]) + b_ref[...]


def _modulation(cond8, w_mod, b_mod):
    depth, _, n = w_mod.shape
    tn = 1536
    return pl.pallas_call(
        _mod_kernel,
        grid=(depth, n // tn),
        in_specs=[pl.BlockSpec((8, D_MODEL), lambda l, j: (0, 0)),
                  pl.BlockSpec((None, D_MODEL, tn), lambda l, j: (l, 0, j)),
                  pl.BlockSpec((None, 1, tn), lambda l, j: (l, 0, j))],
        out_specs=pl.BlockSpec((None, 8, tn), lambda l, j: (l, 0, j)),
        out_shape=jax.ShapeDtypeStruct((depth, 8, n), F32),
        compiler_params=_cparams(("parallel", "parallel")),
        name="modulation",
    )(cond8, w_mod, b_mod.reshape(depth, 1, n))


def _mod_row_map(tm, rows_per_cond):
    if rows_per_cond is None:
        return lambda i: 0
    return lambda i: 1 + (i * tm) // rows_per_cond


def _inproj_kernel(x_ref, mod_ref, nw_ref, w_ref, wab_ref, proj_ref, ab_ref, u4_ref, h_scr, u_scr):
    j = pl.program_id(1)
    rows8 = u4_ref.shape[1]

    @pl.when(j == 0)
    def _():
        h = _rms(x_ref[...]) * nw_ref[...]
        h = h * (1.0 + mod_ref[1:2, :]) + mod_ref[0:1, :]
        hb = h.astype(BF16)
        h_scr[...] = hb
        ab_ref[...] = jnp.dot(hb, wab_ref[...], preferred_element_type=F32)

    acc = jnp.dot(h_scr[...], w_ref[...], preferred_element_type=F32)
    proj_ref[...] = acc

    @pl.when(j == UB_BLOCK)
    def _():
        for b in range(S5_BLOCKS):
            u_scr[b] = acc[:, UB_LOCAL + LANE * b:UB_LOCAL + LANE * (b + 1)]
        for b in range(S5_BLOCKS):
            for t in range(S5_TILE):
                u4_ref[b, :, LANE * t:LANE * (t + 1)] = u_scr[b, pl.ds(t, rows8, stride=S5_TILE), :]


def _in_projection(x2d, mod6, norm_w, w_main, w_ab, rows_per_cond):
    n = x2d.shape[0]
    tm, tn = 512, PROJ_TN
    row = _mod_row_map(tm, rows_per_cond)
    return pl.pallas_call(
        _inproj_kernel,
        grid=(n // tm, N_PROJ // tn),
        in_specs=[pl.BlockSpec((tm, D_MODEL), lambda i, j: (i, 0)),
                  pl.BlockSpec((None, 6, D_MODEL), lambda i, j: (row(i), 0, 0)),
                  pl.BlockSpec((1, D_MODEL), lambda i, j: (0, 0)),
                  pl.BlockSpec((D_MODEL, tn), lambda i, j: (0, j)),
                  pl.BlockSpec((D_MODEL, LANE), lambda i, j: (0, 0))],
        out_specs=[pl.BlockSpec((tm, tn), lambda i, j: (i, j)),
                   pl.BlockSpec((tm, LANE), lambda i, j: (i, 0)),
                   pl.BlockSpec((S5_BLOCKS, tm // S5_TILE, S5_W), lambda i, j: (0, i, 0))],
        out_shape=[jax.ShapeDtypeStruct((n, N_PROJ), F32),
                   jax.ShapeDtypeStruct((n, LANE), F32),
                   jax.ShapeDtypeStruct((S5_BLOCKS, n // S5_TILE, S5_W), F32)],
        scratch_shapes=[pltpu.VMEM((tm, D_MODEL), BF16), pltpu.VMEM((S5_BLOCKS, tm, LANE), F32)],
        compiler_params=_cparams(("parallel", "arbitrary")),
        name="in_projection",
    )(x2d, mod6, norm_w, w_main, w_ab)


M_INCL, M_STRICT, M_EYE, M_LEVEL = 0, 2, 4, 4
N_MASKS = 11


def _gdn_masks():
    r = jnp.arange(SEG)[:, None]
    c = jnp.arange(SEG)[None, :]
    same = (r // GDN_CHUNK) == (c // GDN_CHUNK)
    incl = [same & (r >= c), same & (r <= c)]
    strict = [same & (r > c), same & (r < c)]
    eye = [r == c]
    levels = [((r >> k) == (c >> k)) & ((r >> (k - 1)) != (c >> (k - 1))) for k in range(1, 7)]
    return jnp.stack(incl + strict + eye + levels).astype(F32)


def _gdn_pre_kernel(*refs, segs_per_seq):
    has_halo = segs_per_seq > 1
    q_ref, k_ref, v_ref = refs[:3]
    pos = 3
    halos = None
    if has_halo:
        halos = refs[3:9]
        pos = 9
    (ab_ref, wq_ref, wk_ref, wv_ref, gp_ref, msk_ref, tri_ref,
     u_ref, w_ref, qg_ref, kg_ref, qkm_ref, eg_ref) = refs[pos:]
    seg = pl.program_id(0)
    rowi = lax.broadcasted_iota(jnp.int32, (SEG, W_MIX), 0)
    lane = lax.broadcasted_iota(jnp.int32, (SEG, LANE), 1)

    def conv_act(x_ref, cw_ref, prev_ref, next_ref):
        x = x_ref[...]
        xm = pltpu.roll(x, 1, 0)
        xp = pltpu.roll(x, SEG - 1, 0)
        if has_halo:
            p = seg % segs_per_seq
            prev = jnp.where(p > 0, prev_ref[7:8, :], 0.0)
            nxt = jnp.where(p < segs_per_seq - 1, next_ref[0:1, :], 0.0)
        else:
            prev = jnp.zeros((1, W_MIX), F32)
            nxt = prev
        xm = jnp.where(rowi == 0, prev, xm)
        xp = jnp.where(rowi == SEG - 1, nxt, xp)
        return _silu(cw_ref[0:1, :] * xm + cw_ref[1:2, :] * x + cw_ref[2:3, :] * xp)

    hq = halos[0:2] if has_halo else (None, None)
    hk = halos[2:4] if has_halo else (None, None)
    hv = halos[4:6] if has_halo else (None, None)
    qa_all = conv_act(q_ref, wq_ref, *hq)
    ka_all = conv_act(k_ref, wk_ref, *hk)
    va_all = conv_act(v_ref, wv_ref, *hv)

    abv = ab_ref[...]
    log_a = -jnp.exp(gp_ref[0:1, :]) * _softplus(abv + gp_ref[1:2, :])
    beta_all = _sigmoid(abv)

    def col(x, idx):
        return jnp.sum(jnp.where(lane == idx, x, 0.0), axis=1, keepdims=True)

    heads = []
    for h in range(N_HEADS):
        hs = slice(h * D_HEAD, (h + 1) * D_HEAD)
        qa, ka = qa_all[:, hs], ka_all[:, hs]
        qn = qa * lax.rsqrt(jnp.sum(qa * qa, axis=-1, keepdims=True) + EPS) * (D_HEAD ** -0.5)
        kn = ka * lax.rsqrt(jnp.sum(ka * ka, axis=-1, keepdims=True) + EPS)
        heads.append(dict(qn=qn, kn=kn, va=va_all[:, hs], kk=_dot_nt(kn, kn), qk=_dot_nt(qn, kn)))

    chains = []
    for h in range(N_HEADS):
        for d in range(2):
            hd = heads[h]
            g = jnp.broadcast_to(col(log_a, d * N_HEADS + h), (SEG, LANE))
            beta = col(beta_all, 2 * N_HEADS + d * N_HEADS + h)
            ghi, gmid, glo = _split3(g)
            pieces = jnp.where(lane == 0, ghi, jnp.where(lane == 1, gmid, jnp.where(lane == 2, glo, 0.0)))
            cs = jnp.dot(tri_ref[d], pieces.astype(BF16), preferred_element_type=F32)
            gc = jnp.broadcast_to(jnp.sum(jnp.where(lane < 3, cs, 0.0), axis=1, keepdims=True), (SEG, LANE))
            chi, cmid, clo = _split3(gc)
            a_mat = jnp.where(lane == 0, chi, jnp.where(lane == 1, cmid, jnp.where(lane == 2, clo,
                              jnp.where(lane < 6, 1.0, 0.0))))
            b_mat = jnp.where(lane < 3, 1.0, jnp.where(lane == 3, -chi, jnp.where(lane == 4, -cmid,
                              jnp.where(lane == 5, -clo, 0.0))))
            e = jnp.exp(jnp.minimum(_dot_nt(a_mat, b_mat), 0.0))
            lmat = (beta * hd['kk']) * (e * msk_ref[M_STRICT + d])
            chains.append(dict(h=h, d=d, beta=beta, gc=gc, lmat=lmat,
                               qkm=hd['qk'] * (e * msk_ref[M_INCL + d]),
                               x=msk_ref[M_EYE] - lmat * msk_ref[M_LEVEL + 1]))

    for lvl in range(2, 7):
        for ch in chains:
            c = ch['lmat'] * msk_ref[M_LEVEL + lvl]
            ch['x'] = ch['x'] - _dot(ch['x'], _dot(c, ch['x']))

    for ch in chains:
        h, d, beta, gc = ch['h'], ch['d'], ch['beta'], ch['gc']
        hd = heads[h]
        qn, kn, va = hd['qn'], hd['kn'], hd['va']
        hs = slice(h * D_HEAD, (h + 1) * D_HEAD)
        egc = jnp.exp(gc)
        uw = _dot(ch['x'], jnp.concatenate([va * beta, kn * (beta * egc)], axis=1))
        u_ref[d, :, hs] = uw[:, :D_HEAD]
        w_ref[d, :, hs] = uw[:, D_HEAD:].astype(BF16)
        qg_ref[d, :, hs] = (qn * egc).astype(BF16)
        edge = GDN_CHUNK - 1 if d == 0 else 0
        for c4 in range(SEG // GDN_CHUNK):
            lo_r = c4 * GDN_CHUNK
            rs = slice(lo_r, lo_r + GDN_CHUNK)
            g_edge = gc[lo_r + edge:lo_r + edge + 1, :]
            kg_ref[d, rs, hs] = (kn[rs, :] * jnp.exp(g_edge - gc[rs, :])).astype(BF16)
            eg_ref[d, c4, h:h + 1, :] = jnp.exp(g_edge)
            qkm_ref[d, h, rs, :] = ch['qkm'][rs, rs].astype(BF16)


def _gdn_precompute(proj, ab, wconv, gate_par, masks, tri, seq_len):
    n = proj.shape[0]
    segs_per_seq = seq_len // SEG
    nseg = n // SEG
    units = (U_QA // N_HEADS, U_KA // N_HEADS, U_VA // N_HEADS)
    in_specs = [pl.BlockSpec((SEG, W_MIX), lambda s, cb=cb: (s, cb)) for cb in units]
    args = [proj, proj, proj]
    if segs_per_seq > 1:
        r8 = SEG // 8
        last8 = n // 8 - 1
        for cb in units:
            in_specs.append(pl.BlockSpec((8, W_MIX), lambda s, cb=cb: (jnp.maximum(s * r8 - 1, 0), cb)))
            in_specs.append(pl.BlockSpec((8, W_MIX), lambda s, cb=cb: (jnp.minimum((s + 1) * r8, last8), cb)))
            args += [proj, proj]
    in_specs += [pl.BlockSpec((SEG, LANE), lambda s: (s, 0))]
    in_specs += [pl.BlockSpec((3, W_MIX), lambda s, o=o: (0, o)) for o in range(3)]
    in_specs += [pl.BlockSpec((8, LANE), lambda s: (0, 0)),
                 pl.BlockSpec((N_MASKS, SEG, SEG), lambda s: (0, 0, 0)),
                 pl.BlockSpec((2, SEG, SEG), lambda s: (0, 0, 0))]
    args += [ab, wconv, wconv, wconv, gate_par, masks, tri]
    big = pl.BlockSpec((2, SEG, W_MIX), lambda s: (0, s, 0))
    cpseg = SEG // GDN_CHUNK
    return pl.pallas_call(
        functools.partial(_gdn_pre_kernel, segs_per_seq=segs_per_seq),
        grid=(nseg,),
        in_specs=in_specs,
        out_specs=[big, big, big, big,
                   pl.BlockSpec((2, N_HEADS, SEG, GDN_CHUNK), lambda s: (0, 0, s, 0)),
                   pl.BlockSpec((2, cpseg, N_HEADS, LANE), lambda s: (0, s, 0, 0))],
        out_shape=[jax.ShapeDtypeStruct((2, n, W_MIX), F32),
                   jax.ShapeDtypeStruct((2, n, W_MIX), BF16),
                   jax.ShapeDtypeStruct((2, n, W_MIX), BF16),
                   jax.ShapeDtypeStruct((2, n, W_MIX), BF16),
                   jax.ShapeDtypeStruct((2, N_HEADS, n, GDN_CHUNK), BF16),
                   jax.ShapeDtypeStruct((2, n // GDN_CHUNK, N_HEADS, LANE), F32)],
        compiler_params=_cparams(("parallel",)),
        name="gdn_precompute",
    )(*args)


GDN_SCAN_ROWS = 1024


def _gdn_scan_kernel(*refs, chunks_per_seq, has_init):
    ins = refs[:12]
    dir_refs = [ins[0::2], ins[1::2]]
    if has_init:
        s0_ref, of_ref, ob_ref, s_scr = refs[12:]
        fin_refs = None
    else:
        of_ref, ob_ref, sff_ref, sfb_ref, s_scr = refs[12:]
        fin_refs = (sff_ref, sfb_ref)
    o_refs = (of_ref, ob_ref)
    nchunk = of_ref.shape[0] // GDN_CHUNK
    cps = chunks_per_seq

    @pl.when(pl.program_id(1) == 0)
    def _():
        if has_init:
            s_scr[...] = s0_ref[...]
        else:
            s_scr[...] = jnp.zeros_like(s_scr)

    def body(i, carry):
        chains = []
        for h in range(N_HEADS):
            for d in range(2):
                c = i if d == 0 else nchunk - 1 - i
                chains.append(dict(h=h, d=d, c=c, hs=slice(h * D_HEAD, (h + 1) * D_HEAD),
                                   rsl=pl.ds(pl.multiple_of(c * GDN_CHUNK, GDN_CHUNK), GDN_CHUNK)))
        for ch in chains:
            s = s_scr[ch['d'], ch['h']]
            if not has_init:
                s = jnp.where(i % cps == 0, 0.0, s)
            ch['s'] = s
            ch['sb'] = s.astype(BF16)
        for ch in chains:
            u_ref, w_ref, qg_ref = dir_refs[ch['d']][:3]
            rsl, hs = ch['rsl'], ch['hs']
            ch['vb'] = (u_ref[rsl, hs] - jnp.dot(w_ref[rsl, hs], ch['sb'], preferred_element_type=F32)).astype(BF16)
            ch['o'] = jnp.dot(qg_ref[rsl, hs], ch['sb'], preferred_element_type=F32)
        for ch in chains:
            kg_ref, qkm_ref, eg_ref = dir_refs[ch['d']][3:]
            rsl, hs, h = ch['rsl'], ch['hs'], ch['h']
            ch['o'] = ch['o'] + jnp.dot(qkm_ref[h, rsl, :], ch['vb'], preferred_element_type=F32)
            ch['s_new'] = ch['s'] * eg_ref[ch['c']][h:h + 1, :] + _dot_tn(kg_ref[rsl, hs], ch['vb'])
        for ch in chains:
            s_scr[ch['d'], ch['h']] = ch['s_new']
            o_refs[ch['d']][ch['rsl'], ch['hs']] = ch['o']
        if not has_init:
            @pl.when(i % cps == cps - 1)
            def _():
                for ch in chains:
                    fin_refs[ch['d']][ch['c'] // cps, ch['h']] = ch['s_new']
        return carry

    lax.fori_loop(0, nchunk, body, 0)


def _gdn_scan(pre, s0, seq_len):
    n = pre[0].shape[1]
    br = GDN_SCAN_ROWS
    has_init = s0 is not None
    group_rows = seq_len if has_init else n
    ngroups, nblk = n // group_rows, group_rows // br
    fwd = lambda g, j: g * nblk + j
    bwd = lambda g, j: g * nblk + (nblk - 1 - j)
    in_specs, args = [], []
    for arr, kind in zip(pre, ("row", "row", "row", "row", "qkm", "eg")):
        for d, pos in ((0, fwd), (1, bwd)):
            if kind == "row":
                in_specs.append(pl.BlockSpec((None, br, W_MIX), lambda g, j, d=d, pos=pos: (d, pos(g, j), 0)))
            elif kind == "qkm":
                in_specs.append(pl.BlockSpec((None, N_HEADS, br, GDN_CHUNK),
                                             lambda g, j, d=d, pos=pos: (d, 0, pos(g, j), 0)))
            else:
                in_specs.append(pl.BlockSpec((None, br // GDN_CHUNK, N_HEADS, LANE),
                                             lambda g, j, d=d, pos=pos: (d, pos(g, j), 0, 0)))
            args.append(arr)
    out_specs = [pl.BlockSpec((br, W_MIX), lambda g, j: (fwd(g, j), 0)),
                 pl.BlockSpec((br, W_MIX), lambda g, j: (bwd(g, j), 0))]
    out_shape = [jax.ShapeDtypeStruct((n, W_MIX), F32), jax.ShapeDtypeStruct((n, W_MIX), F32)]
    if has_init:
        in_specs.append(pl.BlockSpec((None, 2, N_HEADS, D_HEAD, D_HEAD), lambda g, j: (g, 0, 0, 0, 0)))
        args.append(s0)
    else:
        spb = br // seq_len
        fin_shape = jax.ShapeDtypeStruct((n // seq_len, N_HEADS, D_HEAD, D_HEAD), F32)
        out_specs += [pl.BlockSpec((spb, N_HEADS, D_HEAD, D_HEAD), lambda g, j: (fwd(g, j), 0, 0, 0)),
                      pl.BlockSpec((spb, N_HEADS, D_HEAD, D_HEAD), lambda g, j: (bwd(g, j), 0, 0, 0))]
        out_shape += [fin_shape, fin_shape]
    res = pl.pallas_call(
        functools.partial(_gdn_scan_kernel, chunks_per_seq=seq_len // GDN_CHUNK, has_init=has_init),
        grid=(ngroups, nblk),
        in_specs=in_specs, out_specs=out_specs, out_shape=out_shape,
        scratch_shapes=[pltpu.VMEM((2, N_HEADS, D_HEAD, D_HEAD), F32)],
        compiler_params=_cparams(("parallel", "arbitrary")),
        name="gdn_scan",
    )(*args)
    fin = None if has_init else jnp.stack([res[2], res[3]], axis=1)
    return res[0], res[1], fin


def _s5_matrices(lam_re, lam_im, log_dt, b_re, b_im, c_re, c_im):
    t = S5_TILE
    step = jnp.exp(log_dt)[:, :, None]
    lr, li = lam_re * step, lam_im * step
    js = jnp.arange(t + 1, dtype=F32)[:, None, None, None]
    mag = jnp.exp(lr[None] * js)
    pw_re, pw_im = mag * jnp.cos(li[None] * js), mag * jnp.sin(li[None] * js)
    a_re, a_im = pw_re[1], pw_im[1]
    den = lam_re * lam_re + lam_im * lam_im
    z_re = ((a_re - 1.0) * lam_re + a_im * lam_im) / den
    z_im = (a_im * lam_re - (a_re - 1.0) * lam_im) / den
    zb_re = z_re[..., None] * b_re[None] - z_im[..., None] * b_im[None]
    zb_im = z_re[..., None] * b_im[None] + z_im[..., None] * b_re[None]
    w_re = pw_re[..., None] * zb_re[None] - pw_im[..., None] * zb_im[None]
    w_im = pw_re[..., None] * zb_im[None] + pw_im[..., None] * zb_re[None]
    taps = (jnp.einsum('gcp,jdgpi->jdgic', c_re, w_re[:t]) - jnp.einsum('gcp,jdgpi->jdgic', c_im, w_im[:t]))
    s_idx = jnp.arange(t)[:, None]
    t_idx = jnp.arange(t)[None, :]
    eye8 = jnp.eye(8, dtype=F32)
    nb = S5_BLOCKS

    def blocked(x, axis):
        return x.reshape(x.shape[:axis] + (nb, 8) + x.shape[axis + 1:])

    tm, gm, cm = [], [], []
    for d in range(2):
        lag = (t_idx - s_idx) if d == 0 else (s_idx - t_idx)
        kst = jnp.where((lag >= 0)[:, :, None, None, None], taps[jnp.clip(lag, 0, t - 1), d], 0.0)
        kst = blocked(kst, 2)
        tm.append(jnp.einsum('stbgic,gh->bsgithc', kst, eye8).reshape(nb, S5_W, S5_W))
        e_in = (t - 1 - jnp.arange(t)) if d == 0 else jnp.arange(t)
        ws = jnp.stack([w_re[e_in, d], w_im[e_in, d]], axis=2)
        ws = blocked(ws, 1)
        gm.append(jnp.einsum('sbgrpi,gh->bsgirhp', ws, eye8).reshape(nb, S5_W, S5_W))
        e_out = (jnp.arange(t) + 1) if d == 0 else (t - jnp.arange(t))
        pr, pi = pw_re[e_out, d], pw_im[e_out, d]
        wo_re = c_re[None] * pr[:, :, None, :] - c_im[None] * pi[:, :, None, :]
        wo_im = c_re[None] * pi[:, :, None, :] + c_im[None] * pr[:, :, None, :]
        wo = blocked(jnp.stack([wo_re, -wo_im], axis=0), 2)
        cm.append(jnp.einsum('rtbgcp,gh->brgpthc', wo, eye8).reshape(nb, S5_W, S5_W))
    a8r = pw_re[t].reshape(2, nb, 4, LANE)
    a8i = pw_im[t].reshape(2, nb, 4, LANE)
    a1 = jnp.concatenate([a8r, a8r], axis=2)
    a2 = jnp.concatenate([-a8i, a8i], axis=2)
    stack = lambda xs: jnp.stack(xs).astype(BF16)
    return stack(tm), stack(gm), stack(cm), a1, a2


def _s5_in_kernel(u_ref, gm_ref, g_ref):
    rows = u_ref.shape[0]
    g = _dot(u_ref[...], gm_ref[...])
    for s in range(8):
        g_ref[pl.ds(s, rows, stride=8), :] = g[:, LANE * s:LANE * (s + 1)]


def _s5_tile_inputs(u4f, gm):
    nb, rows, _ = u4f.shape
    return pl.pallas_call(
        _s5_in_kernel,
        grid=(nb, 2),
        in_specs=[pl.BlockSpec((None, rows, S5_W), lambda b, d: (b, 0, 0)),
                  pl.BlockSpec((None, None, S5_W, S5_W), lambda b, d: (d, b, 0, 0))],
        out_specs=pl.BlockSpec((None, None, rows * 8, LANE), lambda b, d: (d, b, 0, 0)),
        out_shape=jax.ShapeDtypeStruct((2, nb, rows * 8, LANE), F32),
        compiler_params=_cparams(("parallel", "parallel")),
        name="s5_tile_inputs",
    )(u4f, gm)


def _s5_scan_kernel(*refs, nseq, rows_per_seq, has_init):
    g_ref, a1_ref, a2_ref = refs[:3]
    if has_init:
        x0_ref, xp_ref = refs[3:]
        fin_ref = None
    else:
        xp_ref, fin_ref = refs[3:]
    nblk = g_ref.shape[0]
    backward = pl.program_id(0) == 1
    chains = [(b, s) for b in range(nblk) for s in range(nseq)]

    def body(i, xs):
        n = jnp.where(backward, rows_per_seq - 1 - i, i)
        out = []
        for (b, s), x in zip(chains, xs):
            r = s * rows_per_seq + n
            xp_ref[b, r] = x
            out.append(a1_ref[b] * x + a2_ref[b] * pltpu.roll(x, 4, 0) + g_ref[b, r])
        return tuple(out)

    if has_init:
        init = tuple(x0_ref[b] for (b, s) in chains)
    else:
        init = tuple(jnp.zeros((8, LANE), F32) for _ in chains)
    xs = lax.fori_loop(0, rows_per_seq, body, init)
    if not has_init:
        for (b, s), x in zip(chains, xs):
            fin_ref[s] = x


def _s5_scan(g5, a1, a2, x0, seq_rows):
    _, nb, rows, _, _ = g5.shape
    nseq = rows // seq_rows
    if x0 is None:
        return pl.pallas_call(
            functools.partial(_s5_scan_kernel, nseq=nseq, rows_per_seq=seq_rows, has_init=False),
            grid=(2, nb),
            in_specs=[pl.BlockSpec((None, 1, rows, 8, LANE), lambda d, b: (d, b, 0, 0, 0)),
                      pl.BlockSpec((None, 1, 8, LANE), lambda d, b: (d, b, 0, 0)),
                      pl.BlockSpec((None, 1, 8, LANE), lambda d, b: (d, b, 0, 0))],
            out_specs=[pl.BlockSpec((None, 1, rows, 8, LANE), lambda d, b: (d, b, 0, 0, 0)),
                       pl.BlockSpec((None, None, nseq, 8, LANE), lambda d, b: (d, b, 0, 0, 0))],
            out_shape=[jax.ShapeDtypeStruct(g5.shape, F32),
                       jax.ShapeDtypeStruct((2, nb, nseq, 8, LANE), F32)],
            compiler_params=_cparams(("parallel", "parallel")),
            name="s5_scan_ctx",
        )(g5, a1, a2)
    xprev = pl.pallas_call(
        functools.partial(_s5_scan_kernel, nseq=1, rows_per_seq=seq_rows, has_init=True),
        grid=(2, nseq),
        in_specs=[pl.BlockSpec((None, nb, seq_rows, 8, LANE), lambda d, s: (d, 0, s, 0, 0)),
                  pl.BlockSpec((None, nb, 8, LANE), lambda d, s: (d, 0, 0, 0)),
                  pl.BlockSpec((None, nb, 8, LANE), lambda d, s: (d, 0, 0, 0)),
                  pl.BlockSpec((None, None, nb, 8, LANE), lambda d, s: (s, d, 0, 0, 0))],
        out_specs=pl.BlockSpec((None, nb, seq_rows, 8, LANE), lambda d, s: (d, 0, s, 0, 0)),
        out_shape=jax.ShapeDtypeStruct(g5.shape, F32),
        compiler_params=_cparams(("parallel", "parallel")),
        name="s5_scan_latent",
    )(g5, a1, a2, x0)
    return xprev, None


def _s5_out_kernel(u_ref, xp_ref, tm_ref, cm_ref, y_ref):
    tr = u_ref.shape[0]
    u = u_ref[...].astype(BF16)
    acc = jnp.dot(u, tm_ref[0], preferred_element_type=F32)
    acc += jnp.dot(u, tm_ref[1], preferred_element_type=F32)
    for d in range(2):
        xp = jnp.concatenate([xp_ref[d, pl.ds(s, tr, stride=8), :] for s in range(8)], axis=1)
        acc += _dot(xp, cm_ref[d])
    for t in range(S5_TILE):
        y_ref[pl.ds(t, tr, stride=S5_TILE), :] = acc[:, LANE * t:LANE * (t + 1)]


def _s5_outputs(u4f, xprev, tm, cm):
    nb, rows, _ = u4f.shape
    tr = 512
    return pl.pallas_call(
        _s5_out_kernel,
        grid=(nb, rows // tr),
        in_specs=[pl.BlockSpec((None, tr, S5_W), lambda b, r: (b, r, 0)),
                  pl.BlockSpec((2, None, tr * 8, LANE), lambda b, r: (0, b, r, 0)),
                  pl.BlockSpec((2, None, S5_W, S5_W), lambda b, r: (0, b, 0, 0)),
                  pl.BlockSpec((2, None, S5_W, S5_W), lambda b, r: (0, b, 0, 0))],
        out_specs=pl.BlockSpec((tr * S5_TILE, LANE), lambda b, r: (r, b)),
        out_shape=jax.ShapeDtypeStruct((rows * S5_TILE, S5_BLOCKS * LANE), F32),
        compiler_params=_cparams(("parallel", "parallel")),
        name="s5_outputs",
    )(u4f, xprev, tm, cm)


def _s5_mixer(u4f, mats, x0, seq_len):
    tm, gm, cm, a1, a2 = mats
    nb, rows, _ = u4f.shape
    g = _s5_tile_inputs(u4f, gm)
    xprev, fin = _s5_scan(g.reshape(2, nb, rows, 8, LANE), a1, a2, x0, seq_len // S5_TILE)
    yd = _s5_outputs(u4f, xprev.reshape(2, nb, rows * 8, LANE), tm, cm)
    return yd, fin


def _ret_tables():
    h = jnp.arange(N_HEADS, dtype=F32)
    lg = jnp.log1p(-jnp.exp2(-5.0 - h))
    lgf, lgb = lg[:, None], lg[::-1][:, None]
    i = jnp.arange(SEG, dtype=F32)
    dist = i[:, None] - i[None, :]
    d_f = jnp.where(dist >= 0, jnp.exp(lgf[:, :, None] * jnp.maximum(dist, 0.0)), 0.0)
    d_b = jnp.where(dist <= 0, jnp.exp(lgb[:, :, None] * jnp.maximum(-dist, 0.0)), 0.0)
    dsum = d_f + d_b
    dec = jnp.stack([jnp.exp(lgf * (i + 1.0)), jnp.exp(lgf * (SEG - 1.0 - i)),
                     jnp.exp(lgb * (SEG - i)), jnp.exp(lgb * i)], axis=1)
    dec = jnp.broadcast_to(dec[..., None], (N_HEADS, 4, SEG, LANE))
    cd = jnp.stack([jnp.exp(lgf[:, 0] * SEG), jnp.exp(lgb[:, 0] * SEG)], axis=1)
    cd = jnp.broadcast_to(jnp.pad(cd, ((0, 0), (0, 6)))[..., None], (N_HEADS, 8, LANE))
    return dsum, dec, cd


def _rope_tables(n_tokens):
    n_rows = n_tokens // GRID_W
    rows = jnp.repeat(jnp.arange(n_rows), GRID_W).astype(F32)
    cols = jnp.tile(jnp.arange(GRID_W), n_rows).astype(F32)
    pairs = D_HEAD // 4
    freqs = ROPE_BASE ** (-jnp.arange(pairs, dtype=F32) / pairs)
    ang = jnp.concatenate([rows[:, None] * freqs, cols[:, None] * freqs], axis=-1)
    cos, sin = jnp.cos(ang), jnp.sin(ang)
    return jnp.concatenate([cos, cos], axis=-1), jnp.concatenate([-sin, sin], axis=-1)


def _ret_kernel(*refs, is_ctx):
    q_ref, k_ref, v_ref, g_ref, dsum_ref, dec_ref, cd_ref = refs[:7]
    if is_ctx:
        o_ref, sfin_ref = refs[7:]
    else:
        c2_ref, s2_ref, s0_ref, o_ref, sbin_scr = refs[7:]
    rows = q_ref.shape[0]
    nchunk = rows // SEG
    scale = D_HEAD ** -0.5

    def chunk_rows(c):
        return pl.ds(pl.multiple_of(c * SEG, SEG), SEG)

    def load_qk(rsl):
        q, k = q_ref[rsl, :], k_ref[rsl, :]
        if not is_ctx:
            c2, s2 = c2_ref[rsl, :], s2_ref[rsl, :]
            q = q * c2 + pltpu.roll(q, D_HEAD // 2, 1) * s2
            k = k * c2 + pltpu.roll(k, D_HEAD // 2, 1) * s2
        return q, k * scale

    def finish(rsl, r):
        o_ref[rsl, :] = _rms(r) * _silu(g_ref[rsl, :])

    if is_ctx:
        def body(c, carry):
            rsl = chunk_rows(c)
            q, k = load_qk(rsl)
            v = v_ref[rsl, :]
            finish(rsl, _dot(_dot_nt(q, k) * dsum_ref[...], v))
            sfin_ref[c, 0] = _dot_tn(k * dec_ref[1], v)
            sfin_ref[c, 1] = _dot_tn(k * dec_ref[3], v)
            return carry
        lax.fori_loop(0, nchunk, body, 0)
    else:
        cd_f, cd_b = cd_ref[0:1, :], cd_ref[1:2, :]

        def back(i, s_b):
            c = nchunk - 1 - i
            rsl = chunk_rows(c)
            _, k = load_qk(rsl)
            sbin_scr[c] = s_b
            return s_b * cd_b + _dot_tn(k * dec_ref[3], v_ref[rsl, :])
        lax.fori_loop(0, nchunk, back, s0_ref[1])

        def fwd(c, s_f):
            rsl = chunk_rows(c)
            q, k = load_qk(rsl)
            v = v_ref[rsl, :]
            r = (_dot(_dot_nt(q, k) * dsum_ref[...], v) + _dot(q * dec_ref[0], s_f)
                 + _dot(q * dec_ref[2], sbin_scr[c]))
            finish(rsl, r)
            return s_f * cd_f + _dot_tn(k * dec_ref[1], v)
        lax.fori_loop(0, nchunk, fwd, s0_ref[0])


def _retention(proj, tables, rope, s0, seq_len, step_rows):
    dsum, dec, cd = tables
    n = proj.shape[0]
    is_ctx = s0 is None
    blk = lambda unit: pl.BlockSpec((step_rows, LANE), lambda s, h, unit=unit: (s, unit + h))
    in_specs = [blk(U_QC), blk(U_KC), blk(U_VC), blk(U_GC),
                pl.BlockSpec((None, SEG, SEG), lambda s, h: (h, 0, 0)),
                pl.BlockSpec((None, 4, SEG, LANE), lambda s, h: (h, 0, 0, 0)),
                pl.BlockSpec((None, 8, LANE), lambda s, h: (h, 0, 0))]
    args = [proj, proj, proj, proj, dsum, dec, cd]
    out_specs = [pl.BlockSpec((step_rows, LANE), lambda s, h: (s, h))]
    out_shape = [jax.ShapeDtypeStruct((n, W_MIX), F32)]
    scratch = []
    if is_ctx:
        assert seq_len == SEG
        out_specs.append(pl.BlockSpec((step_rows // SEG, 2, None, D_HEAD, D_HEAD), lambda s, h: (s, 0, h, 0, 0)))
        out_shape.append(jax.ShapeDtypeStruct((n // SEG, 2, N_HEADS, D_HEAD, D_HEAD), F32))
    else:
        assert step_rows == seq_len
        in_specs += [pl.BlockSpec((step_rows, LANE), lambda s, h: (0, 0)),
                     pl.BlockSpec((step_rows, LANE), lambda s, h: (0, 0)),
                     pl.BlockSpec((None, 2, None, D_HEAD, D_HEAD), lambda s, h: (s, 0, h, 0, 0))]
        args += [rope[0], rope[1], s0]
        scratch = [pltpu.VMEM((step_rows // SEG, D_HEAD, D_HEAD), F32)]
    res = pl.pallas_call(
        functools.partial(_ret_kernel, is_ctx=is_ctx),
        grid=(n // step_rows, N_HEADS),
        in_specs=in_specs, out_specs=out_specs, out_shape=out_shape, scratch_shapes=scratch,
        compiler_params=_cparams(("parallel", "parallel")),
        name="retention",
    )(*args)
    return res[0], (res[1] if is_ctx else None)


def _merge_kernel(x_ref, of_ref, ob_ref, z_ref, na_ref, yd_ref, u_ref, oc_ref, gates_ref, mod_ref, d_ref,
                  wglu_ref, bglu_ref, wa_ref, wb_ref, wc_ref, wo_ref, o_ref):
    tot = of_ref[...] + ob_ref[...]
    o_a = jnp.concatenate([_rms(tot[:, h * D_HEAD:(h + 1) * D_HEAD]) * na_ref[...] for h in range(N_HEADS)],
                          axis=1) * _silu(z_ref[...])
    y = _gelu_tanh(u_ref[...] * d_ref[...] + yd_ref[...])
    y = y * _sigmoid(_dot(y, wglu_ref[...]) + bglu_ref[...])
    gt = _sigmoid(gates_ref[...])
    merged = (gt[:, :D_MODEL] * _dot(o_a, wa_ref[...])
              + gt[:, D_MODEL:2 * D_MODEL] * _dot(y, wb_ref[...])
              + gt[:, 2 * D_MODEL:] * _dot(oc_ref[...], wc_ref[...]))
    o_ref[...] = x_ref[...] + mod_ref[2:3, :] * _dot(merged, wo_ref[...])


def _merge(x2d, o_f, o_b, yd, o_c, proj, mod6, lw, rows_per_cond):
    n = x2d.shape[0]
    tm = 256
    row = _mod_row_map(tm, rows_per_cond)
    full = lambda shape: pl.BlockSpec(shape, lambda i: (0,) * len(shape))
    mix = lambda cb: pl.BlockSpec((tm, W_MIX), lambda i, cb=cb: (i, cb))
    return pl.pallas_call(
        _merge_kernel,
        grid=(n // tm,),
        in_specs=[pl.BlockSpec((tm, D_MODEL), lambda i: (i, 0)),
                  mix(0), mix(0), mix(U_ZA // N_HEADS), full((1, D_HEAD)),
                  mix(0), mix(U_UB // N_HEADS), mix(0),
                  pl.BlockSpec((tm, 3 * D_MODEL), lambda i: (i, 0)),
                  pl.BlockSpec((None, 6, D_MODEL), lambda i: (row(i), 0, 0)),
                  full((1, W_MIX)), full((W_MIX, W_MIX)), full((1, W_MIX)),
                  full((W_MIX, D_MODEL)), full((W_MIX, D_MODEL)), full((W_MIX, D_MODEL)),
                  full((D_MODEL, D_MODEL))],
        out_specs=pl.BlockSpec((tm, D_MODEL), lambda i: (i, 0)),
        out_shape=jax.ShapeDtypeStruct((n, D_MODEL), F32),
        compiler_params=_cparams(("parallel",)),
        name="merge",
    )(x2d, o_f, o_b, proj, lw['norm_a'], yd, proj, o_c, proj, mod6, lw['ssm_d'], lw['w_glu'], lw['b_glu'],
      lw['w_br_a'], lw['w_br_b'], lw['w_br_c'], lw['w_o'])


def _ffn_up_kernel(x_ref, mod_ref, nw_ref, w_ref, o_ref, h_scr):
    @pl.when(pl.program_id(1) == 0)
    def _():
        h = _rms(x_ref[...]) * nw_ref[...]
        h_scr[...] = (h * (1.0 + mod_ref[4:5, :]) + mod_ref[3:4, :]).astype(BF16)
    o_ref[...] = jnp.dot(h_scr[...], w_ref[...], preferred_element_type=F32)


def _ffn_up(x2d, mod6, norm_w, w_up, rows_per_cond):
    n = x2d.shape[0]
    tm, tn = 512, 1408
    row = _mod_row_map(tm, rows_per_cond)
    return pl.pallas_call(
        _ffn_up_kernel,
        grid=(n // tm, (2 * D_FF) // tn),
        in_specs=[pl.BlockSpec((tm, D_MODEL), lambda i, j: (i, 0)),
                  pl.BlockSpec((None, 6, D_MODEL), lambda i, j: (row(i), 0, 0)),
                  pl.BlockSpec((1, D_MODEL), lambda i, j: (0, 0)),
                  pl.BlockSpec((D_MODEL, tn), lambda i, j: (0, j))],
        out_specs=pl.BlockSpec((tm, tn), lambda i, j: (i, j)),
        out_shape=jax.ShapeDtypeStruct((n, 2 * D_FF), F32),
        scratch_shapes=[pltpu.VMEM((tm, D_MODEL), BF16)],
        compiler_params=_cparams(("parallel", "arbitrary")),
        name="ffn_up",
    )(x2d, mod6, norm_w, w_up)


def _ffn_down_kernel(*refs, blocks_per_seq, final):
    has_halo = blocks_per_seq > 1
    hu_ref = refs[0]
    pos = 1
    if has_halo:
        prev_ref, next_ref = refs[1:3]
        pos = 3
    cw_ref, cb_ref, wd_ref, x_ref, mod_ref, fn_ref, o_ref = refs[pos:]
    tm = hu_ref.shape[0]
    hu = hu_ref[...]
    rowi = lax.broadcasted_iota(jnp.int32, hu.shape, 0)
    hm = pltpu.roll(hu, 1, 0)
    hp = pltpu.roll(hu, tm - 1, 0)
    if has_halo:
        p = pl.program_id(0) % blocks_per_seq
        prev = jnp.where(p > 0, prev_ref[7:8, :], 0.0)
        nxt = jnp.where(p < blocks_per_seq - 1, next_ref[0:1, :], 0.0)
    else:
        prev = jnp.zeros((1, hu.shape[1]), F32)
        nxt = prev
    hm = jnp.where(rowi == 0, prev, hm)
    hp = jnp.where(rowi == tm - 1, nxt, hp)
    c = cw_ref[0:1, :] * hm + cw_ref[1:2, :] * hu + cw_ref[2:3, :] * hp + cb_ref[...]
    act = _silu(c[:, :D_FF]) * c[:, D_FF:]
    x2 = x_ref[...] + mod_ref[5:6, :] * _dot(act, wd_ref[...])
    if final:
        x2 = _rms(x2) * fn_ref[...]
    o_ref[...] = x2


def _ffn_down(hu, x2d, mod6, lw, final_norm, seq_len, rows_per_cond, final):
    n = x2d.shape[0]
    tm = 256
    blocks_per_seq = seq_len // tm
    row = _mod_row_map(tm, rows_per_cond)
    in_specs = [pl.BlockSpec((tm, 2 * D_FF), lambda i: (i, 0))]
    args = [hu]
    if blocks_per_seq > 1:
        r8 = tm // 8
        last8 = n // 8 - 1
        in_specs += [pl.BlockSpec((8, 2 * D_FF), lambda i: (jnp.maximum(i * r8 - 1, 0), 0)),
                     pl.BlockSpec((8, 2 * D_FF), lambda i: (jnp.minimum((i + 1) * r8, last8), 0))]
        args += [hu, hu]
    full = lambda shape: pl.BlockSpec(shape, lambda i: (0,) * len(shape))
    in_specs += [full((3, 2 * D_FF)), full((1, 2 * D_FF)), full((D_FF, D_MODEL)),
                 pl.BlockSpec((tm, D_MODEL), lambda i: (i, 0)),
                 pl.BlockSpec((None, 6, D_MODEL), lambda i: (row(i), 0, 0)),
                 full((1, D_MODEL))]
    args += [lw['w_conv_ffn'], lw['b_conv_ffn'], lw['w_down'], x2d, mod6, final_norm]
    return pl.pallas_call(
        functools.partial(_ffn_down_kernel, blocks_per_seq=blocks_per_seq, final=final),
        grid=(n // tm,),
        in_specs=in_specs,
        out_specs=pl.BlockSpec((tm, D_MODEL), lambda i: (i, 0)),
        out_shape=jax.ShapeDtypeStruct((n, D_MODEL), F32),
        compiler_params=_cparams(("parallel",)),
        name="ffn_down",
    )(*args)


def _run_pass(x, mod, layers, consts, init_states, final_norm):
    b, seq_len, _ = x.shape
    n = b * seq_len
    is_ctx = init_states is None
    rows_per_cond = None if is_ctx else seq_len
    step_rows = 4096
    x2d = x.reshape(n, D_MODEL)
    depth = len(layers)
    finals = []
    for li, lw in enumerate(layers):
        mod6 = mod[li]
        proj, ab, u4f = _in_projection(x2d, mod6, lw['norm1'], lw['w_in_main'], lw['w_in_ab'], rows_per_cond)
        pre = _gdn_precompute(proj, ab, lw['w_conv_qkv'], lw['gate_par'], consts['gdn_masks'], consts['gdn_tri'],
                              seq_len)
        o_f, o_b, sd = _gdn_scan(pre, None if is_ctx else init_states[0][:, li], seq_len)
        yd, sfin = _s5_mixer(u4f, lw['s5_mats'], None if is_ctx else init_states[1][:, li], seq_len)
        o_c, sr = _retention(proj, consts['ret_tables'], consts.get('rope'),
                             None if is_ctx else init_states[2][:, li], seq_len, step_rows)
        x1 = _merge(x2d, o_f, o_b, yd, o_c, proj, mod6, lw, rows_per_cond)
        hu = _ffn_up(x1, mod6, lw['norm2'], lw['w_up'], rows_per_cond)
        x2d = _ffn_down(hu, x1, mod6, lw, final_norm, seq_len, rows_per_cond, final=(li == depth - 1))
        finals.append((sd, sfin, sr))
    return x2d.reshape(b, seq_len, D_MODEL), finals


def _s5_state_to_rows(s_re, s_im):
    shp = s_re.shape[:3]
    re = s_re.reshape(shp + (S5_BLOCKS, 4, LANE))
    im = s_im.reshape(shp + (S5_BLOCKS, 4, LANE))
    return jnp.concatenate([re, im], axis=-2)


def _s5_rows_to_state(fin):
    nb = fin.shape[2]
    re = fin[:, :, :, 0:4, :].reshape(2, S5_BLOCKS, nb, 8, P_B)
    im = fin[:, :, :, 4:8, :].reshape(2, S5_BLOCKS, nb, 8, P_B)
    perm = lambda t: jnp.transpose(t, (2, 0, 1, 3, 4)).reshape(nb, 2, G_B, P_B)
    return perm(re), perm(im)


def kernel(x_prompt, x_sample, state_delta, state_ssm_re, state_ssm_im, state_ret, c, c_ctx,
           final_norm, norm1, norm2, w_mod, b_mod, w_in, w_conv_qkv, a_log, dt_bias, norm_a, w_br_a,
           ssm_lam_re, ssm_lam_im, ssm_log_dt, ssm_b_re, ssm_b_im, ssm_c_re, ssm_c_im, ssm_d,
           w_glu, b_glu, w_br_b, w_br_c, w_o, w_up, w_conv_ffn, b_conv_ffn, w_down):
    depth = w_in.shape[0]
    dec_b = x_sample.shape[0]
    n_qkv = 3 * W_MIX
    layers = []
    for i in range(depth):
        wi = w_in[i]
        gates_w = wi[:, -3 * D_MODEL:]
        rest = wi[:, n_qkv + 16:-3 * D_MODEL]
        w_main = jnp.concatenate([gates_w, wi[:, :n_qkv], rest], axis=1).astype(BF16)
        w_ab = jnp.pad(wi[:, n_qkv:n_qkv + 16], ((0, 0), (0, LANE - 16))).astype(BF16)
        gate_par = jnp.zeros((8, LANE), F32)
        gate_par = gate_par.at[0, :8].set(a_log[i].reshape(8)).at[1, :8].set(dt_bias[i].reshape(8))
        layers.append(dict(
            norm1=norm1[i][None], norm2=norm2[i][None], w_in_main=w_main, w_in_ab=w_ab,
            w_conv_qkv=w_conv_qkv[i], gate_par=gate_par, norm_a=norm_a[i][None],
            s5_mats=_s5_matrices(ssm_lam_re[i], ssm_lam_im[i], ssm_log_dt[i], ssm_b_re[i], ssm_b_im[i],
                                 ssm_c_re[i], ssm_c_im[i]),
            ssm_d=ssm_d[i][None], w_glu=w_glu[i].astype(BF16), b_glu=b_glu[i][None],
            w_br_a=w_br_a[i].astype(BF16), w_br_b=w_br_b[i].astype(BF16), w_br_c=w_br_c[i].astype(BF16),
            w_o=w_o[i].astype(BF16), w_up=w_up[i].astype(BF16), w_conv_ffn=w_conv_ffn[i],
            b_conv_ffn=b_conv_ffn[i][None], w_down=w_down[i].astype(BF16)))
    masks = _gdn_masks()
    consts = dict(gdn_masks=masks, gdn_tri=masks[M_INCL:M_INCL + 2].astype(BF16), ret_tables=_ret_tables())
    fnorm = final_norm[None]

    cond8 = jnp.concatenate([c_ctx[None], c, jnp.zeros((8 - 1 - dec_b, D_MODEL), F32)], axis=0)
    mod = _modulation(cond8, w_mod, b_mod).reshape(depth, 8, 6, D_MODEL)

    y_prompt, ctx_finals = _run_pass(x_prompt, mod, layers, consts, None, fnorm)
    consts_lat = dict(consts, rope=_rope_tables(x_sample.shape[1]))
    x0_rows = _s5_state_to_rows(state_ssm_re, state_ssm_im)
    y_sample, _ = _run_pass(x_sample, mod, layers, consts_lat, (state_delta, x0_rows, state_ret), fnorm)

    new_delta = jnp.stack([f[0] for f in ctx_finals], axis=1)
    s5 = [_s5_rows_to_state(f[1]) for f in ctx_finals]
    new_re = jnp.stack([s[0] for s in s5], axis=1)
    new_im = jnp.stack([s[1] for s in s5], axis=1)
    new_ret = jnp.stack([f[2] for f in ctx_finals], axis=1)
    return (y_prompt, y_sample, new_delta, new_re, new_im, new_ret)
```

```python
import functools
import math

import jax
import jax.numpy as jnp
from jax import lax
from jax.experimental import pallas as pl
from jax.experimental.pallas import tpu as pltpu

F32 = jnp.float32
BF16 = jnp.bfloat16

D_MODEL = 1024
EPS = 1e-6
N_HEADS = 4
D_HEAD = 128
W_MIX = N_HEADS * D_HEAD
GDN_CHUNK = 64
SEG = 256
GRID_W = 64
ROPE_BASE = 10000.0
G_B, P_B, GC_B = 32, 64, 16
S5_TILE = 8
S5_BLOCKS = 4
S5_W = S5_TILE * 128
D_FF = 2816
LANE = 128
VMEM_LIMIT = 56 * 1024 * 1024

U_GATES = 0
U_QA, U_KA, U_VA, U_ZA = 24, 28, 32, 36
U_UB = 40
U_QC, U_KC, U_VC, U_GC = 44, 48, 52, 56
N_PROJ = 60 * LANE
PROJ_TN = 1280
UB_BLOCK = (U_UB * LANE) // PROJ_TN
UB_LOCAL = U_UB * LANE - UB_BLOCK * PROJ_TN


def _cparams(sem):
    return pltpu.CompilerParams(dimension_semantics=sem, vmem_limit_bytes=VMEM_LIMIT)


def _dot(a, b):
    return jnp.dot(a.astype(BF16), b.astype(BF16), preferred_element_type=F32)


def _dot_nt(a, b):
    return lax.dot_general(a.astype(BF16), b.astype(BF16), (((1,), (1,)), ((), ())),
                           preferred_element_type=F32)


def _dot_tn(a, b):
    return lax.dot_general(a.astype(BF16), b.astype(BF16), (((0,), (0,)), ((), ())),
                           preferred_element_type=F32)


def _sigmoid(x):
    return jax.nn.sigmoid(x)


def _silu(x):
    return x * _sigmoid(x)


def _softplus(x):
    return jnp.maximum(x, 0.0) + jnp.log1p(jnp.exp(-jnp.abs(x)))


def _gelu_tanh(x):
    return 0.5 * x * (1.0 + jnp.tanh(math.sqrt(2.0 / math.pi) * (x + 0.044715 * x * x * x)))


def _rms(x):
    return x * lax.rsqrt(jnp.mean(x * x, axis=-1, keepdims=True) + EPS)


def _split3(x):
    hi = x.astype(BF16).astype(F32)
    r1 = x - hi
    mid = r1.astype(BF16).astype(F32)
    lo = (r1 - mid).astype(BF16).astype(F32)
    return hi, mid, lo


def _mod_kernel(c_ref, w_ref, b_ref, o_ref):
    o_ref[...] = _dot(_silu(c_ref[...]), w_ref[...]) + b_ref[...]


def _modulation(cond8, w_mod, b_mod):
    depth, _, n = w_mod.shape
    tn = 1536
    return pl.pallas_call(
        _mod_kernel,
        grid=(depth, n // tn),
        in_specs=[pl.BlockSpec((8, D_MODEL), lambda l, j: (0, 0)),
                  pl.BlockSpec((None, D_MODEL, tn), lambda l, j: (l, 0, j)),
                  pl.BlockSpec((None, 1, tn), lambda l, j: (l, 0, j))],
        out_specs=pl.BlockSpec((None, 8, tn), lambda l, j: (l, 0, j)),
        out_shape=jax.ShapeDtypeStruct((depth, 8, n), F32),
        compiler_params=_cparams(("parallel", "parallel")),
        name="modulation",
    )(cond8, w_mod, b_mod.reshape(depth, 1, n))


def _mod_row_map(tm, rows_per_cond):
    if rows_per_cond is None:
        return lambda i: 0
    return lambda i: 1 + (i * tm) // rows_per_cond


def _inproj_kernel(x_ref, mod_ref, nw_ref, w_ref, wab_ref, proj_ref, ab_ref, u4_ref, h_scr, u_scr):
    j = pl.program_id(1)
    rows8 = u4_ref.shape[1]

    @pl.when(j == 0)
    def _():
        h = _rms(x_ref[...]) * nw_ref[...]
        h = h * (1.0 + mod_ref[1:2, :]) + mod_ref[0:1, :]
        hb = h.astype(BF16)
        h_scr[...] = hb
        ab_ref[...] = jnp.dot(hb, wab_ref[...], preferred_element_type=F32)

    acc = jnp.dot(h_scr[...], w_ref[...], preferred_element_type=F32)
    proj_ref[...] = acc

    @pl.when(j == UB_BLOCK)
    def _():
        for b in range(S5_BLOCKS):
            u_scr[b] = acc[:, UB_LOCAL + LANE * b:UB_LOCAL + LANE * (b + 1)]
        for b in range(S5_BLOCKS):
            for t in range(S5_TILE):
                u4_ref[b, :, LANE * t:LANE * (t + 1)] = u_scr[b, pl.ds(t, rows8, stride=S5_TILE), :]


def _in_projection(x2d, mod6, norm_w, w_main, w_ab, rows_per_cond):
    n = x2d.shape[0]
    tm, tn = 1024, PROJ_TN
    row = _mod_row_map(tm, rows_per_cond)
    return pl.pallas_call(
        _inproj_kernel,
        grid=(n // tm, N_PROJ // tn),
        in_specs=[pl.BlockSpec((tm, D_MODEL), lambda i, j: (i, 0)),
                  pl.BlockSpec((None, 6, D_MODEL), lambda i, j: (row(i), 0, 0)),
                  pl.BlockSpec((1, D_MODEL), lambda i, j: (0, 0)),
                  pl.BlockSpec((D_MODEL, tn), lambda i, j: (0, j)),
                  pl.BlockSpec((D_MODEL, LANE), lambda i, j: (0, 0))],
        out_specs=[pl.BlockSpec((tm, tn), lambda i, j: (i, j)),
                   pl.BlockSpec((tm, LANE), lambda i, j: (i, 0)),
                   pl.BlockSpec((S5_BLOCKS, tm // S5_TILE, S5_W), lambda i, j: (0, i, 0))],
        out_shape=[jax.ShapeDtypeStruct((n, N_PROJ), F32),
                   jax.ShapeDtypeStruct((n, LANE), F32),
                   jax.ShapeDtypeStruct((S5_BLOCKS, n // S5_TILE, S5_W), F32)],
        scratch_shapes=[pltpu.VMEM((tm, D_MODEL), BF16), pltpu.VMEM((S5_BLOCKS, tm, LANE), F32)],
        compiler_params=_cparams(("parallel", "arbitrary")),
        name="in_projection",
    )(x2d, mod6, norm_w, w_main, w_ab)


M_INCL, M_STRICT, M_EYE, M_LEVEL = 0, 2, 4, 4
N_MASKS = 11


def _gdn_masks():
    r = jnp.arange(SEG)[:, None]
    c = jnp.arange(SEG)[None, :]
    same = (r // GDN_CHUNK) == (c // GDN_CHUNK)
    incl = [same & (r >= c), same & (r <= c)]
    strict = [same & (r > c), same & (r < c)]
    eye = [r == c]
    levels = [((r >> k) == (c >> k)) & ((r >> (k - 1)) != (c >> (k - 1))) for k in range(1, 7)]
    return jnp.stack(incl + strict + eye + levels).astype(F32)


def _gdn_pre_kernel(*refs, segs_per_seq):
    has_halo = segs_per_seq > 1
    q_ref, k_ref, v_ref = refs[:3]
    pos = 3
    halos = None
    if has_halo:
        halos = refs[3:9]
        pos = 9
    (ab_ref, wq_ref, wk_ref, wv_ref, gp_ref, msk_ref, tri_ref, lvl_ref,
     u_ref, w_ref, qg_ref, kg_ref, qkm_ref, eg_ref) = refs[pos:]
    seg = pl.program_id(0)
    rowi = lax.broadcasted_iota(jnp.int32, (SEG, W_MIX), 0)
    lane = lax.broadcasted_iota(jnp.int32, (SEG, LANE), 1)

    def conv_act(x_ref, cw_ref, prev_ref, next_ref):
        x = x_ref[...]
        xm = pltpu.roll(x, 1, 0)
        xp = pltpu.roll(x, SEG - 1, 0)
        if has_halo:
            p = seg % segs_per_seq
            prev = jnp.where(p > 0, prev_ref[7:8, :], 0.0)
            nxt = jnp.where(p < segs_per_seq - 1, next_ref[0:1, :], 0.0)
        else:
            prev = jnp.zeros((1, W_MIX), F32)
            nxt = prev
        xm = jnp.where(rowi == 0, prev, xm)
        xp = jnp.where(rowi == SEG - 1, nxt, xp)
        return _silu(cw_ref[0:1, :] * xm + cw_ref[1:2, :] * x + cw_ref[2:3, :] * xp)

    hq = halos[0:2] if has_halo else (None, None)
    hk = halos[2:4] if has_halo else (None, None)
    hv = halos[4:6] if has_halo else (None, None)
    qa_all = conv_act(q_ref, wq_ref, *hq)
    ka_all = conv_act(k_ref, wk_ref, *hk)
    va_all = conv_act(v_ref, wv_ref, *hv)

    abv = ab_ref[...]
    log_a = -jnp.exp(gp_ref[0:1, :]) * _softplus(abv + gp_ref[1:2, :])
    beta_all = _sigmoid(abv)

    def col(x, idx):
        return jnp.sum(jnp.where(lane == idx, x, 0.0), axis=1, keepdims=True)

    heads = []
    for h in range(N_HEADS):
        hs = slice(h * D_HEAD, (h + 1) * D_HEAD)
        qa, ka = qa_all[:, hs], ka_all[:, hs]
        qn = qa * lax.rsqrt(jnp.sum(qa * qa, axis=-1, keepdims=True) + EPS) * (D_HEAD ** -0.5)
        kn = ka * lax.rsqrt(jnp.sum(ka * ka, axis=-1, keepdims=True) + EPS)
        heads.append(dict(qn=qn, kn=kn, va=va_all[:, hs], kk=_dot_nt(kn, kn), qk=_dot_nt(qn, kn)))

    chains = []
    for h in range(N_HEADS):
        for d in range(2):
            hd = heads[h]
            g = jnp.broadcast_to(col(log_a, d * N_HEADS + h), (SEG, LANE))
            beta = col(beta_all, 2 * N_HEADS + d * N_HEADS + h)
            ghi, gmid, glo = _split3(g)
            pieces = jnp.where(lane == 0, ghi, jnp.where(lane == 1, gmid, jnp.where(lane == 2, glo, 0.0)))
            cs = jnp.dot(tri_ref[d], pieces.astype(BF16), preferred_element_type=F32)
            gc = jnp.broadcast_to(jnp.sum(jnp.where(lane < 3, cs, 0.0), axis=1, keepdims=True), (SEG, LANE))
            chi, cmid, clo = _split3(gc)
            a_mat = jnp.where(lane == 0, chi, jnp.where(lane == 1, cmid, jnp.where(lane == 2, clo,
                              jnp.where(lane < 6, 1.0, 0.0))))
            b_mat = jnp.where(lane < 3, 1.0, jnp.where(lane == 3, -chi, jnp.where(lane == 4, -cmid,
                              jnp.where(lane == 5, -clo, 0.0))))
            e = jnp.exp(jnp.minimum(_dot_nt(a_mat, b_mat), 0.0))
            lmat = (beta * hd['kk']) * (e * msk_ref[M_STRICT + d])
            chains.append(dict(h=h, d=d, beta=beta, gc=gc, lmat=lmat.astype(BF16),
                               qkm=hd['qk'] * (e * msk_ref[M_INCL + d]),
                               x=(msk_ref[M_EYE] - lmat * msk_ref[M_LEVEL + 1]).astype(BF16)))

    for lvl in range(2, 7):
        for ch in chains:
            c = ch['lmat'] * lvl_ref[lvl - 1]
            t = jnp.dot(c, ch['x'], preferred_element_type=F32).astype(BF16)
            ch['x'] = ch['x'] - jnp.dot(ch['x'], t, preferred_element_type=F32).astype(BF16)

    for ch in chains:
        h, d, beta, gc = ch['h'], ch['d'], ch['beta'], ch['gc']
        hd = heads[h]
        qn, kn, va = hd['qn'], hd['kn'], hd['va']
        hs = slice(h * D_HEAD, (h + 1) * D_HEAD)
        egc = jnp.exp(gc)
        rhs = jnp.concatenate([va * beta, kn * (beta * egc)], axis=1).astype(BF16)
        uw = jnp.dot(ch['x'], rhs, preferred_element_type=F32)
        u_ref[d, :, hs] = uw[:, :D_HEAD]
        w_ref[d, :, hs] = uw[:, D_HEAD:].astype(BF16)
        qg_ref[d, :, hs] = (qn * egc).astype(BF16)
        edge = GDN_CHUNK - 1 if d == 0 else 0
        for c4 in range(SEG // GDN_CHUNK):
            lo_r = c4 * GDN_CHUNK
            rs = slice(lo_r, lo_r + GDN_CHUNK)
            g_edge = gc[lo_r + edge:lo_r + edge + 1, :]
            kg_ref[d, rs, hs] = (kn[rs, :] * jnp.exp(g_edge - gc[rs, :])).astype(BF16)
            eg_ref[d, c4, h:h + 1, :] = jnp.exp(g_edge)
            qkm_ref[d, h, rs, :] = ch['qkm'][rs, rs].astype(BF16)


def _gdn_precompute(proj, ab, wconv, gate_par, masks, tri, levels, seq_len):
    n = proj.shape[0]
    segs_per_seq = seq_len // SEG
    nseg = n // SEG
    units = (U_QA // N_HEADS, U_KA // N_HEADS, U_VA // N_HEADS)
    in_specs = [pl.BlockSpec((SEG, W_MIX), lambda s, cb=cb: (s, cb)) for cb in units]
    args = [proj, proj, proj]
    if segs_per_seq > 1:
        r8 = SEG // 8
        last8 = n // 8 - 1
        for cb in units:
            in_specs.append(pl.BlockSpec((8, W_MIX), lambda s, cb=cb: (jnp.maximum(s * r8 - 1, 0), cb)))
            in_specs.append(pl.BlockSpec((8, W_MIX), lambda s, cb=cb: (jnp.minimum((s + 1) * r8, last8), cb)))
            args += [proj, proj]
    in_specs += [pl.BlockSpec((SEG, LANE), lambda s: (s, 0))]
    in_specs += [pl.BlockSpec((3, W_MIX), lambda s, o=o: (0, o)) for o in range(3)]
    in_specs += [pl.BlockSpec((8, LANE), lambda s: (0, 0)),
                 pl.BlockSpec((N_MASKS, SEG, SEG), lambda s: (0, 0, 0)),
                 pl.BlockSpec((2, SEG, SEG), lambda s: (0, 0, 0)),
                 pl.BlockSpec((6, SEG, SEG), lambda s: (0, 0, 0))]
    args += [ab, wconv, wconv, wconv, gate_par, masks, tri, levels]
    big = pl.BlockSpec((2, SEG, W_MIX), lambda s: (0, s, 0))
    cpseg = SEG // GDN_CHUNK
    return pl.pallas_call(
        functools.partial(_gdn_pre_kernel, segs_per_seq=segs_per_seq),
        grid=(nseg,),
        in_specs=in_specs,
        out_specs=[big, big, big, big,
                   pl.BlockSpec((2, N_HEADS, SEG, GDN_CHUNK), lambda s: (0, 0, s, 0)),
                   pl.BlockSpec((2, cpseg, N_HEADS, LANE), lambda s: (0, s, 0, 0))],
        out_shape=[jax.ShapeDtypeStruct((2, n, W_MIX), F32),
                   jax.ShapeDtypeStruct((2, n, W_MIX), BF16),
                   jax.ShapeDtypeStruct((2, n, W_MIX), BF16),
                   jax.ShapeDtypeStruct((2, n, W_MIX), BF16),
                   jax.ShapeDtypeStruct((2, N_HEADS, n, GDN_CHUNK), BF16),
                   jax.ShapeDtypeStruct((2, n // GDN_CHUNK, N_HEADS, LANE), F32)],
        compiler_params=_cparams(("parallel",)),
        name="gdn_precompute",
    )(*args)


GDN_SCAN_ROWS = 1024


def _gdn_scan_kernel(*refs, chunks_per_seq, has_init):
    ins = refs[:12]
    dir_refs = [ins[0::2], ins[1::2]]
    if has_init:
        s0_ref, of_ref, ob_ref, s_scr = refs[12:]
        fin_refs = None
    else:
        of_ref, ob_ref, sff_ref, sfb_ref, s_scr = refs[12:]
        fin_refs = (sff_ref, sfb_ref)
    o_refs = (of_ref, ob_ref)
    nchunk = of_ref.shape[0] // GDN_CHUNK
    cps = chunks_per_seq

    @pl.when(pl.program_id(1) == 0)
    def _():
        if has_init:
            s_scr[...] = s0_ref[...]
        else:
            s_scr[...] = jnp.zeros_like(s_scr)

    def body(i, carry):
        chains = []
        for h in range(N_HEADS):
            for d in range(2):
                c = i if d == 0 else nchunk - 1 - i
                chains.append(dict(h=h, d=d, c=c, hs=slice(h * D_HEAD, (h + 1) * D_HEAD),
                                   rsl=pl.ds(pl.multiple_of(c * GDN_CHUNK, GDN_CHUNK), GDN_CHUNK)))
        for ch in chains:
            s = s_scr[ch['d'], ch['h']]
            if not has_init:
                s = jnp.where(i % cps == 0, 0.0, s)
            ch['s'] = s
            ch['sb'] = s.astype(BF16)
        for ch in chains:
            u_ref, w_ref, qg_ref = dir_refs[ch['d']][:3]
            rsl, hs = ch['rsl'], ch['hs']
            ch['vb'] = (u_ref[rsl, hs] - jnp.dot(w_ref[rsl, hs], ch['sb'], preferred_element_type=F32)).astype(BF16)
            ch['o'] = jnp.dot(qg_ref[rsl, hs], ch['sb'], preferred_element_type=F32)
        for ch in chains:
            kg_ref, qkm_ref, eg_ref = dir_refs[ch['d']][3:]
            rsl, hs, h = ch['rsl'], ch['hs'], ch['h']
            ch['o'] = ch['o'] + jnp.dot(qkm_ref[h, rsl, :], ch['vb'], preferred_element_type=F32)
            ch['s_new'] = ch['s'] * eg_ref[ch['c']][h:h + 1, :] + _dot_tn(kg_ref[rsl, hs], ch['vb'])
        for ch in chains:
            s_scr[ch['d'], ch['h']] = ch['s_new']
            o_refs[ch['d']][ch['rsl'], ch['hs']] = ch['o']
        if not has_init:
            @pl.when(i % cps == cps - 1)
            def _():
                for ch in chains:
                    fin_refs[ch['d']][ch['c'] // cps, ch['h']] = ch['s_new']
        return carry

    lax.fori_loop(0, nchunk, body, 0)


def _gdn_scan(pre, s0, seq_len):
    n = pre[0].shape[1]
    br = GDN_SCAN_ROWS
    has_init = s0 is not None
    group_rows = seq_len if has_init else n
    ngroups, nblk = n // group_rows, group_rows // br
    fwd = lambda g, j: g * nblk + j
    bwd = lambda g, j: g * nblk + (nblk - 1 - j)
    in_specs, args = [], []
    for arr, kind in zip(pre, ("row", "row", "row", "row", "qkm", "eg")):
        for d, pos in ((0, fwd), (1, bwd)):
            if kind == "row":
                in_specs.append(pl.BlockSpec((None, br, W_MIX), lambda g, j, d=d, pos=pos: (d, pos(g, j), 0)))
            elif kind == "qkm":
                in_specs.append(pl.BlockSpec((None, N_HEADS, br, GDN_CHUNK),
                                             lambda g, j, d=d, pos=pos: (d, 0, pos(g, j), 0)))
            else:
                in_specs.append(pl.BlockSpec((None, br // GDN_CHUNK, N_HEADS, LANE),
                                             lambda g, j, d=d, pos=pos: (d, pos(g, j), 0, 0)))
            args.append(arr)
    out_specs = [pl.BlockSpec((br, W_MIX), lambda g, j: (fwd(g, j), 0)),
                 pl.BlockSpec((br, W_MIX), lambda g, j: (bwd(g, j), 0))]
    out_shape = [jax.ShapeDtypeStruct((n, W_MIX), F32), jax.ShapeDtypeStruct((n, W_MIX), F32)]
    if has_init:
        in_specs.append(pl.BlockSpec((None, 2, N_HEADS, D_HEAD, D_HEAD), lambda g, j: (g, 0, 0, 0, 0)))
        args.append(s0)
    else:
        spb = br // seq_len
        fin_shape = jax.ShapeDtypeStruct((n // seq_len, N_HEADS, D_HEAD, D_HEAD), F32)
        out_specs += [pl.BlockSpec((spb, N_HEADS, D_HEAD, D_HEAD), lambda g, j: (fwd(g, j), 0, 0, 0)),
                      pl.BlockSpec((spb, N_HEADS, D_HEAD, D_HEAD), lambda g, j: (bwd(g, j), 0, 0, 0))]
        out_shape += [fin_shape, fin_shape]
    res = pl.pallas_call(
        functools.partial(_gdn_scan_kernel, chunks_per_seq=seq_len // GDN_CHUNK, has_init=has_init),
        grid=(ngroups, nblk),
        in_specs=in_specs, out_specs=out_specs, out_shape=out_shape,
        scratch_shapes=[pltpu.VMEM((2, N_HEADS, D_HEAD, D_HEAD), F32)],
        compiler_params=_cparams(("parallel", "arbitrary")),
        name="gdn_scan",
    )(*args)
    fin = None if has_init else jnp.stack([res[2], res[3]], axis=1)
    return res[0], res[1], fin


def _s5_matrices(lam_re, lam_im, log_dt, b_re, b_im, c_re, c_im):
    t = S5_TILE
    step = jnp.exp(log_dt)[:, :, None]
    lr, li = lam_re * step, lam_im * step
    js = jnp.arange(t + 1, dtype=F32)[:, None, None, None]
    mag = jnp.exp(lr[None] * js)
    pw_re, pw_im = mag * jnp.cos(li[None] * js), mag * jnp.sin(li[None] * js)
    a_re, a_im = pw_re[1], pw_im[1]
    den = lam_re * lam_re + lam_im * lam_im
    z_re = ((a_re - 1.0) * lam_re + a_im * lam_im) / den
    z_im = (a_im * lam_re - (a_re - 1.0) * lam_im) / den
    zb_re = z_re[..., None] * b_re[None] - z_im[..., None] * b_im[None]
    zb_im = z_re[..., None] * b_im[None] + z_im[..., None] * b_re[None]
    w_re = pw_re[..., None] * zb_re[None] - pw_im[..., None] * zb_im[None]
    w_im = pw_re[..., None] * zb_im[None] + pw_im[..., None] * zb_re[None]
    taps = (jnp.einsum('gcp,jdgpi->jdgic', c_re, w_re[:t]) - jnp.einsum('gcp,jdgpi->jdgic', c_im, w_im[:t]))
    s_idx = jnp.arange(t)[:, None]
    t_idx = jnp.arange(t)[None, :]
    nb = S5_BLOCKS

    def blocked(x, axis):
        return x.reshape(x.shape[:axis] + (nb, 8) + x.shape[axis + 1:])

    tm, gm, cm = [], [], []
    for d in range(2):
        lag = (t_idx - s_idx) if d == 0 else (s_idx - t_idx)
        kst = jnp.where((lag >= 0)[:, :, None, None, None], taps[jnp.clip(lag, 0, t - 1), d], 0.0)
        kst = blocked(kst, 2)
        tm.append(jnp.transpose(kst, (2, 0, 3, 4, 1, 5)).reshape(nb, S5_W, LANE))
        e_in = (t - 1 - jnp.arange(t)) if d == 0 else jnp.arange(t)
        ws = jnp.stack([w_re[e_in, d], w_im[e_in, d]], axis=2)
        ws = blocked(ws, 1)
        gm.append(jnp.transpose(ws, (1, 0, 2, 5, 3, 4)).reshape(nb, S5_W, LANE))
        e_out = (jnp.arange(t) + 1) if d == 0 else (t - jnp.arange(t))
        pr, pi = pw_re[e_out, d], pw_im[e_out, d]
        wo_re = c_re[None] * pr[:, :, None, :] - c_im[None] * pi[:, :, None, :]
        wo_im = c_re[None] * pi[:, :, None, :] + c_im[None] * pr[:, :, None, :]
        wo = blocked(jnp.stack([wo_re, -wo_im], axis=0), 2)
        cm.append(jnp.transpose(wo, (2, 0, 3, 5, 1, 4)).reshape(nb, S5_W, LANE))
    a8r = pw_re[t].reshape(2, nb, 4, LANE)
    a8i = pw_im[t].reshape(2, nb, 4, LANE)
    a1 = jnp.concatenate([a8r, a8r], axis=2)
    a2 = jnp.concatenate([-a8i, a8i], axis=2)
    return jnp.stack([jnp.stack(tm), jnp.stack(gm), jnp.stack(cm)]), a1, a2


def _s5_expand_consts():
    col = jnp.arange(S5_W)
    src = jnp.arange(LANE)
    ex_tc = (src[:, None] == ((col // LANE) * GC_B + col % GC_B)[None, :])
    ex_rp = (src[:, None] == ((col // (S5_W // 2)) * P_B + col % P_B)[None, :])
    g_tc = (col // GC_B) % 8
    g_rp = (col // P_B) % 8
    masks = [g_tc[:, None] == g_tc[None, :], g_tc[:, None] == g_rp[None, :], g_rp[:, None] == g_tc[None, :]]
    return (jnp.stack([ex_tc, ex_rp, ex_tc]).astype(BF16), jnp.stack(masks).astype(BF16))


def _s5_expand_kernel(c_ref, ex_ref, m_ref, o_ref):
    full = jnp.dot(c_ref[...].astype(BF16), ex_ref[...], preferred_element_type=F32)
    o_ref[...] = (full * m_ref[...].astype(F32)).astype(BF16)


def _s5_expand(compact, consts):
    ex, masks = consts
    nl, _, _, nb = compact.shape[:4]
    return pl.pallas_call(
        _s5_expand_kernel,
        grid=(3, nl, 2, nb),
        in_specs=[pl.BlockSpec((None, None, None, None, S5_W, LANE), lambda k, l, d, b: (l, k, d, b, 0, 0)),
                  pl.BlockSpec((None, LANE, S5_W), lambda k, l, d, b: (k, 0, 0)),
                  pl.BlockSpec((None, S5_W, S5_W), lambda k, l, d, b: (k, 0, 0))],
        out_specs=pl.BlockSpec((None, None, None, None, S5_W, S5_W), lambda k, l, d, b: (l, k, d, b, 0, 0)),
        out_shape=jax.ShapeDtypeStruct(compact.shape[:4] + (S5_W, S5_W), BF16),
        compiler_params=_cparams(("parallel",) * 4),
        name="s5_expand",
    )(compact, ex, masks)


def _s5_in_kernel(u_ref, gm_ref, g_ref):
    rows = u_ref.shape[0]
    g = _dot(u_ref[...], gm_ref[...])
    for s in range(8):
        g_ref[pl.ds(s, rows, stride=8), :] = g[:, LANE * s:LANE * (s + 1)]


def _s5_tile_inputs(u4f, gm):
    nb, rows, _ = u4f.shape
    return pl.pallas_call(
        _s5_in_kernel,
        grid=(nb, 2),
        in_specs=[pl.BlockSpec((None, rows, S5_W), lambda b, d: (b, 0, 0)),
                  pl.BlockSpec((None, None, S5_W, S5_W), lambda b, d: (d, b, 0, 0))],
        out_specs=pl.BlockSpec((None, None, rows * 8, LANE), lambda b, d: (d, b, 0, 0)),
        out_shape=jax.ShapeDtypeStruct((2, nb, rows * 8, LANE), F32),
        compiler_params=_cparams(("parallel", "parallel")),
        name="s5_tile_inputs",
    )(u4f, gm)


def _s5_scan_kernel(*refs, nseq, rows_per_seq, has_init):
    g_ref, a1_ref, a2_ref = refs[:3]
    if has_init:
        x0_ref, xp_ref = refs[3:]
        fin_ref = None
    else:
        xp_ref, fin_ref = refs[3:]
    nblk = g_ref.shape[0]
    backward = pl.program_id(0) == 1
    chains = [(b, s) for b in range(nblk) for s in range(nseq)]

    def body(i, xs):
        n = jnp.where(backward, rows_per_seq - 1 - i, i)
        out = []
        for (b, s), x in zip(chains, xs):
            r = s * rows_per_seq + n
            xp_ref[b, r] = x
            out.append(a1_ref[b] * x + a2_ref[b] * pltpu.roll(x, 4, 0) + g_ref[b, r])
        return tuple(out)

    if has_init:
        init = tuple(x0_ref[b] for (b, s) in chains)
    else:
        init = tuple(jnp.zeros((8, LANE), F32) for _ in chains)
    xs = lax.fori_loop(0, rows_per_seq, body, init)
    if not has_init:
        for (b, s), x in zip(chains, xs):
            fin_ref[s] = x


def _s5_scan(g5, a1, a2, x0, seq_rows):
    _, nb, rows, _, _ = g5.shape
    nseq = rows // seq_rows
    if x0 is None:
        return pl.pallas_call(
            functools.partial(_s5_scan_kernel, nseq=nseq, rows_per_seq=seq_rows, has_init=False),
            grid=(2, nb),
            in_specs=[pl.BlockSpec((None, 1, rows, 8, LANE), lambda d, b: (d, b, 0, 0, 0)),
                      pl.BlockSpec((None, 1, 8, LANE), lambda d, b: (d, b, 0, 0)),
                      pl.BlockSpec((None, 1, 8, LANE), lambda d, b: (d, b, 0, 0))],
            out_specs=[pl.BlockSpec((None, 1, rows, 8, LANE), lambda d, b: (d, b, 0, 0, 0)),
                       pl.BlockSpec((None, None, nseq, 8, LANE), lambda d, b: (d, b, 0, 0, 0))],
            out_shape=[jax.ShapeDtypeStruct(g5.shape, F32),
                       jax.ShapeDtypeStruct((2, nb, nseq, 8, LANE), F32)],
            compiler_params=_cparams(("parallel", "parallel")),
            name="s5_scan_ctx",
        )(g5, a1, a2)
    xprev = pl.pallas_call(
        functools.partial(_s5_scan_kernel, nseq=1, rows_per_seq=seq_rows, has_init=True),
        grid=(2, nseq),
        in_specs=[pl.BlockSpec((None, nb, seq_rows, 8, LANE), lambda d, s: (d, 0, s, 0, 0)),
                  pl.BlockSpec((None, nb, 8, LANE), lambda d, s: (d, 0, 0, 0)),
                  pl.BlockSpec((None, nb, 8, LANE), lambda d, s: (d, 0, 0, 0)),
                  pl.BlockSpec((None, None, nb, 8, LANE), lambda d, s: (s, d, 0, 0, 0))],
        out_specs=pl.BlockSpec((None, nb, seq_rows, 8, LANE), lambda d, s: (d, 0, s, 0, 0)),
        out_shape=jax.ShapeDtypeStruct(g5.shape, F32),
        compiler_params=_cparams(("parallel", "parallel")),
        name="s5_scan_latent",
    )(g5, a1, a2, x0)
    return xprev, None


def _s5_out_kernel(u_ref, xp_ref, tm_ref, cm_ref, y_ref):
    tr = u_ref.shape[0]
    u = u_ref[...].astype(BF16)
    acc = jnp.dot(u, tm_ref[0], preferred_element_type=F32)
    acc += jnp.dot(u, tm_ref[1], preferred_element_type=F32)
    for d in range(2):
        xp = jnp.concatenate([xp_ref[d, pl.ds(s, tr, stride=8), :] for s in range(8)], axis=1)
        acc += _dot(xp, cm_ref[d])
    for t in range(S5_TILE):
        y_ref[pl.ds(t, tr, stride=S5_TILE), :] = acc[:, LANE * t:LANE * (t + 1)]


def _s5_outputs(u4f, xprev, tm, cm):
    nb, rows, _ = u4f.shape
    tr = 512
    return pl.pallas_call(
        _s5_out_kernel,
        grid=(nb, rows // tr),
        in_specs=[pl.BlockSpec((None, tr, S5_W), lambda b, r: (b, r, 0)),
                  pl.BlockSpec((2, None, tr * 8, LANE), lambda b, r: (0, b, r, 0)),
                  pl.BlockSpec((2, None, S5_W, S5_W), lambda b, r: (0, b, 0, 0)),
                  pl.BlockSpec((2, None, S5_W, S5_W), lambda b, r: (0, b, 0, 0))],
        out_specs=pl.BlockSpec((tr * S5_TILE, LANE), lambda b, r: (r, b)),
        out_shape=jax.ShapeDtypeStruct((rows * S5_TILE, S5_BLOCKS * LANE), F32),
        compiler_params=_cparams(("parallel", "parallel")),
        name="s5_outputs",
    )(u4f, xprev, tm, cm)


def _s5_mixer(u4f, mats, x0, seq_len):
    tm, gm, cm, a1, a2 = mats
    nb, rows, _ = u4f.shape
    g = _s5_tile_inputs(u4f, gm)
    xprev, fin = _s5_scan(g.reshape(2, nb, rows, 8, LANE), a1, a2, x0, seq_len // S5_TILE)
    yd = _s5_outputs(u4f, xprev.reshape(2, nb, rows * 8, LANE), tm, cm)
    return yd, fin


def _ret_tables():
    h = jnp.arange(N_HEADS, dtype=F32)
    lg = jnp.log1p(-jnp.exp2(-5.0 - h))
    lgf, lgb = lg[:, None], lg[::-1][:, None]
    i = jnp.arange(SEG, dtype=F32)
    dist = i[:, None] - i[None, :]
    d_f = jnp.where(dist >= 0, jnp.exp(lgf[:, :, None] * jnp.maximum(dist, 0.0)), 0.0)
    d_b = jnp.where(dist <= 0, jnp.exp(lgb[:, :, None] * jnp.maximum(-dist, 0.0)), 0.0)
    dsum = d_f + d_b
    dec = jnp.stack([jnp.exp(lgf * (i + 1.0)), jnp.exp(lgf * (SEG - 1.0 - i)),
                     jnp.exp(lgb * (SEG - i)), jnp.exp(lgb * i)], axis=1)
    dec = jnp.broadcast_to(dec[..., None], (N_HEADS, 4, SEG, LANE))
    cd = jnp.stack([jnp.exp(lgf[:, 0] * SEG), jnp.exp(lgb[:, 0] * SEG)], axis=1)
    cd = jnp.broadcast_to(jnp.pad(cd, ((0, 0), (0, 6)))[..., None], (N_HEADS, 8, LANE))
    return dsum, dec, cd


def _rope_tables(n_tokens):
    n_rows = n_tokens // GRID_W
    rows = jnp.repeat(jnp.arange(n_rows), GRID_W).astype(F32)
    cols = jnp.tile(jnp.arange(GRID_W), n_rows).astype(F32)
    pairs = D_HEAD // 4
    freqs = ROPE_BASE ** (-jnp.arange(pairs, dtype=F32) / pairs)
    ang = jnp.concatenate([rows[:, None] * freqs, cols[:, None] * freqs], axis=-1)
    cos, sin = jnp.cos(ang), jnp.sin(ang)
    return jnp.concatenate([cos, cos], axis=-1), jnp.concatenate([-sin, sin], axis=-1)


def _ret_kernel(*refs, is_ctx):
    q_ref, k_ref, v_ref, g_ref, dsum_ref, dec_ref, cd_ref = refs[:7]
    if is_ctx:
        o_ref, sfin_ref = refs[7:]
    else:
        c2_ref, s2_ref, s0_ref, o_ref, sbin_scr = refs[7:]
    rows = q_ref.shape[0]
    nchunk = rows // SEG
    scale = D_HEAD ** -0.5

    def chunk_rows(c):
        return pl.ds(pl.multiple_of(c * SEG, SEG), SEG)

    def load_qk(rsl):
        q, k = q_ref[rsl, :], k_ref[rsl, :]
        if not is_ctx:
            c2, s2 = c2_ref[rsl, :], s2_ref[rsl, :]
            q = q * c2 + pltpu.roll(q, D_HEAD // 2, 1) * s2
            k = k * c2 + pltpu.roll(k, D_HEAD // 2, 1) * s2
        return q, k * scale

    def finish(rsl, r):
        o_ref[rsl, :] = _rms(r) * _silu(g_ref[rsl, :])

    if is_ctx:
        def body(c, carry):
            rsl = chunk_rows(c)
            q, k = load_qk(rsl)
            v = v_ref[rsl, :]
            finish(rsl, _dot(_dot_nt(q, k) * dsum_ref[...], v))
            sfin_ref[c, 0] = _dot_tn(k * dec_ref[1], v)
            sfin_ref[c, 1] = _dot_tn(k * dec_ref[3], v)
            return carry
        lax.fori_loop(0, nchunk, body, 0, unroll=2)
    else:
        cd_f, cd_b = cd_ref[0:1, :], cd_ref[1:2, :]

        def back(i, s_b):
            c = nchunk - 1 - i
            rsl = chunk_rows(c)
            _, k = load_qk(rsl)
            sbin_scr[c] = s_b
            return s_b * cd_b + _dot_tn(k * dec_ref[3], v_ref[rsl, :])
        lax.fori_loop(0, nchunk, back, s0_ref[1], unroll=2)

        def fwd(c, s_f):
            rsl = chunk_rows(c)
            q, k = load_qk(rsl)
            v = v_ref[rsl, :]
            r = (_dot(_dot_nt(q, k) * dsum_ref[...], v) + _dot(q * dec_ref[0], s_f)
                 + _dot(q * dec_ref[2], sbin_scr[c]))
            finish(rsl, r)
            return s_f * cd_f + _dot_tn(k * dec_ref[1], v)
        lax.fori_loop(0, nchunk, fwd, s0_ref[0], unroll=2)


def _retention(proj, tables, rope, s0, seq_len, step_rows):
    dsum, dec, cd = tables
    n = proj.shape[0]
    is_ctx = s0 is None
    blk = lambda unit: pl.BlockSpec((step_rows, LANE), lambda s, h, unit=unit: (s, unit + h))
    in_specs = [blk(U_QC), blk(U_KC), blk(U_VC), blk(U_GC),
                pl.BlockSpec((None, SEG, SEG), lambda s, h: (h, 0, 0)),
                pl.BlockSpec((None, 4, SEG, LANE), lambda s, h: (h, 0, 0, 0)),
                pl.BlockSpec((None, 8, LANE), lambda s, h: (h, 0, 0))]
    args = [proj, proj, proj, proj, dsum, dec, cd]
    out_specs = [pl.BlockSpec((step_rows, LANE), lambda s, h: (s, h))]
    out_shape = [jax.ShapeDtypeStruct((n, W_MIX), F32)]
    scratch = []
    if is_ctx:
        assert seq_len == SEG
        out_specs.append(pl.BlockSpec((step_rows // SEG, 2, None, D_HEAD, D_HEAD), lambda s, h: (s, 0, h, 0, 0)))
        out_shape.append(jax.ShapeDtypeStruct((n // SEG, 2, N_HEADS, D_HEAD, D_HEAD), F32))
    else:
        assert step_rows == seq_len
        in_specs += [pl.BlockSpec((step_rows, LANE), lambda s, h: (0, 0)),
                     pl.BlockSpec((step_rows, LANE), lambda s, h: (0, 0)),
                     pl.BlockSpec((None, 2, None, D_HEAD, D_HEAD), lambda s, h: (s, 0, h, 0, 0))]
        args += [rope[0], rope[1], s0]
        scratch = [pltpu.VMEM((step_rows // SEG, D_HEAD, D_HEAD), F32)]
    res = pl.pallas_call(
        functools.partial(_ret_kernel, is_ctx=is_ctx),
        grid=(n // step_rows, N_HEADS),
        in_specs=in_specs, out_specs=out_specs, out_shape=out_shape, scratch_shapes=scratch,
        compiler_params=_cparams(("parallel", "parallel")),
        name="retention",
    )(*args)
    return res[0], (res[1] if is_ctx else None)


def _merge_kernel(x_ref, of_ref, ob_ref, z_ref, na_ref, yd_ref, u_ref, oc_ref, gates_ref, mod_ref, d_ref,
                  wglu_ref, bglu_ref, wa_ref, wb_ref, wc_ref, wo_ref, o_ref):
    tot = of_ref[...] + ob_ref[...]
    o_a = jnp.concatenate([_rms(tot[:, h * D_HEAD:(h + 1) * D_HEAD]) * na_ref[...] for h in range(N_HEADS)],
                          axis=1) * _silu(z_ref[...])
    y = _gelu_tanh(u_ref[...] * d_ref[...] + yd_ref[...])
    y = y * _sigmoid(_dot(y, wglu_ref[...]) + bglu_ref[...])
    gt = _sigmoid(gates_ref[...])
    merged = (gt[:, :D_MODEL] * _dot(o_a, wa_ref[...])
              + gt[:, D_MODEL:2 * D_MODEL] * _dot(y, wb_ref[...])
              + gt[:, 2 * D_MODEL:] * _dot(oc_ref[...], wc_ref[...]))
    o_ref[...] = x_ref[...] + mod_ref[2:3, :] * _dot(merged, wo_ref[...])


def _merge(x2d, o_f, o_b, yd, o_c, proj, mod6, lw, rows_per_cond):
    n = x2d.shape[0]
    tm = 256
    row = _mod_row_map(tm, rows_per_cond)
    full = lambda shape: pl.BlockSpec(shape, lambda i: (0,) * len(shape))
    mix = lambda cb: pl.BlockSpec((tm, W_MIX), lambda i, cb=cb: (i, cb))
    return pl.pallas_call(
        _merge_kernel,
        grid=(n // tm,),
        in_specs=[pl.BlockSpec((tm, D_MODEL), lambda i: (i, 0)),
                  mix(0), mix(0), mix(U_ZA // N_HEADS), full((1, D_HEAD)),
                  mix(0), mix(U_UB // N_HEADS), mix(0),
                  pl.BlockSpec((tm, 3 * D_MODEL), lambda i: (i, 0)),
                  pl.BlockSpec((None, 6, D_MODEL), lambda i: (row(i), 0, 0)),
                  full((1, W_MIX)), full((W_MIX, W_MIX)), full((1, W_MIX)),
                  full((W_MIX, D_MODEL)), full((W_MIX, D_MODEL)), full((W_MIX, D_MODEL)),
                  full((D_MODEL, D_MODEL))],
        out_specs=pl.BlockSpec((tm, D_MODEL), lambda i: (i, 0)),
        out_shape=jax.ShapeDtypeStruct((n, D_MODEL), F32),
        compiler_params=_cparams(("parallel",)),
        name="merge",
    )(x2d, o_f, o_b, proj, lw['norm_a'], yd, proj, o_c, proj, mod6, lw['ssm_d'], lw['w_glu'], lw['b_glu'],
      lw['w_br_a'], lw['w_br_b'], lw['w_br_c'], lw['w_o'])


FFN_TM = 256
FFN_CW = 256


def _ffn_kernel(x_ref, xr_ref, mod_ref, modr_ref, nw_ref, wup_ref, cw_ref, cb_ref, wd_ref, fn_ref, o_ref,
                hu_a, hu_b, act_scr, *, blocks_per_seq, final):
    i = pl.program_id(0)
    tm = x_ref.shape[0]

    @pl.when(i == 0)
    def _():
        hu_a[...] = jnp.zeros_like(hu_a)
        hu_b[...] = jnp.zeros_like(hu_b)

    p = (i + blocks_per_seq - 1) % blocks_per_seq
    has_prev = p > 0
    has_next = p < blocks_per_seq - 1
    rowi = lax.broadcasted_iota(jnp.int32, (tm, FFN_CW), 0)

    def step(new_ref, cur_ref):
        prev_row = jnp.where(has_prev, new_ref[tm - 1:tm, :], 0.0)
        h = _rms(x_ref[...]) * nw_ref[...]
        h = (h * (1.0 + mod_ref[4:5, :]) + mod_ref[3:4, :]).astype(BF16)
        hu_new = jnp.dot(h, wup_ref[...], preferred_element_type=F32)
        next_row = jnp.where(has_next, hu_new[0:1, :], 0.0)
        new_ref[...] = hu_new

        def conv(lo):
            cs = slice(lo, lo + FFN_CW)
            hu = cur_ref[:, cs]
            hm = jnp.where(rowi == 0, prev_row[:, cs], pltpu.roll(hu, 1, 0))
            hp = jnp.where(rowi == tm - 1, next_row[:, cs], pltpu.roll(hu, tm - 1, 0))
            return cw_ref[0:1, cs] * hm + cw_ref[1:2, cs] * hu + cw_ref[2:3, cs] * hp + cb_ref[:, cs]

        for cb in range(D_FF // FFN_CW):
            lo = cb * FFN_CW
            act_scr[:, lo:lo + FFN_CW] = (_silu(conv(lo)) * conv(D_FF + lo)).astype(BF16)
        x2 = xr_ref[...] + modr_ref[5:6, :] * jnp.dot(act_scr[...], wd_ref[...], preferred_element_type=F32)
        if final:
            x2 = _rms(x2) * fn_ref[...]
        o_ref[...] = x2

    @pl.when(i % 2 == 0)
    def _():
        step(hu_a, hu_b)

    @pl.when(i % 2 == 1)
    def _():
        step(hu_b, hu_a)


def _ffn(x2d, mod6, lw, final_norm, seq_len, rows_per_cond, final):
    n = x2d.shape[0]
    tm = FFN_TM
    nblk = n // tm
    row = _mod_row_map(tm, rows_per_cond)
    cur = lambda i: jnp.minimum(i, nblk - 1)
    done = lambda i: jnp.maximum(i - 1, 0)
    full = lambda shape: pl.BlockSpec(shape, lambda i: (0,) * len(shape))
    return pl.pallas_call(
        functools.partial(_ffn_kernel, blocks_per_seq=seq_len // tm, final=final),
        grid=(nblk + 1,),
        in_specs=[pl.BlockSpec((tm, D_MODEL), lambda i: (cur(i), 0)),
                  pl.BlockSpec((tm, D_MODEL), lambda i: (done(i), 0)),
                  pl.BlockSpec((None, 6, D_MODEL), lambda i: (row(cur(i)), 0, 0)),
                  pl.BlockSpec((None, 6, D_MODEL), lambda i: (row(done(i)), 0, 0)),
                  full((1, D_MODEL)), full((D_MODEL, 2 * D_FF)), full((3, 2 * D_FF)), full((1, 2 * D_FF)),
                  full((D_FF, D_MODEL)), full((1, D_MODEL))],
        out_specs=pl.BlockSpec((tm, D_MODEL), lambda i: (done(i), 0)),
        out_shape=jax.ShapeDtypeStruct((n, D_MODEL), F32),
        scratch_shapes=[pltpu.VMEM((tm, 2 * D_FF), F32), pltpu.VMEM((tm, 2 * D_FF), F32),
                        pltpu.VMEM((tm, D_FF), BF16)],
        compiler_params=_cparams(("arbitrary",)),
        name="ffn",
    )(x2d, x2d, mod6, mod6, lw['norm2'], lw['w_up'], lw['w_conv_ffn'], lw['b_conv_ffn'], lw['w_down'],
      final_norm)


def _run_pass(x, mod, layers, consts, init_states, final_norm):
    b, seq_len, _ = x.shape
    n = b * seq_len
    is_ctx = init_states is None
    rows_per_cond = None if is_ctx else seq_len
    step_rows = 4096
    x2d = x.reshape(n, D_MODEL)
    depth = len(layers)
    finals = []
    for li, lw in enumerate(layers):
        mod6 = mod[li]
        proj, ab, u4f = _in_projection(x2d, mod6, lw['norm1'], lw['w_in_main'], lw['w_in_ab'], rows_per_cond)
        pre = _gdn_precompute(proj, ab, lw['w_conv_qkv'], lw['gate_par'], consts['gdn_masks'], consts['gdn_tri'],
                              consts['gdn_levels'], seq_len)
        o_f, o_b, sd = _gdn_scan(pre, None if is_ctx else init_states[0][:, li], seq_len)
        yd, sfin = _s5_mixer(u4f, lw['s5_mats'], None if is_ctx else init_states[1][:, li], seq_len)
        o_c, sr = _retention(proj, consts['ret_tables'], consts.get('rope'),
                             None if is_ctx else init_states[2][:, li], seq_len, step_rows)
        x1 = _merge(x2d, o_f, o_b, yd, o_c, proj, mod6, lw, rows_per_cond)
        x2d = _ffn(x1, mod6, lw, final_norm, seq_len, rows_per_cond, final=(li == depth - 1))
        finals.append((sd, sfin, sr))
    return x2d.reshape(b, seq_len, D_MODEL), finals


def _s5_state_to_rows(s_re, s_im):
    shp = s_re.shape[:3]
    re = s_re.reshape(shp + (S5_BLOCKS, 4, LANE))
    im = s_im.reshape(shp + (S5_BLOCKS, 4, LANE))
    return jnp.concatenate([re, im], axis=-2)


def _s5_rows_to_state(fin):
    nb = fin.shape[2]
    re = fin[:, :, :, 0:4, :].reshape(2, S5_BLOCKS, nb, 8, P_B)
    im = fin[:, :, :, 4:8, :].reshape(2, S5_BLOCKS, nb, 8, P_B)
    perm = lambda t: jnp.transpose(t, (2, 0, 1, 3, 4)).reshape(nb, 2, G_B, P_B)
    return perm(re), perm(im)


def kernel(x_prompt, x_sample, state_delta, state_ssm_re, state_ssm_im, state_ret, c, c_ctx,
           final_norm, norm1, norm2, w_mod, b_mod, w_in, w_conv_qkv, a_log, dt_bias, norm_a, w_br_a,
           ssm_lam_re, ssm_lam_im, ssm_log_dt, ssm_b_re, ssm_b_im, ssm_c_re, ssm_c_im, ssm_d,
           w_glu, b_glu, w_br_b, w_br_c, w_o, w_up, w_conv_ffn, b_conv_ffn, w_down):
    depth = w_in.shape[0]
    dec_b = x_sample.shape[0]
    n_qkv = 3 * W_MIX
    layers = []
    for i in range(depth):
        wi = w_in[i]
        gates_w = wi[:, -3 * D_MODEL:]
        rest = wi[:, n_qkv + 16:-3 * D_MODEL]
        w_main = jnp.concatenate([gates_w, wi[:, :n_qkv], rest], axis=1).astype(BF16)
        w_ab = jnp.pad(wi[:, n_qkv:n_qkv + 16], ((0, 0), (0, LANE - 16))).astype(BF16)
        gate_par = jnp.zeros((8, LANE), F32)
        gate_par = gate_par.at[0, :8].set(a_log[i].reshape(8)).at[1, :8].set(dt_bias[i].reshape(8))
        layers.append(dict(
            norm1=norm1[i][None], norm2=norm2[i][None], w_in_main=w_main, w_in_ab=w_ab,
            w_conv_qkv=w_conv_qkv[i], gate_par=gate_par, norm_a=norm_a[i][None],
            s5_mats=_s5_matrices(ssm_lam_re[i], ssm_lam_im[i], ssm_log_dt[i], ssm_b_re[i], ssm_b_im[i],
                                 ssm_c_re[i], ssm_c_im[i]),
            ssm_d=ssm_d[i][None], w_glu=w_glu[i].astype(BF16), b_glu=b_glu[i][None],
            w_br_a=w_br_a[i].astype(BF16), w_br_b=w_br_b[i].astype(BF16), w_br_c=w_br_c[i].astype(BF16),
            w_o=w_o[i].astype(BF16), w_up=w_up[i].astype(BF16), w_conv_ffn=w_conv_ffn[i],
            b_conv_ffn=b_conv_ffn[i][None], w_down=w_down[i].astype(BF16)))
    expanded = _s5_expand(jnp.stack([lw['s5_mats'][0] for lw in layers]), _s5_expand_consts())
    for i, lw in enumerate(layers):
        _, a1, a2 = lw['s5_mats']
        lw['s5_mats'] = (expanded[i, 0], expanded[i, 1], expanded[i, 2], a1, a2)
    masks = _gdn_masks()
    consts = dict(gdn_masks=masks, gdn_tri=masks[M_INCL:M_INCL + 2].astype(BF16),
                  gdn_levels=masks[M_LEVEL + 1:M_LEVEL + 7].astype(BF16), ret_tables=_ret_tables())
    fnorm = final_norm[None]

    cond8 = jnp.concatenate([c_ctx[None], c, jnp.zeros((8 - 1 - dec_b, D_MODEL), F32)], axis=0)
    mod = _modulation(cond8, w_mod, b_mod).reshape(depth, 8, 6, D_MODEL)

    y_prompt, ctx_finals = _run_pass(x_prompt, mod, layers, consts, None, fnorm)
    consts_lat = dict(consts, rope=_rope_tables(x_sample.shape[1]))
    x0_rows = _s5_state_to_rows(state_ssm_re, state_ssm_im)
    y_sample, _ = _run_pass(x_sample, mod, layers, consts_lat, (state_delta, x0_rows, state_ret), fnorm)

    new_delta = jnp.stack([f[0] for f in ctx_finals], axis=1)
    s5 = [_s5_rows_to_state(f[1]) for f in ctx_finals]
    new_re = jnp.stack([s[0] for s in s5], axis=1)
    new_im = jnp.stack([s[1] for s in s5], axis=1)
    new_ret = jnp.stack([f[2] for f in ctx_finals], axis=1)
    return (y_prompt, y_sample, new_delta, new_re, new_im, new_ret)
```

```python
import functools
import math

import jax
import jax.numpy as jnp
from jax import lax
from jax.experimental import pallas as pl
from jax.experimental.pallas import tpu as pltpu

F32 = jnp.float32
BF16 = jnp.bfloat16

D_MODEL = 1024
EPS = 1e-6
N_HEADS = 4
D_HEAD = 128
W_MIX = N_HEADS * D_HEAD
GDN_CHUNK = 64
SEG = 256
GRID_W = 64
ROPE_BASE = 10000.0
G_B, P_B, GC_B = 32, 64, 16
S5_TILE = 8
S5_BLOCKS = 4
S5_W = S5_TILE * 128
D_FF = 2816
LANE = 128
VMEM_LIMIT = 56 * 1024 * 1024

U_GATES = 0
U_QA, U_KA, U_VA, U_ZA = 24, 28, 32, 36
U_UB = 40
U_QC, U_KC, U_VC, U_GC = 44, 48, 52, 56
N_PROJ = 60 * LANE
PROJ_TN = 1280
UB_BLOCK = (U_UB * LANE) // PROJ_TN
UB_LOCAL = U_UB * LANE - UB_BLOCK * PROJ_TN
PROJ_DTYPE = BF16
HALO_ROWS = 16


def _cparams(sem):
    return pltpu.CompilerParams(dimension_semantics=sem, vmem_limit_bytes=VMEM_LIMIT)


def _dot(a, b):
    return jnp.dot(a.astype(BF16), b.astype(BF16), preferred_element_type=F32)


def _dot_nt(a, b):
    return lax.dot_general(a.astype(BF16), b.astype(BF16), (((1,), (1,)), ((), ())),
                           preferred_element_type=F32)


def _dot_tn(a, b):
    return lax.dot_general(a.astype(BF16), b.astype(BF16), (((0,), (0,)), ((), ())),
                           preferred_element_type=F32)


def _sigmoid(x):
    return jax.nn.sigmoid(x)


def _silu(x):
    return x * _sigmoid(x)


def _softplus(x):
    return jnp.maximum(x, 0.0) + jnp.log1p(jnp.exp(-jnp.abs(x)))


def _gelu_tanh(x):
    return 0.5 * x * (1.0 + jnp.tanh(math.sqrt(2.0 / math.pi) * (x + 0.044715 * x * x * x)))


def _rms(x):
    return x * lax.rsqrt(jnp.mean(x * x, axis=-1, keepdims=True) + EPS)


def _split3(x):
    hi = x.astype(BF16).astype(F32)
    r1 = x - hi
    mid = r1.astype(BF16).astype(F32)
    lo = (r1 - mid).astype(BF16).astype(F32)
    return hi, mid, lo


def _mod_kernel(c_ref, w_ref, b_ref, o_ref):
    o_ref[...] = _dot(_silu(c_ref[...]), w_ref[...]) + b_ref[...]


def _modulation(cond8, w_mod, b_mod):
    depth, _, n = w_mod.shape
    tn = 1536
    return pl.pallas_call(
        _mod_kernel,
        grid=(depth, n // tn),
        in_specs=[pl.BlockSpec((8, D_MODEL), lambda l, j: (0, 0)),
                  pl.BlockSpec((None, D_MODEL, tn), lambda l, j: (l, 0, j)),
                  pl.BlockSpec((None, 1, tn), lambda l, j: (l, 0, j))],
        out_specs=pl.BlockSpec((None, 8, tn), lambda l, j: (l, 0, j)),
        out_shape=jax.ShapeDtypeStruct((depth, 8, n), F32),
        compiler_params=_cparams(("parallel", "parallel")),
        name="modulation",
    )(cond8, w_mod, b_mod.reshape(depth, 1, n))


def _mod_row_map(tm, rows_per_cond):
    if rows_per_cond is None:
        return lambda i: 0
    return lambda i: 1 + (i * tm) // rows_per_cond


def _inproj_kernel(x_ref, mod_ref, nw_ref, w_ref, wab_ref, proj_ref, ab_ref, u4_ref, h_scr, u_scr):
    j = pl.program_id(1)
    rows8 = u4_ref.shape[1]

    @pl.when(j == 0)
    def _():
        h = _rms(x_ref[...]) * nw_ref[...]
        h = h * (1.0 + mod_ref[1:2, :]) + mod_ref[0:1, :]
        hb = h.astype(BF16)
        h_scr[...] = hb
        ab_ref[...] = jnp.dot(hb, wab_ref[...], preferred_element_type=F32)

    acc = jnp.dot(h_scr[...], w_ref[...], preferred_element_type=F32)
    proj_ref[...] = acc.astype(PROJ_DTYPE)

    @pl.when(j == UB_BLOCK)
    def _():
        for b in range(S5_BLOCKS):
            u_scr[b] = acc[:, UB_LOCAL + LANE * b:UB_LOCAL + LANE * (b + 1)]
        for b in range(S5_BLOCKS):
            for t in range(S5_TILE):
                u4_ref[b, :, LANE * t:LANE * (t + 1)] = u_scr[b, pl.ds(t, rows8, stride=S5_TILE), :]


def _in_projection(x2d, mod6, norm_w, w_main, w_ab, rows_per_cond):
    n = x2d.shape[0]
    tm, tn = 1024, PROJ_TN
    row = _mod_row_map(tm, rows_per_cond)
    return pl.pallas_call(
        _inproj_kernel,
        grid=(n // tm, N_PROJ // tn),
        in_specs=[pl.BlockSpec((tm, D_MODEL), lambda i, j: (i, 0)),
                  pl.BlockSpec((None, 6, D_MODEL), lambda i, j: (row(i), 0, 0)),
                  pl.BlockSpec((1, D_MODEL), lambda i, j: (0, 0)),
                  pl.BlockSpec((D_MODEL, tn), lambda i, j: (0, j)),
                  pl.BlockSpec((D_MODEL, LANE), lambda i, j: (0, 0))],
        out_specs=[pl.BlockSpec((tm, tn), lambda i, j: (i, j)),
                   pl.BlockSpec((tm, LANE), lambda i, j: (i, 0)),
                   pl.BlockSpec((S5_BLOCKS, tm // S5_TILE, S5_W), lambda i, j: (0, i, 0))],
        out_shape=[jax.ShapeDtypeStruct((n, N_PROJ), PROJ_DTYPE),
                   jax.ShapeDtypeStruct((n, LANE), F32),
                   jax.ShapeDtypeStruct((S5_BLOCKS, n // S5_TILE, S5_W), F32)],
        scratch_shapes=[pltpu.VMEM((tm, D_MODEL), BF16), pltpu.VMEM((S5_BLOCKS, tm, LANE), F32)],
        compiler_params=_cparams(("parallel", "arbitrary")),
        name="in_projection",
    )(x2d, mod6, norm_w, w_main, w_ab)


M_INCL, M_STRICT, M_EYE, M_LEVEL = 0, 2, 4, 4
N_MASKS = 11


def _gdn_masks():
    r = jnp.arange(SEG)[:, None]
    c = jnp.arange(SEG)[None, :]
    same = (r // GDN_CHUNK) == (c // GDN_CHUNK)
    incl = [same & (r >= c), same & (r <= c)]
    strict = [same & (r > c), same & (r < c)]
    eye = [r == c]
    levels = [((r >> k) == (c >> k)) & ((r >> (k - 1)) != (c >> (k - 1))) for k in range(1, 7)]
    return jnp.stack(incl + strict + eye + levels).astype(F32)


def _gdn_pre_kernel(*refs, segs_per_seq):
    has_halo = segs_per_seq > 1
    q_ref, k_ref, v_ref = refs[:3]
    pos = 3
    halos = None
    if has_halo:
        halos = refs[3:9]
        pos = 9
    (ab_ref, wq_ref, wk_ref, wv_ref, gp_ref, msk_ref, tri_ref, lvl_ref,
     u_ref, w_ref, qg_ref, kg_ref, qkm_ref, eg_ref) = refs[pos:]
    seg = pl.program_id(0)
    rowi = lax.broadcasted_iota(jnp.int32, (SEG, W_MIX), 0)
    lane = lax.broadcasted_iota(jnp.int32, (SEG, LANE), 1)

    def conv_act(x_ref, cw_ref, prev_ref, next_ref):
        x = x_ref[...].astype(F32)
        xm = pltpu.roll(x, 1, 0)
        xp = pltpu.roll(x, SEG - 1, 0)
        if has_halo:
            p = seg % segs_per_seq
            prev = jnp.where(p > 0, prev_ref[HALO_ROWS - 1:HALO_ROWS, :].astype(F32), 0.0)
            nxt = jnp.where(p < segs_per_seq - 1, next_ref[0:1, :].astype(F32), 0.0)
        else:
            prev = jnp.zeros((1, W_MIX), F32)
            nxt = prev
        xm = jnp.where(rowi == 0, prev, xm)
        xp = jnp.where(rowi == SEG - 1, nxt, xp)
        return _silu(cw_ref[0:1, :] * xm + cw_ref[1:2, :] * x + cw_ref[2:3, :] * xp)

    hq = halos[0:2] if has_halo else (None, None)
    hk = halos[2:4] if has_halo else (None, None)
    hv = halos[4:6] if has_halo else (None, None)
    qa_all = conv_act(q_ref, wq_ref, *hq)
    ka_all = conv_act(k_ref, wk_ref, *hk)
    va_all = conv_act(v_ref, wv_ref, *hv)

    abv = ab_ref[...]
    log_a = -jnp.exp(gp_ref[0:1, :]) * _softplus(abv + gp_ref[1:2, :])
    beta_all = _sigmoid(abv)

    def col(x, idx):
        return jnp.sum(jnp.where(lane == idx, x, 0.0), axis=1, keepdims=True)

    heads = []
    for h in range(N_HEADS):
        hs = slice(h * D_HEAD, (h + 1) * D_HEAD)
        qa, ka = qa_all[:, hs], ka_all[:, hs]
        qn = qa * lax.rsqrt(jnp.sum(qa * qa, axis=-1, keepdims=True) + EPS) * (D_HEAD ** -0.5)
        kn = ka * lax.rsqrt(jnp.sum(ka * ka, axis=-1, keepdims=True) + EPS)
        heads.append(dict(qn=qn, kn=kn, va=va_all[:, hs], kk=_dot_nt(kn, kn), qk=_dot_nt(qn, kn)))

    chains = []
    for h in range(N_HEADS):
        for d in range(2):
            hd = heads[h]
            g = jnp.broadcast_to(col(log_a, d * N_HEADS + h), (SEG, LANE))
            beta = col(beta_all, 2 * N_HEADS + d * N_HEADS + h)
            ghi, gmid, glo = _split3(g)
            pieces = jnp.where(lane == 0, ghi, jnp.where(lane == 1, gmid, jnp.where(lane == 2, glo, 0.0)))
            cs = jnp.dot(tri_ref[d], pieces.astype(BF16), preferred_element_type=F32)
            gc = jnp.broadcast_to(jnp.sum(jnp.where(lane < 3, cs, 0.0), axis=1, keepdims=True), (SEG, LANE))
            chi, cmid, clo = _split3(gc)
            a_mat = jnp.where(lane == 0, chi, jnp.where(lane == 1, cmid, jnp.where(lane == 2, clo,
                              jnp.where(lane < 6, 1.0, 0.0))))
            b_mat = jnp.where(lane < 3, 1.0, jnp.where(lane == 3, -chi, jnp.where(lane == 4, -cmid,
                              jnp.where(lane == 5, -clo, 0.0))))
            e = jnp.exp(jnp.minimum(_dot_nt(a_mat, b_mat), 0.0))
            lmat = (beta * hd['kk']) * (e * msk_ref[M_STRICT + d])
            chains.append(dict(h=h, d=d, beta=beta, gc=gc, lmat=lmat.astype(BF16),
                               qkm=hd['qk'] * (e * msk_ref[M_INCL + d]),
                               x=(msk_ref[M_EYE] - lmat * msk_ref[M_LEVEL + 1]).astype(BF16)))

    for lvl in range(2, 7):
        for ch in chains:
            c = ch['lmat'] * lvl_ref[lvl - 1]
            t = jnp.dot(c, ch['x'], preferred_element_type=F32).astype(BF16)
            ch['x'] = ch['x'] - jnp.dot(ch['x'], t, preferred_element_type=F32).astype(BF16)

    for ch in chains:
        h, d, beta, gc = ch['h'], ch['d'], ch['beta'], ch['gc']
        hd = heads[h]
        qn, kn, va = hd['qn'], hd['kn'], hd['va']
        hs = slice(h * D_HEAD, (h + 1) * D_HEAD)
        egc = jnp.exp(gc)
        rhs = jnp.concatenate([va * beta, kn * (beta * egc)], axis=1).astype(BF16)
        uw = jnp.dot(ch['x'], rhs, preferred_element_type=F32)
        u_ref[d, :, hs] = uw[:, :D_HEAD]
        w_ref[d, :, hs] = uw[:, D_HEAD:].astype(BF16)
        qg_ref[d, :, hs] = (qn * egc).astype(BF16)
        edge = GDN_CHUNK - 1 if d == 0 else 0
        for c4 in range(SEG // GDN_CHUNK):
            lo_r = c4 * GDN_CHUNK
            rs = slice(lo_r, lo_r + GDN_CHUNK)
            g_edge = gc[lo_r + edge:lo_r + edge + 1, :]
            kg_ref[d, rs, hs] = (kn[rs, :] * jnp.exp(g_edge - gc[rs, :])).astype(BF16)
            eg_ref[d, c4, h:h + 1, :] = jnp.exp(g_edge)
            qkm_ref[d, h, rs, :] = ch['qkm'][rs, rs].astype(BF16)


def _gdn_precompute(proj, ab, wconv, gate_par, masks, tri, levels, seq_len):
    n = proj.shape[0]
    segs_per_seq = seq_len // SEG
    nseg = n // SEG
    units = (U_QA // N_HEADS, U_KA // N_HEADS, U_VA // N_HEADS)
    in_specs = [pl.BlockSpec((SEG, W_MIX), lambda s, cb=cb: (s, cb)) for cb in units]
    args = [proj, proj, proj]
    if segs_per_seq > 1:
        r8 = SEG // HALO_ROWS
        last8 = n // HALO_ROWS - 1
        for cb in units:
            in_specs.append(pl.BlockSpec((HALO_ROWS, W_MIX), lambda s, cb=cb: (jnp.maximum(s * r8 - 1, 0), cb)))
            in_specs.append(pl.BlockSpec((HALO_ROWS, W_MIX),
                                         lambda s, cb=cb: (jnp.minimum((s + 1) * r8, last8), cb)))
            args += [proj, proj]
    in_specs += [pl.BlockSpec((SEG, LANE), lambda s: (s, 0))]
    in_specs += [pl.BlockSpec((3, W_MIX), lambda s, o=o: (0, o)) for o in range(3)]
    in_specs += [pl.BlockSpec((8, LANE), lambda s: (0, 0)),
                 pl.BlockSpec((N_MASKS, SEG, SEG), lambda s: (0, 0, 0)),
                 pl.BlockSpec((2, SEG, SEG), lambda s: (0, 0, 0)),
                 pl.BlockSpec((6, SEG, SEG), lambda s: (0, 0, 0))]
    args += [ab, wconv, wconv, wconv, gate_par, masks, tri, levels]
    big = pl.BlockSpec((2, SEG, W_MIX), lambda s: (0, s, 0))
    cpseg = SEG // GDN_CHUNK
    return pl.pallas_call(
        functools.partial(_gdn_pre_kernel, segs_per_seq=segs_per_seq),
        grid=(nseg,),
        in_specs=in_specs,
        out_specs=[big, big, big, big,
                   pl.BlockSpec((2, N_HEADS, SEG, GDN_CHUNK), lambda s: (0, 0, s, 0)),
                   pl.BlockSpec((2, cpseg, N_HEADS, LANE), lambda s: (0, s, 0, 0))],
        out_shape=[jax.ShapeDtypeStruct((2, n, W_MIX), F32),
                   jax.ShapeDtypeStruct((2, n, W_MIX), BF16),
                   jax.ShapeDtypeStruct((2, n, W_MIX), BF16),
                   jax.ShapeDtypeStruct((2, n, W_MIX), BF16),
                   jax.ShapeDtypeStruct((2, N_HEADS, n, GDN_CHUNK), BF16),
                   jax.ShapeDtypeStruct((2, n // GDN_CHUNK, N_HEADS, LANE), F32)],
        compiler_params=_cparams(("parallel",)),
        name="gdn_precompute",
    )(*args)


GDN_SCAN_ROWS = 1024


def _gdn_scan_kernel(*refs, chunks_per_seq, has_init):
    ins = refs[:12]
    dir_refs = [ins[0::2], ins[1::2]]
    if has_init:
        s0_ref, of_ref, ob_ref, s_scr = refs[12:]
        fin_refs = None
    else:
        of_ref, ob_ref, sff_ref, sfb_ref, s_scr = refs[12:]
        fin_refs = (sff_ref, sfb_ref)
    o_refs = (of_ref, ob_ref)
    nchunk = of_ref.shape[0] // GDN_CHUNK
    cps = chunks_per_seq

    @pl.when(pl.program_id(1) == 0)
    def _():
        if has_init:
            s_scr[...] = s0_ref[...]
        else:
            s_scr[...] = jnp.zeros_like(s_scr)

    def body(i, carry):
        chains = []
        for h in range(N_HEADS):
            for d in range(2):
                c = i if d == 0 else nchunk - 1 - i
                chains.append(dict(h=h, d=d, c=c, hs=slice(h * D_HEAD, (h + 1) * D_HEAD),
                                   rsl=pl.ds(pl.multiple_of(c * GDN_CHUNK, GDN_CHUNK), GDN_CHUNK)))
        for ch in chains:
            s = s_scr[ch['d'], ch['h']]
            if not has_init:
                s = jnp.where(i % cps == 0, 0.0, s)
            ch['s'] = s
            ch['sb'] = s.astype(BF16)
        for ch in chains:
            u_ref, w_ref, qg_ref = dir_refs[ch['d']][:3]
            rsl, hs = ch['rsl'], ch['hs']
            ch['vb'] = (u_ref[rsl, hs] - jnp.dot(w_ref[rsl, hs], ch['sb'], preferred_element_type=F32)).astype(BF16)
            ch['o'] = jnp.dot(qg_ref[rsl, hs], ch['sb'], preferred_element_type=F32)
        for ch in chains:
            kg_ref, qkm_ref, eg_ref = dir_refs[ch['d']][3:]
            rsl, hs, h = ch['rsl'], ch['hs'], ch['h']
            ch['o'] = ch['o'] + jnp.dot(qkm_ref[h, rsl, :], ch['vb'], preferred_element_type=F32)
            ch['s_new'] = ch['s'] * eg_ref[ch['c']][h:h + 1, :] + _dot_tn(kg_ref[rsl, hs], ch['vb'])
        for ch in chains:
            s_scr[ch['d'], ch['h']] = ch['s_new']
            o_refs[ch['d']][ch['rsl'], ch['hs']] = ch['o']
        if not has_init:
            @pl.when(i % cps == cps - 1)
            def _():
                for ch in chains:
                    fin_refs[ch['d']][ch['c'] // cps, ch['h']] = ch['s_new']
        return carry

    lax.fori_loop(0, nchunk, body, 0)


def _gdn_scan(pre, s0, seq_len):
    n = pre[0].shape[1]
    br = GDN_SCAN_ROWS
    has_init = s0 is not None
    group_rows = seq_len if has_init else n
    ngroups, nblk = n // group_rows, group_rows // br
    fwd = lambda g, j: g * nblk + j
    bwd = lambda g, j: g * nblk + (nblk - 1 - j)
    in_specs, args = [], []
    for arr, kind in zip(pre, ("row", "row", "row", "row", "qkm", "eg")):
        for d, pos in ((0, fwd), (1, bwd)):
            if kind == "row":
                in_specs.append(pl.BlockSpec((None, br, W_MIX), lambda g, j, d=d, pos=pos: (d, pos(g, j), 0)))
            elif kind == "qkm":
                in_specs.append(pl.BlockSpec((None, N_HEADS, br, GDN_CHUNK),
                                             lambda g, j, d=d, pos=pos: (d, 0, pos(g, j), 0)))
            else:
                in_specs.append(pl.BlockSpec((None, br // GDN_CHUNK, N_HEADS, LANE),
                                             lambda g, j, d=d, pos=pos: (d, pos(g, j), 0, 0)))
            args.append(arr)
    out_specs = [pl.BlockSpec((br, W_MIX), lambda g, j: (fwd(g, j), 0)),
                 pl.BlockSpec((br, W_MIX), lambda g, j: (bwd(g, j), 0))]
    out_shape = [jax.ShapeDtypeStruct((n, W_MIX), F32), jax.ShapeDtypeStruct((n, W_MIX), F32)]
    if has_init:
        in_specs.append(pl.BlockSpec((None, 2, N_HEADS, D_HEAD, D_HEAD), lambda g, j: (g, 0, 0, 0, 0)))
        args.append(s0)
    else:
        spb = br // seq_len
        fin_shape = jax.ShapeDtypeStruct((n // seq_len, N_HEADS, D_HEAD, D_HEAD), F32)
        out_specs += [pl.BlockSpec((spb, N_HEADS, D_HEAD, D_HEAD), lambda g, j: (fwd(g, j), 0, 0, 0)),
                      pl.BlockSpec((spb, N_HEADS, D_HEAD, D_HEAD), lambda g, j: (bwd(g, j), 0, 0, 0))]
        out_shape += [fin_shape, fin_shape]
    res = pl.pallas_call(
        functools.partial(_gdn_scan_kernel, chunks_per_seq=seq_len // GDN_CHUNK, has_init=has_init),
        grid=(ngroups, nblk),
        in_specs=in_specs, out_specs=out_specs, out_shape=out_shape,
        scratch_shapes=[pltpu.VMEM((2, N_HEADS, D_HEAD, D_HEAD), F32)],
        compiler_params=_cparams(("parallel", "arbitrary")),
        name="gdn_scan",
    )(*args)
    fin = None if has_init else jnp.stack([res[2], res[3]], axis=1)
    return res[0], res[1], fin


def _s5_matrices(lam_re, lam_im, log_dt, b_re, b_im, c_re, c_im):
    t = S5_TILE
    step = jnp.exp(log_dt)[:, :, None]
    lr, li = lam_re * step, lam_im * step
    js = jnp.arange(t + 1, dtype=F32)[:, None, None, None]
    mag = jnp.exp(lr[None] * js)
    pw_re, pw_im = mag * jnp.cos(li[None] * js), mag * jnp.sin(li[None] * js)
    a_re, a_im = pw_re[1], pw_im[1]
    den = lam_re * lam_re + lam_im * lam_im
    z_re = ((a_re - 1.0) * lam_re + a_im * lam_im) / den
    z_im = (a_im * lam_re - (a_re - 1.0) * lam_im) / den
    zb_re = z_re[..., None] * b_re[None] - z_im[..., None] * b_im[None]
    zb_im = z_re[..., None] * b_im[None] + z_im[..., None] * b_re[None]
    w_re = pw_re[..., None] * zb_re[None] - pw_im[..., None] * zb_im[None]
    w_im = pw_re[..., None] * zb_im[None] + pw_im[..., None] * zb_re[None]
    taps = (jnp.einsum('gcp,jdgpi->jdgic', c_re, w_re[:t]) - jnp.einsum('gcp,jdgpi->jdgic', c_im, w_im[:t]))
    nb = S5_BLOCKS
    zero_tap = jnp.zeros_like(taps[0, 0])

    def blocked(x, axis):
        return x.reshape(x.shape[:axis] + (nb, 8) + x.shape[axis + 1:])

    tm, gm, cm = [], [], []
    for d in range(2):
        sign = 1 if d == 0 else -1
        kst = jnp.stack([jnp.stack([taps[sign * (tt - ss), d] if sign * (tt - ss) >= 0 else zero_tap
                                    for tt in range(t)]) for ss in range(t)])
        kst = blocked(kst, 2)
        tm.append(jnp.transpose(kst, (2, 0, 3, 4, 1, 5)).reshape(nb, S5_W, LANE))
        e_in = [t - 1 - ss for ss in range(t)] if d == 0 else list(range(t))
        wr_in = jnp.stack([w_re[e, d] for e in e_in])
        wi_in = jnp.stack([w_im[e, d] for e in e_in])
        ws = blocked(jnp.stack([wr_in, wi_in], axis=2), 1)
        gm.append(jnp.transpose(ws, (1, 0, 2, 5, 3, 4)).reshape(nb, S5_W, LANE))
        e_out = [tt + 1 for tt in range(t)] if d == 0 else [t - tt for tt in range(t)]
        pr = jnp.stack([pw_re[e, d] for e in e_out])
        pi = jnp.stack([pw_im[e, d] for e in e_out])
        wo_re = c_re[None] * pr[:, :, None, :] - c_im[None] * pi[:, :, None, :]
        wo_im = c_re[None] * pi[:, :, None, :] + c_im[None] * pr[:, :, None, :]
        wo = blocked(jnp.stack([wo_re, -wo_im], axis=0), 2)
        cm.append(jnp.transpose(wo, (2, 0, 3, 5, 1, 4)).reshape(nb, S5_W, LANE))
    a8r = pw_re[t].reshape(2, nb, 4, LANE)
    a8i = pw_im[t].reshape(2, nb, 4, LANE)
    a1 = jnp.concatenate([a8r, a8r], axis=2)
    a2 = jnp.concatenate([-a8i, a8i], axis=2)
    taps_both = jnp.stack([tm[0] + tm[1], jnp.zeros_like(tm[0])])
    return jnp.stack([taps_both, jnp.stack(gm), jnp.stack(cm)]), a1, a2


def _s5_expand_consts():
    col = jnp.arange(S5_W)
    src = jnp.arange(LANE)
    ex_tc = (src[:, None] == ((col // LANE) * GC_B + col % GC_B)[None, :])
    ex_rp = (src[:, None] == ((col // (S5_W // 2)) * P_B + col % P_B)[None, :])
    g_tc = (col // GC_B) % 8
    g_rp = (col // P_B) % 8
    masks = [g_tc[:, None] == g_tc[None, :], g_tc[:, None] == g_rp[None, :], g_rp[:, None] == g_tc[None, :]]
    return (jnp.stack([ex_tc, ex_rp, ex_tc]).astype(BF16), jnp.stack(masks).astype(BF16))


def _s5_expand_kernel(c_ref, ex_ref, m_ref, o_ref):
    full = jnp.dot(c_ref[...].astype(BF16), ex_ref[...], preferred_element_type=F32)
    o_ref[...] = (full * m_ref[...].astype(F32)).astype(BF16)


def _s5_expand(compact, consts):
    ex, masks = consts
    nl, _, _, nb = compact.shape[:4]
    return pl.pallas_call(
        _s5_expand_kernel,
        grid=(3, nl, 2, nb),
        in_specs=[pl.BlockSpec((None, None, None, None, S5_W, LANE), lambda k, l, d, b: (l, k, d, b, 0, 0)),
                  pl.BlockSpec((None, LANE, S5_W), lambda k, l, d, b: (k, 0, 0)),
                  pl.BlockSpec((None, S5_W, S5_W), lambda k, l, d, b: (k, 0, 0))],
        out_specs=pl.BlockSpec((None, None, None, None, S5_W, S5_W), lambda k, l, d, b: (l, k, d, b, 0, 0)),
        out_shape=jax.ShapeDtypeStruct(compact.shape[:4] + (S5_W, S5_W), BF16),
        compiler_params=_cparams(("parallel",) * 4),
        name="s5_expand",
    )(compact, ex, masks)


def _s5_in_kernel(u_ref, gm_ref, g_ref):
    rows = u_ref.shape[0]
    g = _dot(u_ref[...], gm_ref[...])
    for s in range(8):
        g_ref[pl.ds(s, rows, stride=8), :] = g[:, LANE * s:LANE * (s + 1)]


def _s5_tile_inputs(u4f, gm):
    nb, rows, _ = u4f.shape
    return pl.pallas_call(
        _s5_in_kernel,
        grid=(nb, 2),
        in_specs=[pl.BlockSpec((None, rows, S5_W), lambda b, d: (b, 0, 0)),
                  pl.BlockSpec((None, None, S5_W, S5_W), lambda b, d: (d, b, 0, 0))],
        out_specs=pl.BlockSpec((None, None, rows * 8, LANE), lambda b, d: (d, b, 0, 0)),
        out_shape=jax.ShapeDtypeStruct((2, nb, rows * 8, LANE), F32),
        compiler_params=_cparams(("parallel", "parallel")),
        name="s5_tile_inputs",
    )(u4f, gm)


def _s5_scan_kernel(*refs, nseq, rows_per_seq, has_init):
    g_ref, a1_ref, a2_ref = refs[:3]
    if has_init:
        x0_ref, xp_ref = refs[3:]
        fin_ref = None
    else:
        xp_ref, fin_ref = refs[3:]
    nblk = g_ref.shape[0]
    backward = pl.program_id(0) == 1
    chains = [(b, s) for b in range(nblk) for s in range(nseq)]

    def body(i, xs):
        n = jnp.where(backward, rows_per_seq - 1 - i, i)
        out = []
        for (b, s), x in zip(chains, xs):
            r = s * rows_per_seq + n
            xp_ref[b, r] = x
            out.append(a1_ref[b] * x + a2_ref[b] * pltpu.roll(x, 4, 0) + g_ref[b, r])
        return tuple(out)

    if has_init:
        init = tuple(x0_ref[b] for (b, s) in chains)
    else:
        init = tuple(jnp.zeros((8, LANE), F32) for _ in chains)
    xs = lax.fori_loop(0, rows_per_seq, body, init)
    if not has_init:
        for (b, s), x in zip(chains, xs):
            fin_ref[s] = x


def _s5_scan(g5, a1, a2, x0, seq_rows):
    _, nb, rows, _, _ = g5.shape
    nseq = rows // seq_rows
    if x0 is None:
        return pl.pallas_call(
            functools.partial(_s5_scan_kernel, nseq=nseq, rows_per_seq=seq_rows, has_init=False),
            grid=(2, nb),
            in_specs=[pl.BlockSpec((None, 1, rows, 8, LANE), lambda d, b: (d, b, 0, 0, 0)),
                      pl.BlockSpec((None, 1, 8, LANE), lambda d, b: (d, b, 0, 0)),
                      pl.BlockSpec((None, 1, 8, LANE), lambda d, b: (d, b, 0, 0))],
            out_specs=[pl.BlockSpec((None, 1, rows, 8, LANE), lambda d, b: (d, b, 0, 0, 0)),
                       pl.BlockSpec((None, None, nseq, 8, LANE), lambda d, b: (d, b, 0, 0, 0))],
            out_shape=[jax.ShapeDtypeStruct(g5.shape, F32),
                       jax.ShapeDtypeStruct((2, nb, nseq, 8, LANE), F32)],
            compiler_params=_cparams(("parallel", "parallel")),
            name="s5_scan_ctx",
        )(g5, a1, a2)
    xprev = pl.pallas_call(
        functools.partial(_s5_scan_kernel, nseq=1, rows_per_seq=seq_rows, has_init=True),
        grid=(2, nseq),
        in_specs=[pl.BlockSpec((None, nb, seq_rows, 8, LANE), lambda d, s: (d, 0, s, 0, 0)),
                  pl.BlockSpec((None, nb, 8, LANE), lambda d, s: (d, 0, 0, 0)),
                  pl.BlockSpec((None, nb, 8, LANE), lambda d, s: (d, 0, 0, 0)),
                  pl.BlockSpec((None, None, nb, 8, LANE), lambda d, s: (s, d, 0, 0, 0))],
        out_specs=pl.BlockSpec((None, nb, seq_rows, 8, LANE), lambda d, s: (d, 0, s, 0, 0)),
        out_shape=jax.ShapeDtypeStruct(g5.shape, F32),
        compiler_params=_cparams(("parallel", "parallel")),
        name="s5_scan_latent",
    )(g5, a1, a2, x0)
    return xprev, None


def _s5_out_kernel(u_ref, xp_ref, tm_ref, cm_ref, y_ref):
    tr = u_ref.shape[0]
    acc = jnp.dot(u_ref[...].astype(BF16), tm_ref[...], preferred_element_type=F32)
    for d in range(2):
        xp = jnp.concatenate([xp_ref[d, pl.ds(s, tr, stride=8), :] for s in range(8)], axis=1)
        acc += _dot(xp, cm_ref[d])
    for t in range(S5_TILE):
        y_ref[pl.ds(t, tr, stride=S5_TILE), :] = acc[:, LANE * t:LANE * (t + 1)]


def _s5_outputs(u4f, xprev, tm, cm):
    nb, rows, _ = u4f.shape
    tr = 512
    return pl.pallas_call(
        _s5_out_kernel,
        grid=(nb, rows // tr),
        in_specs=[pl.BlockSpec((None, tr, S5_W), lambda b, r: (b, r, 0)),
                  pl.BlockSpec((2, None, tr * 8, LANE), lambda b, r: (0, b, r, 0)),
                  pl.BlockSpec((None, None, S5_W, S5_W), lambda b, r: (0, b, 0, 0)),
                  pl.BlockSpec((2, None, S5_W, S5_W), lambda b, r: (0, b, 0, 0))],
        out_specs=pl.BlockSpec((tr * S5_TILE, LANE), lambda b, r: (r, b)),
        out_shape=jax.ShapeDtypeStruct((rows * S5_TILE, S5_BLOCKS * LANE), F32),
        compiler_params=_cparams(("parallel", "parallel")),
        name="s5_outputs",
    )(u4f, xprev, tm, cm)


def _s5_mixer(u4f, mats, x0, seq_len):
    tm, gm, cm, a1, a2 = mats
    nb, rows, _ = u4f.shape
    g = _s5_tile_inputs(u4f, gm)
    xprev, fin = _s5_scan(g.reshape(2, nb, rows, 8, LANE), a1, a2, x0, seq_len // S5_TILE)
    yd = _s5_outputs(u4f, xprev.reshape(2, nb, rows * 8, LANE), tm, cm)
    return yd, fin


def _ret_tables():
    h = jnp.arange(N_HEADS, dtype=F32)
    lg = jnp.log1p(-jnp.exp2(-5.0 - h))
    lgf, lgb = lg[:, None], lg[::-1][:, None]
    i = jnp.arange(SEG, dtype=F32)
    dist = i[:, None] - i[None, :]
    d_f = jnp.where(dist >= 0, jnp.exp(lgf[:, :, None] * jnp.maximum(dist, 0.0)), 0.0)
    d_b = jnp.where(dist <= 0, jnp.exp(lgb[:, :, None] * jnp.maximum(-dist, 0.0)), 0.0)
    dsum = d_f + d_b
    dec = jnp.stack([jnp.exp(lgf * (i + 1.0)), jnp.exp(lgf * (SEG - 1.0 - i)),
                     jnp.exp(lgb * (SEG - i)), jnp.exp(lgb * i)], axis=1)
    dec = jnp.broadcast_to(dec[..., None], (N_HEADS, 4, SEG, LANE))
    cd = jnp.stack([jnp.exp(lgf[:, 0] * SEG), jnp.exp(lgb[:, 0] * SEG)], axis=1)
    cd = jnp.broadcast_to(jnp.pad(cd, ((0, 0), (0, 6)))[..., None], (N_HEADS, 8, LANE))
    return dsum, dec, cd


def _rope_tables(n_tokens):
    n_rows = n_tokens // GRID_W
    rows = jnp.repeat(jnp.arange(n_rows), GRID_W).astype(F32)
    cols = jnp.tile(jnp.arange(GRID_W), n_rows).astype(F32)
    pairs = D_HEAD // 4
    freqs = ROPE_BASE ** (-jnp.arange(pairs, dtype=F32) / pairs)
    ang = jnp.concatenate([rows[:, None] * freqs, cols[:, None] * freqs], axis=-1)
    cos, sin = jnp.cos(ang), jnp.sin(ang)
    return jnp.concatenate([cos, cos], axis=-1), jnp.concatenate([-sin, sin], axis=-1)


def _ret_kernel(*refs, is_ctx):
    q_ref, k_ref, v_ref, g_ref, dsum_ref, dec_ref, cd_ref = refs[:7]
    if is_ctx:
        o_ref, sfin_ref = refs[7:]
    else:
        c2_ref, s2_ref, s0_ref, o_ref, sbin_scr = refs[7:]
    rows = q_ref.shape[0]
    nchunk = rows // SEG
    scale = D_HEAD ** -0.5

    def chunk_rows(c):
        return pl.ds(pl.multiple_of(c * SEG, SEG), SEG)

    def load_qk(rsl):
        q, k = q_ref[rsl, :].astype(F32), k_ref[rsl, :].astype(F32)
        if not is_ctx:
            c2, s2 = c2_ref[rsl, :], s2_ref[rsl, :]
            q = q * c2 + pltpu.roll(q, D_HEAD // 2, 1) * s2
            k = k * c2 + pltpu.roll(k, D_HEAD // 2, 1) * s2
        return q, k * scale

    def finish(rsl, r):
        o_ref[rsl, :] = _rms(r) * _silu(g_ref[rsl, :].astype(F32))

    if is_ctx:
        def body(c, carry):
            rsl = chunk_rows(c)
            q, k = load_qk(rsl)
            v = v_ref[rsl, :]
            finish(rsl, _dot(_dot_nt(q, k) * dsum_ref[...], v))
            sfin_ref[c, 0] = _dot_tn(k * dec_ref[1], v)
            sfin_ref[c, 1] = _dot_tn(k * dec_ref[3], v)
            return carry
        lax.fori_loop(0, nchunk, body, 0, unroll=2)
    else:
        cd_f, cd_b = cd_ref[0:1, :], cd_ref[1:2, :]

        def back(i, s_b):
            c = nchunk - 1 - i
            rsl = chunk_rows(c)
            _, k = load_qk(rsl)
            sbin_scr[c] = s_b
            return s_b * cd_b + _dot_tn(k * dec_ref[3], v_ref[rsl, :])
        lax.fori_loop(0, nchunk, back, s0_ref[1], unroll=2)

        def fwd(c, s_f):
            rsl = chunk_rows(c)
            q, k = load_qk(rsl)
            v = v_ref[rsl, :]
            r = (_dot(_dot_nt(q, k) * dsum_ref[...], v) + _dot(q * dec_ref[0], s_f)
                 + _dot(q * dec_ref[2], sbin_scr[c]))
            finish(rsl, r)
            return s_f * cd_f + _dot_tn(k * dec_ref[1], v)
        lax.fori_loop(0, nchunk, fwd, s0_ref[0], unroll=2)


def _retention(proj, tables, rope, s0, seq_len, step_rows):
    dsum, dec, cd = tables
    n = proj.shape[0]
    is_ctx = s0 is None
    blk = lambda unit: pl.BlockSpec((step_rows, LANE), lambda s, h, unit=unit: (s, unit + h))
    in_specs = [blk(U_QC), blk(U_KC), blk(U_VC), blk(U_GC),
                pl.BlockSpec((None, SEG, SEG), lambda s, h: (h, 0, 0)),
                pl.BlockSpec((None, 4, SEG, LANE), lambda s, h: (h, 0, 0, 0)),
                pl.BlockSpec((None, 8, LANE), lambda s, h: (h, 0, 0))]
    args = [proj, proj, proj, proj, dsum, dec, cd]
    out_specs = [pl.BlockSpec((step_rows, LANE), lambda s, h: (s, h))]
    out_shape = [jax.ShapeDtypeStruct((n, W_MIX), F32)]
    scratch = []
    if is_ctx:
        assert seq_len == SEG
        out_specs.append(pl.BlockSpec((step_rows // SEG, 2, None, D_HEAD, D_HEAD), lambda s, h: (s, 0, h, 0, 0)))
        out_shape.append(jax.ShapeDtypeStruct((n // SEG, 2, N_HEADS, D_HEAD, D_HEAD), F32))
    else:
        assert step_rows == seq_len
        in_specs += [pl.BlockSpec((step_rows, LANE), lambda s, h: (0, 0)),
                     pl.BlockSpec((step_rows, LANE), lambda s, h: (0, 0)),
                     pl.BlockSpec((None, 2, None, D_HEAD, D_HEAD), lambda s, h: (s, 0, h, 0, 0))]
        args += [rope[0], rope[1], s0]
        scratch = [pltpu.VMEM((step_rows // SEG, D_HEAD, D_HEAD), F32)]
    res = pl.pallas_call(
        functools.partial(_ret_kernel, is_ctx=is_ctx),
        grid=(n // step_rows, N_HEADS),
        in_specs=in_specs, out_specs=out_specs, out_shape=out_shape, scratch_shapes=scratch,
        compiler_params=_cparams(("parallel", "parallel")),
        name="retention",
    )(*args)
    return res[0], (res[1] if is_ctx else None)


def _merge_kernel(x_ref, of_ref, ob_ref, z_ref, na_ref, yd_ref, u_ref, oc_ref, gates_ref, mod_ref, d_ref,
                  wglu_ref, bglu_ref, wa_ref, wb_ref, wc_ref, wo_ref, o_ref):
    tot = of_ref[...] + ob_ref[...]
    o_a = jnp.concatenate([_rms(tot[:, h * D_HEAD:(h + 1) * D_HEAD]) * na_ref[...] for h in range(N_HEADS)],
                          axis=1) * _silu(z_ref[...].astype(F32))
    y = _gelu_tanh(u_ref[...].astype(F32) * d_ref[...] + yd_ref[...])
    y = y * _sigmoid(_dot(y, wglu_ref[...]) + bglu_ref[...])
    gt = _sigmoid(gates_ref[...].astype(F32))
    merged = (gt[:, :D_MODEL] * _dot(o_a, wa_ref[...])
              + gt[:, D_MODEL:2 * D_MODEL] * _dot(y, wb_ref[...])
              + gt[:, 2 * D_MODEL:] * _dot(oc_ref[...], wc_ref[...]))
    o_ref[...] = x_ref[...] + mod_ref[2:3, :] * _dot(merged, wo_ref[...])


def _merge(x2d, o_f, o_b, yd, o_c, proj, mod6, lw, rows_per_cond):
    n = x2d.shape[0]
    tm = 512
    row = _mod_row_map(tm, rows_per_cond)
    full = lambda shape: pl.BlockSpec(shape, lambda i: (0,) * len(shape), pipeline_mode=pl.Buffered(1))
    mix = lambda cb: pl.BlockSpec((tm, W_MIX), lambda i, cb=cb: (i, cb))
    return pl.pallas_call(
        _merge_kernel,
        grid=(n // tm,),
        in_specs=[pl.BlockSpec((tm, D_MODEL), lambda i: (i, 0)),
                  mix(0), mix(0), mix(U_ZA // N_HEADS), full((1, D_HEAD)),
                  mix(0), mix(U_UB // N_HEADS), mix(0),
                  pl.BlockSpec((tm, 3 * D_MODEL), lambda i: (i, 0)),
                  pl.BlockSpec((None, 6, D_MODEL), lambda i: (row(i), 0, 0)),
                  full((1, W_MIX)), full((W_MIX, W_MIX)), full((1, W_MIX)),
                  full((W_MIX, D_MODEL)), full((W_MIX, D_MODEL)), full((W_MIX, D_MODEL)),
                  full((D_MODEL, D_MODEL))],
        out_specs=pl.BlockSpec((tm, D_MODEL), lambda i: (i, 0)),
        out_shape=jax.ShapeDtypeStruct((n, D_MODEL), F32),
        compiler_params=_cparams(("parallel",)),
        name="merge",
    )(x2d, o_f, o_b, proj, lw['norm_a'], yd, proj, o_c, proj, mod6, lw['ssm_d'], lw['w_glu'], lw['b_glu'],
      lw['w_br_a'], lw['w_br_b'], lw['w_br_c'], lw['w_o'])


FFN_TM = 256
FFN_CW = 256
FFN_UP_CW = 512


def _ffn_kernel(x_ref, xr_ref, mod_ref, modr_ref, nw_ref, wup_ref, cw_ref, cb_ref, wd_ref, fn_ref, o_ref,
                hu_a, hu_b, *, seq_len, final):
    i = pl.program_id(0)
    tm = x_ref.shape[0]
    blocks_per_seq = max(seq_len // tm, 1)

    @pl.when(i == 0)
    def _():
        hu_a[...] = jnp.zeros_like(hu_a)
        hu_b[...] = jnp.zeros_like(hu_b)

    p = (i + blocks_per_seq - 1) % blocks_per_seq
    has_prev = p > 0
    has_next = p < blocks_per_seq - 1
    rowi = lax.broadcasted_iota(jnp.int32, (tm, FFN_CW), 0) % min(seq_len, tm)
    last_row = min(seq_len, tm) - 1

    def step(new_ref, cur_ref):
        prev_row = jnp.where(has_prev, new_ref[tm - 1:tm, :], 0.0)
        h = _rms(x_ref[...]) * nw_ref[...]
        h = (h * (1.0 + mod_ref[4:5, :]) + mod_ref[3:4, :]).astype(BF16)
        up_done = set()

        def up(q):
            if q not in up_done:
                up_done.add(q)
                cs = slice(q * FFN_UP_CW, (q + 1) * FFN_UP_CW)
                new_ref[:, cs] = jnp.dot(h, wup_ref[:, cs], preferred_element_type=F32)

        def conv(lo):
            cs = slice(lo, lo + FFN_CW)
            up(lo // FFN_UP_CW)
            up((lo + FFN_CW - 1) // FFN_UP_CW)
            next_row = jnp.where(has_next, new_ref[0:1, cs], 0.0)
            hu = cur_ref[:, cs]
            hm = jnp.where(rowi == 0, prev_row[:, cs], pltpu.roll(hu, 1, 0))
            hp = jnp.where(rowi == last_row, next_row, pltpu.roll(hu, tm - 1, 0))
            return cw_ref[0:1, cs] * hm + cw_ref[1:2, cs] * hu + cw_ref[2:3, cs] * hp + cb_ref[:, cs]

        acc = None
        for cb in range(D_FF // FFN_CW):
            lo = cb * FFN_CW
            act = (_silu(conv(lo)) * conv(D_FF + lo)).astype(BF16)
            part = jnp.dot(act, wd_ref[lo:lo + FFN_CW, :], preferred_element_type=F32)
            acc = part if acc is None else acc + part
        for q in range(wup_ref.shape[1] // FFN_UP_CW):
            up(q)
        x2 = xr_ref[...] + modr_ref[5:6, :] * acc
        if final:
            x2 = _rms(x2) * fn_ref[...]
        o_ref[...] = x2

    @pl.when(i % 2 == 0)
    def _():
        step(hu_a, hu_b)

    @pl.when(i % 2 == 1)
    def _():
        step(hu_b, hu_a)


def _ffn(x2d, mod6, lw, final_norm, seq_len, rows_per_cond, final):
    n = x2d.shape[0]
    tm = FFN_TM
    nblk = n // tm
    row = _mod_row_map(tm, rows_per_cond)
    cur = lambda i: jnp.minimum(i, nblk - 1)
    done = lambda i: jnp.maximum(i - 1, 0)
    full = lambda shape: pl.BlockSpec(shape, lambda i: (0,) * len(shape), pipeline_mode=pl.Buffered(1))
    return pl.pallas_call(
        functools.partial(_ffn_kernel, seq_len=seq_len, final=final),
        grid=(nblk + 1,),
        in_specs=[pl.BlockSpec((tm, D_MODEL), lambda i: (cur(i), 0)),
                  pl.BlockSpec((tm, D_MODEL), lambda i: (done(i), 0)),
                  pl.BlockSpec((None, 6, D_MODEL), lambda i: (row(cur(i)), 0, 0)),
                  pl.BlockSpec((None, 6, D_MODEL), lambda i: (row(done(i)), 0, 0)),
                  full((1, D_MODEL)), full((D_MODEL, 2 * D_FF)), full((3, 2 * D_FF)), full((1, 2 * D_FF)),
                  full((D_FF, D_MODEL)), full((1, D_MODEL))],
        out_specs=pl.BlockSpec((tm, D_MODEL), lambda i: (done(i), 0)),
        out_shape=jax.ShapeDtypeStruct((n, D_MODEL), F32),
        scratch_shapes=[pltpu.VMEM((tm, 2 * D_FF), F32), pltpu.VMEM((tm, 2 * D_FF), F32)],
        compiler_params=_cparams(("arbitrary",)),
        name="ffn",
    )(x2d, x2d, mod6, mod6, lw['norm2'], lw['w_up'], lw['w_conv_ffn'], lw['b_conv_ffn'], lw['w_down'],
      final_norm)


def _run_pass(x, mod, layers, consts, init_states, final_norm):
    b, seq_len, _ = x.shape
    n = b * seq_len
    is_ctx = init_states is None
    rows_per_cond = None if is_ctx else seq_len
    step_rows = 4096
    x2d = x.reshape(n, D_MODEL)
    depth = len(layers)
    finals = []
    for li, lw in enumerate(layers):
        mod6 = mod[li]
        proj, ab, u4f = _in_projection(x2d, mod6, lw['norm1'], lw['w_in_main'], lw['w_in_ab'], rows_per_cond)
        pre = _gdn_precompute(proj, ab, lw['w_conv_qkv'], lw['gate_par'], consts['gdn_masks'], consts['gdn_tri'],
                              consts['gdn_levels'], seq_len)
        o_f, o_b, sd = _gdn_scan(pre, None if is_ctx else init_states[0][:, li], seq_len)
        yd, sfin = _s5_mixer(u4f, lw['s5_mats'], None if is_ctx else init_states[1][:, li], seq_len)
        o_c, sr = _retention(proj, consts['ret_tables'], consts.get('rope'),
                             None if is_ctx else init_states[2][:, li], seq_len, step_rows)
        x1 = _merge(x2d, o_f, o_b, yd, o_c, proj, mod6, lw, rows_per_cond)
        x2d = _ffn(x1, mod6, lw, final_norm, seq_len, rows_per_cond, final=(li == depth - 1))
        finals.append((sd, sfin, sr))
    return x2d.reshape(b, seq_len, D_MODEL), finals


def _s5_state_to_rows(s_re, s_im):
    shp = s_re.shape[:3]
    re = s_re.reshape(shp + (S5_BLOCKS, 4, LANE))
    im = s_im.reshape(shp + (S5_BLOCKS, 4, LANE))
    return jnp.concatenate([re, im], axis=-2)


def _s5_rows_to_state(fin):
    nb = fin.shape[2]
    re = fin[:, :, :, 0:4, :].reshape(2, S5_BLOCKS, nb, 8, P_B)
    im = fin[:, :, :, 4:8, :].reshape(2, S5_BLOCKS, nb, 8, P_B)
    perm = lambda t: jnp.transpose(t, (2, 0, 1, 3, 4)).reshape(nb, 2, G_B, P_B)
    return perm(re), perm(im)


def kernel(x_prompt, x_sample, state_delta, state_ssm_re, state_ssm_im, state_ret, c, c_ctx,
           final_norm, norm1, norm2, w_mod, b_mod, w_in, w_conv_qkv, a_log, dt_bias, norm_a, w_br_a,
           ssm_lam_re, ssm_lam_im, ssm_log_dt, ssm_b_re, ssm_b_im, ssm_c_re, ssm_c_im, ssm_d,
           w_glu, b_glu, w_br_b, w_br_c, w_o, w_up, w_conv_ffn, b_conv_ffn, w_down):
    depth = w_in.shape[0]
    dec_b = x_sample.shape[0]
    n_qkv = 3 * W_MIX
    layers = []
    for i in range(depth):
        wi = w_in[i]
        gates_w = wi[:, -3 * D_MODEL:]
        rest = wi[:, n_qkv + 16:-3 * D_MODEL]
        w_main = jnp.concatenate([gates_w, wi[:, :n_qkv], rest], axis=1).astype(BF16)
        w_ab = jnp.pad(wi[:, n_qkv:n_qkv + 16], ((0, 0), (0, LANE - 16))).astype(BF16)
        gate_par = jnp.zeros((8, LANE), F32)
        gate_par = gate_par.at[0, :8].set(a_log[i].reshape(8)).at[1, :8].set(dt_bias[i].reshape(8))
        layers.append(dict(
            norm1=norm1[i][None], norm2=norm2[i][None], w_in_main=w_main, w_in_ab=w_ab,
            w_conv_qkv=w_conv_qkv[i], gate_par=gate_par, norm_a=norm_a[i][None],
            s5_mats=_s5_matrices(ssm_lam_re[i], ssm_lam_im[i], ssm_log_dt[i], ssm_b_re[i], ssm_b_im[i],
                                 ssm_c_re[i], ssm_c_im[i]),
            ssm_d=ssm_d[i][None], w_glu=w_glu[i].astype(BF16), b_glu=b_glu[i][None],
            w_br_a=w_br_a[i].astype(BF16), w_br_b=w_br_b[i].astype(BF16), w_br_c=w_br_c[i].astype(BF16),
            w_o=w_o[i].astype(BF16), w_up=w_up[i].astype(BF16), w_conv_ffn=w_conv_ffn[i],
            b_conv_ffn=b_conv_ffn[i][None], w_down=w_down[i].astype(BF16)))
    expanded = _s5_expand(jnp.stack([lw['s5_mats'][0] for lw in layers]), _s5_expand_consts())
    for i, lw in enumerate(layers):
        _, a1, a2 = lw['s5_mats']
        lw['s5_mats'] = (expanded[i, 0], expanded[i, 1], expanded[i, 2], a1, a2)
    masks = _gdn_masks()
    consts = dict(gdn_masks=masks, gdn_tri=masks[M_INCL:M_INCL + 2].astype(BF16),
                  gdn_levels=masks[M_LEVEL + 1:M_LEVEL + 7].astype(BF16), ret_tables=_ret_tables())
    fnorm = final_norm[None]

    cond8 = jnp.concatenate([c_ctx[None], c, jnp.zeros((8 - 1 - dec_b, D_MODEL), F32)], axis=0)
    mod = _modulation(cond8, w_mod, b_mod).reshape(depth, 8, 6, D_MODEL)

    y_prompt, ctx_finals = _run_pass(x_prompt, mod, layers, consts, None, fnorm)
    consts_lat = dict(consts, rope=_rope_tables(x_sample.shape[1]))
    x0_rows = _s5_state_to_rows(state_ssm_re, state_ssm_im)
    y_sample, _ = _run_pass(x_sample, mod, layers, consts_lat, (state_delta, x0_rows, state_ret), fnorm)

    new_delta = jnp.stack([f[0] for f in ctx_finals], axis=1)
    s5 = [_s5_rows_to_state(f[1]) for f in ctx_finals]
    new_re = jnp.stack([s[0] for s in s5], axis=1)
    new_im = jnp.stack([s[1] for s in s5], axis=1)
    new_ret = jnp.stack([f[2] for f in ctx_finals], axis=1)
    return (y_prompt, y_sample, new_delta, new_re, new_im, new_ret)
```

```python
import functools
import math

import jax
import jax.numpy as jnp
import numpy as np
from jax import lax
from jax.experimental import pallas as pl
from jax.experimental.pallas import tpu as pltpu

F32 = jnp.float32
BF16 = jnp.bfloat16

D_MODEL = 1024
EPS = 1e-6
N_HEADS = 4
D_HEAD = 128
W_MIX = N_HEADS * D_HEAD
GDN_CHUNK = 64
SEG = 256
GRID_W = 64
ROPE_BASE = 10000.0
G_B, P_B, GC_B = 32, 64, 16
S5_TILE = 8
S5_BLOCKS = 4
S5_W = S5_TILE * 128
D_FF = 2816
LANE = 128
VMEM_LIMIT = 56 * 1024 * 1024

U_GATES = 0
U_QA, U_KA, U_VA, U_ZA = 24, 28, 32, 36
U_UB = 40
U_QC, U_KC, U_VC, U_GC = 44, 48, 52, 56
N_PROJ = 60 * LANE
PROJ_TN = 1280
UB_BLOCK = (U_UB * LANE) // PROJ_TN
UB_LOCAL = U_UB * LANE - UB_BLOCK * PROJ_TN
PROJ_DTYPE = BF16
HALO_ROWS = 16


def _cparams(sem):
    return pltpu.CompilerParams(dimension_semantics=sem, vmem_limit_bytes=VMEM_LIMIT)


def _dot(a, b):
    return jnp.dot(a.astype(BF16), b.astype(BF16), preferred_element_type=F32)


def _dot_nt(a, b):
    return lax.dot_general(a.astype(BF16), b.astype(BF16), (((1,), (1,)), ((), ())),
                           preferred_element_type=F32)


def _dot_tn(a, b):
    return lax.dot_general(a.astype(BF16), b.astype(BF16), (((0,), (0,)), ((), ())),
                           preferred_element_type=F32)


def _sigmoid(x):
    return jax.nn.sigmoid(x)


def _silu(x):
    return x * _sigmoid(x)


def _softplus(x):
    return jnp.maximum(x, 0.0) + jnp.log1p(jnp.exp(-jnp.abs(x)))


def _gelu_tanh(x):
    return 0.5 * x * (1.0 + jnp.tanh(math.sqrt(2.0 / math.pi) * (x + 0.044715 * x * x * x)))


def _rms(x):
    return x * lax.rsqrt(jnp.mean(x * x, axis=-1, keepdims=True) + EPS)


def _split3(x):
    hi = x.astype(BF16).astype(F32)
    r1 = x - hi
    mid = r1.astype(BF16).astype(F32)
    lo = (r1 - mid).astype(BF16).astype(F32)
    return hi, mid, lo


def _mod_kernel(c_ref, w_ref, b_ref, o_ref):
    o_ref[...] = _dot(_silu(c_ref[...]), w_ref[...]) + b_ref[...]


def _modulation(cond8, w_mod, b_mod):
    depth, _, n = w_mod.shape
    tn = 1536
    return pl.pallas_call(
        _mod_kernel,
        grid=(depth, n // tn),
        in_specs=[pl.BlockSpec((8, D_MODEL), lambda l, j: (0, 0)),
                  pl.BlockSpec((None, D_MODEL, tn), lambda l, j: (l, 0, j)),
                  pl.BlockSpec((None, 1, tn), lambda l, j: (l, 0, j))],
        out_specs=pl.BlockSpec((None, 8, tn), lambda l, j: (l, 0, j)),
        out_shape=jax.ShapeDtypeStruct((depth, 8, n), F32),
        compiler_params=_cparams(("parallel", "parallel")),
        name="modulation",
    )(cond8, w_mod, b_mod.reshape(depth, 1, n))


def _mod_row_map(tm, rows_per_cond):
    if rows_per_cond is None:
        return lambda i: 0
    return lambda i: 1 + (i * tm) // rows_per_cond


def _inproj_kernel(x_ref, mod_ref, nw_ref, w_ref, wab_ref, proj_ref, ab_ref, u4_ref, h_scr, u_scr):
    j = pl.program_id(1)
    rows8 = u4_ref.shape[1]

    @pl.when(j == 0)
    def _():
        h = _rms(x_ref[...]) * nw_ref[...]
        h = h * (1.0 + mod_ref[1:2, :]) + mod_ref[0:1, :]
        hb = h.astype(BF16)
        h_scr[...] = hb
        ab_ref[...] = jnp.dot(hb, wab_ref[...], preferred_element_type=F32)

    acc = jnp.dot(h_scr[...], w_ref[...], preferred_element_type=F32)
    proj_ref[...] = acc.astype(PROJ_DTYPE)

    @pl.when(j == UB_BLOCK)
    def _():
        for b in range(S5_BLOCKS):
            u_scr[b] = acc[:, UB_LOCAL + LANE * b:UB_LOCAL + LANE * (b + 1)]
        for b in range(S5_BLOCKS):
            for t in range(S5_TILE):
                u4_ref[b, :, LANE * t:LANE * (t + 1)] = u_scr[b, pl.ds(t, rows8, stride=S5_TILE), :]


def _in_projection(x2d, mod6, norm_w, w_main, w_ab, rows_per_cond):
    n = x2d.shape[0]
    tm, tn = 1024, PROJ_TN
    row = _mod_row_map(tm, rows_per_cond)
    return pl.pallas_call(
        _inproj_kernel,
        grid=(n // tm, N_PROJ // tn),
        in_specs=[pl.BlockSpec((tm, D_MODEL), lambda i, j: (i, 0)),
                  pl.BlockSpec((None, 6, D_MODEL), lambda i, j: (row(i), 0, 0)),
                  pl.BlockSpec((1, D_MODEL), lambda i, j: (0, 0)),
                  pl.BlockSpec((D_MODEL, tn), lambda i, j: (0, j)),
                  pl.BlockSpec((D_MODEL, LANE), lambda i, j: (0, 0))],
        out_specs=[pl.BlockSpec((tm, tn), lambda i, j: (i, j)),
                   pl.BlockSpec((tm, LANE), lambda i, j: (i, 0)),
                   pl.BlockSpec((S5_BLOCKS, tm // S5_TILE, S5_W), lambda i, j: (0, i, 0))],
        out_shape=[jax.ShapeDtypeStruct((n, N_PROJ), PROJ_DTYPE),
                   jax.ShapeDtypeStruct((n, LANE), F32),
                   jax.ShapeDtypeStruct((S5_BLOCKS, n // S5_TILE, S5_W), F32)],
        scratch_shapes=[pltpu.VMEM((tm, D_MODEL), BF16), pltpu.VMEM((S5_BLOCKS, tm, LANE), F32)],
        compiler_params=_cparams(("parallel", "arbitrary")),
        name="in_projection",
    )(x2d, mod6, norm_w, w_main, w_ab)


M_INCL, M_STRICT, M_EYE, M_LEVEL = 0, 2, 4, 4
N_MASKS = 11


def _gdn_masks():
    r = jnp.arange(SEG)[:, None]
    c = jnp.arange(SEG)[None, :]
    same = (r // GDN_CHUNK) == (c // GDN_CHUNK)
    incl = [same & (r >= c), same & (r <= c)]
    strict = [same & (r > c), same & (r < c)]
    eye = [r == c]
    levels = [((r >> k) == (c >> k)) & ((r >> (k - 1)) != (c >> (k - 1))) for k in range(1, 7)]
    return jnp.stack(incl + strict + eye + levels).astype(F32)


def _gdn_pre_kernel(*refs, segs_per_seq):
    has_halo = segs_per_seq > 1
    q_ref, k_ref, v_ref = refs[:3]
    pos = 3
    halos = None
    if has_halo:
        halos = refs[3:9]
        pos = 9
    (ab_ref, wq_ref, wk_ref, wv_ref, gp_ref, msk_ref, tri_ref, lvl_ref,
     u_ref, w_ref, qg_ref, kg_ref, qkm_ref, eg_ref) = refs[pos:]
    seg = pl.program_id(0)
    rowi = lax.broadcasted_iota(jnp.int32, (SEG, W_MIX), 0)
    lane = lax.broadcasted_iota(jnp.int32, (SEG, LANE), 1)

    def conv_act(x_ref, cw_ref, prev_ref, next_ref):
        x = x_ref[...].astype(F32)
        xm = pltpu.roll(x, 1, 0)
        xp = pltpu.roll(x, SEG - 1, 0)
        if has_halo:
            p = seg % segs_per_seq
            prev = jnp.where(p > 0, prev_ref[HALO_ROWS - 1:HALO_ROWS, :].astype(F32), 0.0)
            nxt = jnp.where(p < segs_per_seq - 1, next_ref[0:1, :].astype(F32), 0.0)
        else:
            prev = jnp.zeros((1, W_MIX), F32)
            nxt = prev
        xm = jnp.where(rowi == 0, prev, xm)
        xp = jnp.where(rowi == SEG - 1, nxt, xp)
        return _silu(cw_ref[0:1, :] * xm + cw_ref[1:2, :] * x + cw_ref[2:3, :] * xp)

    hq = halos[0:2] if has_halo else (None, None)
    hk = halos[2:4] if has_halo else (None, None)
    hv = halos[4:6] if has_halo else (None, None)
    qa_all = conv_act(q_ref, wq_ref, *hq)
    ka_all = conv_act(k_ref, wk_ref, *hk)
    va_all = conv_act(v_ref, wv_ref, *hv)

    abv = ab_ref[...]
    log_a = -jnp.exp(gp_ref[0:1, :]) * _softplus(abv + gp_ref[1:2, :])
    beta_all = _sigmoid(abv)

    def col(x, idx):
        return jnp.sum(jnp.where(lane == idx, x, 0.0), axis=1, keepdims=True)

    heads = []
    for h in range(N_HEADS):
        hs = slice(h * D_HEAD, (h + 1) * D_HEAD)
        qa, ka = qa_all[:, hs], ka_all[:, hs]
        qn = qa * lax.rsqrt(jnp.sum(qa * qa, axis=-1, keepdims=True) + EPS) * (D_HEAD ** -0.5)
        kn = ka * lax.rsqrt(jnp.sum(ka * ka, axis=-1, keepdims=True) + EPS)
        heads.append(dict(qn=qn, kn=kn, va=va_all[:, hs], kk=_dot_nt(kn, kn), qk=_dot_nt(qn, kn)))

    chains = []
    for h in range(N_HEADS):
        for d in range(2):
            hd = heads[h]
            g = jnp.broadcast_to(col(log_a, d * N_HEADS + h), (SEG, LANE))
            beta = col(beta_all, 2 * N_HEADS + d * N_HEADS + h)
            ghi, gmid, glo = _split3(g)
            pieces = jnp.where(lane == 0, ghi, jnp.where(lane == 1, gmid, jnp.where(lane == 2, glo, 0.0)))
            cs = jnp.dot(tri_ref[d], pieces.astype(BF16), preferred_element_type=F32)
            gc = jnp.broadcast_to(jnp.sum(jnp.where(lane < 3, cs, 0.0), axis=1, keepdims=True), (SEG, LANE))
            chi, cmid, clo = _split3(gc)
            a_mat = jnp.where(lane == 0, chi, jnp.where(lane == 1, cmid, jnp.where(lane == 2, clo,
                              jnp.where(lane < 6, 1.0, 0.0))))
            b_mat = jnp.where(lane < 3, 1.0, jnp.where(lane == 3, -chi, jnp.where(lane == 4, -cmid,
                              jnp.where(lane == 5, -clo, 0.0))))
            e = jnp.exp(jnp.minimum(_dot_nt(a_mat, b_mat), 0.0))
            lmat = (beta * hd['kk']) * (e * msk_ref[M_STRICT + d])
            chains.append(dict(h=h, d=d, beta=beta, gc=gc, lmat=lmat.astype(BF16),
                               qkm=hd['qk'] * (e * msk_ref[M_INCL + d]),
                               x=(msk_ref[M_EYE] - lmat * msk_ref[M_LEVEL + 1]).astype(BF16)))

    for lvl in range(2, 7):
        for ch in chains:
            c = ch['lmat'] * lvl_ref[lvl - 1]
            t = jnp.dot(c, ch['x'], preferred_element_type=F32).astype(BF16)
            ch['x'] = ch['x'] - jnp.dot(ch['x'], t, preferred_element_type=F32).astype(BF16)

    for ch in chains:
        h, d, beta, gc = ch['h'], ch['d'], ch['beta'], ch['gc']
        hd = heads[h]
        qn, kn, va = hd['qn'], hd['kn'], hd['va']
        hs = slice(h * D_HEAD, (h + 1) * D_HEAD)
        egc = jnp.exp(gc)
        rhs = jnp.concatenate([va * beta, kn * (beta * egc)], axis=1).astype(BF16)
        uw = jnp.dot(ch['x'], rhs, preferred_element_type=F32)
        u_ref[d, :, hs] = uw[:, :D_HEAD]
        w_ref[d, :, hs] = uw[:, D_HEAD:].astype(BF16)
        qg_ref[d, :, hs] = (qn * egc).astype(BF16)
        edge = GDN_CHUNK - 1 if d == 0 else 0
        for c4 in range(SEG // GDN_CHUNK):
            lo_r = c4 * GDN_CHUNK
            rs = slice(lo_r, lo_r + GDN_CHUNK)
            g_edge = gc[lo_r + edge:lo_r + edge + 1, :]
            kg_ref[d, rs, hs] = (kn[rs, :] * jnp.exp(g_edge - gc[rs, :])).astype(BF16)
            eg_ref[d, c4, h:h + 1, :] = jnp.exp(g_edge)
            qkm_ref[d, h, rs, :] = ch['qkm'][rs, rs].astype(BF16)


def _gdn_precompute(proj, ab, wconv, gate_par, masks, tri, levels, seq_len):
    n = proj.shape[0]
    segs_per_seq = seq_len // SEG
    nseg = n // SEG
    units = (U_QA // N_HEADS, U_KA // N_HEADS, U_VA // N_HEADS)
    in_specs = [pl.BlockSpec((SEG, W_MIX), lambda s, cb=cb: (s, cb)) for cb in units]
    args = [proj, proj, proj]
    if segs_per_seq > 1:
        r8 = SEG // HALO_ROWS
        last8 = n // HALO_ROWS - 1
        for cb in units:
            in_specs.append(pl.BlockSpec((HALO_ROWS, W_MIX), lambda s, cb=cb: (jnp.maximum(s * r8 - 1, 0), cb)))
            in_specs.append(pl.BlockSpec((HALO_ROWS, W_MIX),
                                         lambda s, cb=cb: (jnp.minimum((s + 1) * r8, last8), cb)))
            args += [proj, proj]
    in_specs += [pl.BlockSpec((SEG, LANE), lambda s: (s, 0))]
    in_specs += [pl.BlockSpec((3, W_MIX), lambda s, o=o: (0, o)) for o in range(3)]
    in_specs += [pl.BlockSpec((8, LANE), lambda s: (0, 0)),
                 pl.BlockSpec((N_MASKS, SEG, SEG), lambda s: (0, 0, 0)),
                 pl.BlockSpec((2, SEG, SEG), lambda s: (0, 0, 0)),
                 pl.BlockSpec((6, SEG, SEG), lambda s: (0, 0, 0))]
    args += [ab, wconv, wconv, wconv, gate_par, masks, tri, levels]
    big = pl.BlockSpec((2, SEG, W_MIX), lambda s: (0, s, 0))
    cpseg = SEG // GDN_CHUNK
    return pl.pallas_call(
        functools.partial(_gdn_pre_kernel, segs_per_seq=segs_per_seq),
        grid=(nseg,),
        in_specs=in_specs,
        out_specs=[big, big, big, big,
                   pl.BlockSpec((2, N_HEADS, SEG, GDN_CHUNK), lambda s: (0, 0, s, 0)),
                   pl.BlockSpec((2, cpseg, N_HEADS, LANE), lambda s: (0, s, 0, 0))],
        out_shape=[jax.ShapeDtypeStruct((2, n, W_MIX), F32),
                   jax.ShapeDtypeStruct((2, n, W_MIX), BF16),
                   jax.ShapeDtypeStruct((2, n, W_MIX), BF16),
                   jax.ShapeDtypeStruct((2, n, W_MIX), BF16),
                   jax.ShapeDtypeStruct((2, N_HEADS, n, GDN_CHUNK), BF16),
                   jax.ShapeDtypeStruct((2, n // GDN_CHUNK, N_HEADS, LANE), F32)],
        compiler_params=_cparams(("parallel",)),
        name="gdn_precompute",
    )(*args)


GDN_SCAN_ROWS = 1024


def _gdn_scan_kernel(*refs, chunks_per_seq, has_init):
    ins = refs[:12]
    dir_refs = [ins[0::2], ins[1::2]]
    if has_init:
        s0_ref, of_ref, ob_ref, s_scr = refs[12:]
        fin_refs = None
    else:
        of_ref, ob_ref, sff_ref, sfb_ref, s_scr = refs[12:]
        fin_refs = (sff_ref, sfb_ref)
    o_refs = (of_ref, ob_ref)
    nchunk = of_ref.shape[0] // GDN_CHUNK
    cps = chunks_per_seq

    @pl.when(pl.program_id(1) == 0)
    def _():
        if has_init:
            s_scr[...] = s0_ref[...]
        else:
            s_scr[...] = jnp.zeros_like(s_scr)

    def body(i, carry):
        chains = []
        for h in range(N_HEADS):
            for d in range(2):
                c = i if d == 0 else nchunk - 1 - i
                chains.append(dict(h=h, d=d, c=c, hs=slice(h * D_HEAD, (h + 1) * D_HEAD),
                                   rsl=pl.ds(pl.multiple_of(c * GDN_CHUNK, GDN_CHUNK), GDN_CHUNK)))
        for ch in chains:
            s = s_scr[ch['d'], ch['h']]
            if not has_init:
                s = jnp.where(i % cps == 0, 0.0, s)
            ch['s'] = s
            ch['sb'] = s.astype(BF16)
        for ch in chains:
            u_ref, w_ref, qg_ref = dir_refs[ch['d']][:3]
            rsl, hs = ch['rsl'], ch['hs']
            ch['vb'] = (u_ref[rsl, hs] - jnp.dot(w_ref[rsl, hs], ch['sb'], preferred_element_type=F32)).astype(BF16)
            ch['o'] = jnp.dot(qg_ref[rsl, hs], ch['sb'], preferred_element_type=F32)
        for ch in chains:
            kg_ref, qkm_ref, eg_ref = dir_refs[ch['d']][3:]
            rsl, hs, h = ch['rsl'], ch['hs'], ch['h']
            ch['o'] = ch['o'] + jnp.dot(qkm_ref[h, rsl, :], ch['vb'], preferred_element_type=F32)
            ch['s_new'] = ch['s'] * eg_ref[ch['c']][h:h + 1, :] + _dot_tn(kg_ref[rsl, hs], ch['vb'])
        for ch in chains:
            s_scr[ch['d'], ch['h']] = ch['s_new']
            o_refs[ch['d']][ch['rsl'], ch['hs']] = ch['o']
        if not has_init:
            @pl.when(i % cps == cps - 1)
            def _():
                for ch in chains:
                    fin_refs[ch['d']][ch['c'] // cps, ch['h']] = ch['s_new']
        return carry

    lax.fori_loop(0, nchunk, body, 0)


def _gdn_scan(pre, s0, seq_len):
    n = pre[0].shape[1]
    br = GDN_SCAN_ROWS
    has_init = s0 is not None
    group_rows = seq_len if has_init else n
    ngroups, nblk = n // group_rows, group_rows // br
    fwd = lambda g, j: g * nblk + j
    bwd = lambda g, j: g * nblk + (nblk - 1 - j)
    in_specs, args = [], []
    for arr, kind in zip(pre, ("row", "row", "row", "row", "qkm", "eg")):
        for d, pos in ((0, fwd), (1, bwd)):
            if kind == "row":
                in_specs.append(pl.BlockSpec((None, br, W_MIX), lambda g, j, d=d, pos=pos: (d, pos(g, j), 0)))
            elif kind == "qkm":
                in_specs.append(pl.BlockSpec((None, N_HEADS, br, GDN_CHUNK),
                                             lambda g, j, d=d, pos=pos: (d, 0, pos(g, j), 0)))
            else:
                in_specs.append(pl.BlockSpec((None, br // GDN_CHUNK, N_HEADS, LANE),
                                             lambda g, j, d=d, pos=pos: (d, pos(g, j), 0, 0)))
            args.append(arr)
    out_specs = [pl.BlockSpec((br, W_MIX), lambda g, j: (fwd(g, j), 0)),
                 pl.BlockSpec((br, W_MIX), lambda g, j: (bwd(g, j), 0))]
    out_shape = [jax.ShapeDtypeStruct((n, W_MIX), F32), jax.ShapeDtypeStruct((n, W_MIX), F32)]
    if has_init:
        in_specs.append(pl.BlockSpec((None, 2, N_HEADS, D_HEAD, D_HEAD), lambda g, j: (g, 0, 0, 0, 0)))
        args.append(s0)
    else:
        spb = br // seq_len
        fin_shape = jax.ShapeDtypeStruct((n // seq_len, N_HEADS, D_HEAD, D_HEAD), F32)
        out_specs += [pl.BlockSpec((spb, N_HEADS, D_HEAD, D_HEAD), lambda g, j: (fwd(g, j), 0, 0, 0)),
                      pl.BlockSpec((spb, N_HEADS, D_HEAD, D_HEAD), lambda g, j: (bwd(g, j), 0, 0, 0))]
        out_shape += [fin_shape, fin_shape]
    res = pl.pallas_call(
        functools.partial(_gdn_scan_kernel, chunks_per_seq=seq_len // GDN_CHUNK, has_init=has_init),
        grid=(ngroups, nblk),
        in_specs=in_specs, out_specs=out_specs, out_shape=out_shape,
        scratch_shapes=[pltpu.VMEM((2, N_HEADS, D_HEAD, D_HEAD), F32)],
        compiler_params=_cparams(("parallel", "arbitrary")),
        name="gdn_scan",
    )(*args)
    fin = None if has_init else jnp.stack([res[2], res[3]], axis=1)
    return res[0], res[1], fin


def _s5_matrices(lam_re, lam_im, log_dt, b_re, b_im, c_re, c_im):
    t = S5_TILE
    step = jnp.exp(log_dt)[:, :, None]
    lr, li = lam_re * step, lam_im * step
    js = jnp.arange(t + 1, dtype=F32)[:, None, None, None]
    mag = jnp.exp(lr[None] * js)
    pw_re, pw_im = mag * jnp.cos(li[None] * js), mag * jnp.sin(li[None] * js)
    a_re, a_im = pw_re[1], pw_im[1]
    den = lam_re * lam_re + lam_im * lam_im
    z_re = ((a_re - 1.0) * lam_re + a_im * lam_im) / den
    z_im = (a_im * lam_re - (a_re - 1.0) * lam_im) / den
    zb_re = z_re[..., None] * b_re[None] - z_im[..., None] * b_im[None]
    zb_im = z_re[..., None] * b_im[None] + z_im[..., None] * b_re[None]
    w_re = pw_re[..., None] * zb_re[None] - pw_im[..., None] * zb_im[None]
    w_im = pw_re[..., None] * zb_im[None] + pw_im[..., None] * zb_re[None]
    taps = (jnp.einsum('gcp,jdgpi->jdgic', c_re, w_re[:t]) - jnp.einsum('gcp,jdgpi->jdgic', c_im, w_im[:t]))
    nb = S5_BLOCKS
    hi = lax.Precision.HIGHEST

    def blocked(x, axis):
        return x.reshape(x.shape[:axis] + (nb, 8) + x.shape[axis + 1:])

    ss, tt, jj = np.meshgrid(np.arange(t), np.arange(t), np.arange(t), indexing='ij')
    sel_tap = np.stack([tt - ss == jj, ss - tt == jj]).astype(np.float32)
    s2, e2 = np.meshgrid(np.arange(t), np.arange(t + 1), indexing='ij')
    sel_in = np.stack([e2 == t - 1 - s2, e2 == s2]).astype(np.float32)
    sel_out = np.stack([e2 == s2 + 1, e2 == t - s2]).astype(np.float32)

    kst = blocked(jnp.einsum('dstj,jdgic->stgic', sel_tap, taps, precision=hi), 2)
    taps_sum = jnp.transpose(kst, (2, 0, 3, 4, 1, 5)).reshape(nb, S5_W, LANE)
    ws = jnp.stack([jnp.einsum('dse,edgpc->dsgpc', sel_in, w_re, precision=hi),
                    jnp.einsum('dse,edgpc->dsgpc', sel_in, w_im, precision=hi)], axis=3)
    gm = jnp.transpose(blocked(ws, 2), (0, 2, 1, 3, 6, 4, 5)).reshape(2, nb, S5_W, LANE)
    pr = jnp.einsum('dte,edgp->dtgp', sel_out, pw_re, precision=hi)
    pi = jnp.einsum('dte,edgp->dtgp', sel_out, pw_im, precision=hi)
    wo_re = c_re * pr[:, :, :, None, :] - c_im * pi[:, :, :, None, :]
    wo_im = c_re * pi[:, :, :, None, :] + c_im * pr[:, :, :, None, :]
    wo = blocked(jnp.stack([wo_re, -wo_im], axis=1), 3)
    cm = jnp.transpose(wo, (0, 3, 1, 4, 6, 2, 5)).reshape(2, nb, S5_W, LANE)
    a8r = pw_re[t].reshape(2, nb, 4, LANE)
    a8i = pw_im[t].reshape(2, nb, 4, LANE)
    a1 = jnp.concatenate([a8r, a8r], axis=2)
    a2 = jnp.concatenate([-a8i, a8i], axis=2)
    taps_both = jnp.stack([taps_sum, jnp.zeros_like(taps_sum)])
    return jnp.stack([taps_both, gm, cm]), a1, a2


def _s5_expand_consts():
    col = jnp.arange(S5_W)
    src = jnp.arange(LANE)
    ex_tc = (src[:, None] == ((col // LANE) * GC_B + col % GC_B)[None, :])
    ex_rp = (src[:, None] == ((col // (S5_W // 2)) * P_B + col % P_B)[None, :])
    g_tc = (col // GC_B) % 8
    g_rp = (col // P_B) % 8
    masks = [g_tc[:, None] == g_tc[None, :], g_tc[:, None] == g_rp[None, :], g_rp[:, None] == g_tc[None, :]]
    return (jnp.stack([ex_tc, ex_rp, ex_tc]).astype(BF16), jnp.stack(masks).astype(BF16))


def _s5_expand_kernel(c_ref, ex_ref, m_ref, o_ref):
    full = jnp.dot(c_ref[...].astype(BF16), ex_ref[...], preferred_element_type=F32)
    o_ref[...] = (full * m_ref[...].astype(F32)).astype(BF16)


def _s5_expand(compact, consts):
    ex, masks = consts
    nl, _, _, nb = compact.shape[:4]
    return pl.pallas_call(
        _s5_expand_kernel,
        grid=(3, nl, 2, nb),
        in_specs=[pl.BlockSpec((None, None, None, None, S5_W, LANE), lambda k, l, d, b: (l, k, d, b, 0, 0)),
                  pl.BlockSpec((None, LANE, S5_W), lambda k, l, d, b: (k, 0, 0)),
                  pl.BlockSpec((None, S5_W, S5_W), lambda k, l, d, b: (k, 0, 0))],
        out_specs=pl.BlockSpec((None, None, None, None, S5_W, S5_W), lambda k, l, d, b: (l, k, d, b, 0, 0)),
        out_shape=jax.ShapeDtypeStruct(compact.shape[:4] + (S5_W, S5_W), BF16),
        compiler_params=_cparams(("parallel",) * 4),
        name="s5_expand",
    )(compact, ex, masks)


def _s5_in_kernel(u_ref, gm_ref, g_ref):
    rows = u_ref.shape[0]
    g = _dot(u_ref[...], gm_ref[...])
    for s in range(8):
        g_ref[pl.ds(s, rows, stride=8), :] = g[:, LANE * s:LANE * (s + 1)]


S5_KIND_TAP, S5_KIND_IN, S5_KIND_OUT = 0, 1, 2


def _s5_tile_inputs(u4f, mats, layer):
    nb, rows, _ = u4f.shape
    return pl.pallas_call(
        _s5_in_kernel,
        grid=(nb, 2),
        in_specs=[pl.BlockSpec((None, rows, S5_W), lambda b, d: (b, 0, 0)),
                  pl.BlockSpec((None, None, None, None, S5_W, S5_W),
                               lambda b, d: (layer, S5_KIND_IN, d, b, 0, 0))],
        out_specs=pl.BlockSpec((None, None, rows * 8, LANE), lambda b, d: (d, b, 0, 0)),
        out_shape=jax.ShapeDtypeStruct((2, nb, rows * 8, LANE), F32),
        compiler_params=_cparams(("parallel", "parallel")),
        name="s5_tile_inputs",
    )(u4f, mats)


def _s5_scan_kernel(*refs, nseq, rows_per_seq, has_init):
    g_ref, a1_ref, a2_ref = refs[:3]
    if has_init:
        x0_ref, xp_ref = refs[3:]
        fin_ref = None
    else:
        xp_ref, fin_ref = refs[3:]
    nblk = g_ref.shape[0]
    backward = pl.program_id(0) == 1
    chains = [(b, s) for b in range(nblk) for s in range(nseq)]

    def body(i, xs):
        n = jnp.where(backward, rows_per_seq - 1 - i, i)
        out = []
        for (b, s), x in zip(chains, xs):
            r = s * rows_per_seq + n
            xp_ref[b, r] = x
            out.append(a1_ref[b] * x + a2_ref[b] * pltpu.roll(x, 4, 0) + g_ref[b, r])
        return tuple(out)

    if has_init:
        init = tuple(x0_ref[b] for (b, s) in chains)
    else:
        init = tuple(jnp.zeros((8, LANE), F32) for _ in chains)
    xs = lax.fori_loop(0, rows_per_seq, body, init)
    if not has_init:
        for (b, s), x in zip(chains, xs):
            fin_ref[s] = x


def _s5_scan(g5, a1, a2, x0, seq_rows):
    _, nb, rows, _, _ = g5.shape
    nseq = rows // seq_rows
    if x0 is None:
        return pl.pallas_call(
            functools.partial(_s5_scan_kernel, nseq=nseq, rows_per_seq=seq_rows, has_init=False),
            grid=(2, nb),
            in_specs=[pl.BlockSpec((None, 1, rows, 8, LANE), lambda d, b: (d, b, 0, 0, 0)),
                      pl.BlockSpec((None, 1, 8, LANE), lambda d, b: (d, b, 0, 0)),
                      pl.BlockSpec((None, 1, 8, LANE), lambda d, b: (d, b, 0, 0))],
            out_specs=[pl.BlockSpec((None, 1, rows, 8, LANE), lambda d, b: (d, b, 0, 0, 0)),
                       pl.BlockSpec((None, None, nseq, 8, LANE), lambda d, b: (d, b, 0, 0, 0))],
            out_shape=[jax.ShapeDtypeStruct(g5.shape, F32),
                       jax.ShapeDtypeStruct((2, nb, nseq, 8, LANE), F32)],
            compiler_params=_cparams(("parallel", "parallel")),
            name="s5_scan_ctx",
        )(g5, a1, a2)
    xprev = pl.pallas_call(
        functools.partial(_s5_scan_kernel, nseq=1, rows_per_seq=seq_rows, has_init=True),
        grid=(2, nseq),
        in_specs=[pl.BlockSpec((None, nb, seq_rows, 8, LANE), lambda d, s: (d, 0, s, 0, 0)),
                  pl.BlockSpec((None, nb, 8, LANE), lambda d, s: (d, 0, 0, 0)),
                  pl.BlockSpec((None, nb, 8, LANE), lambda d, s: (d, 0, 0, 0)),
                  pl.BlockSpec((None, None, nb, 8, LANE), lambda d, s: (s, d, 0, 0, 0))],
        out_specs=pl.BlockSpec((None, nb, seq_rows, 8, LANE), lambda d, s: (d, 0, s, 0, 0)),
        out_shape=jax.ShapeDtypeStruct(g5.shape, F32),
        compiler_params=_cparams(("parallel", "parallel")),
        name="s5_scan_latent",
    )(g5, a1, a2, x0)
    return xprev, None


def _s5_out_kernel(u_ref, xp_ref, tm_ref, cm_ref, y_ref):
    tr = u_ref.shape[0]
    acc = jnp.dot(u_ref[...].astype(BF16), tm_ref[...], preferred_element_type=F32)
    for d in range(2):
        xp = jnp.concatenate([xp_ref[d, pl.ds(s, tr, stride=8), :] for s in range(8)], axis=1)
        acc += _dot(xp, cm_ref[d])
    for t in range(S5_TILE):
        y_ref[pl.ds(t, tr, stride=S5_TILE), :] = acc[:, LANE * t:LANE * (t + 1)]


def _s5_outputs(u4f, xprev, mats, layer):
    nb, rows, _ = u4f.shape
    tr = 512
    return pl.pallas_call(
        _s5_out_kernel,
        grid=(nb, rows // tr),
        in_specs=[pl.BlockSpec((None, tr, S5_W), lambda b, r: (b, r, 0)),
                  pl.BlockSpec((2, None, tr * 8, LANE), lambda b, r: (0, b, r, 0)),
                  pl.BlockSpec((None, None, None, None, S5_W, S5_W),
                               lambda b, r: (layer, S5_KIND_TAP, 0, b, 0, 0)),
                  pl.BlockSpec((None, None, 2, None, S5_W, S5_W),
                               lambda b, r: (layer, S5_KIND_OUT, 0, b, 0, 0))],
        out_specs=pl.BlockSpec((tr * S5_TILE, LANE), lambda b, r: (r, b)),
        out_shape=jax.ShapeDtypeStruct((rows * S5_TILE, S5_BLOCKS * LANE), F32),
        compiler_params=_cparams(("parallel", "parallel")),
        name="s5_outputs",
    )(u4f, xprev, mats, mats)


def _s5_mixer(u4f, s5, x0, seq_len):
    mats, layer, a1, a2 = s5
    nb, rows, _ = u4f.shape
    g = _s5_tile_inputs(u4f, mats, layer)
    xprev, fin = _s5_scan(g.reshape(2, nb, rows, 8, LANE), a1, a2, x0, seq_len // S5_TILE)
    yd = _s5_outputs(u4f, xprev.reshape(2, nb, rows * 8, LANE), mats, layer)
    return yd, fin


def _ret_tables():
    h = jnp.arange(N_HEADS, dtype=F32)
    lg = jnp.log1p(-jnp.exp2(-5.0 - h))
    lgf, lgb = lg[:, None], lg[::-1][:, None]
    i = jnp.arange(SEG, dtype=F32)
    dist = i[:, None] - i[None, :]
    d_f = jnp.where(dist >= 0, jnp.exp(lgf[:, :, None] * jnp.maximum(dist, 0.0)), 0.0)
    d_b = jnp.where(dist <= 0, jnp.exp(lgb[:, :, None] * jnp.maximum(-dist, 0.0)), 0.0)
    dsum = d_f + d_b
    dec = jnp.stack([jnp.exp(lgf * (i + 1.0)), jnp.exp(lgf * (SEG - 1.0 - i)),
                     jnp.exp(lgb * (SEG - i)), jnp.exp(lgb * i)], axis=1)
    dec = jnp.broadcast_to(dec[..., None], (N_HEADS, 4, SEG, LANE))
    cd = jnp.stack([jnp.exp(lgf[:, 0] * SEG), jnp.exp(lgb[:, 0] * SEG)], axis=1)
    cd = jnp.broadcast_to(jnp.pad(cd, ((0, 0), (0, 6)))[..., None], (N_HEADS, 8, LANE))
    return dsum, dec, cd


def _rope_tables(n_tokens):
    n_rows = n_tokens // GRID_W
    rows = jnp.repeat(jnp.arange(n_rows), GRID_W).astype(F32)
    cols = jnp.tile(jnp.arange(GRID_W), n_rows).astype(F32)
    pairs = D_HEAD // 4
    freqs = ROPE_BASE ** (-jnp.arange(pairs, dtype=F32) / pairs)
    ang = jnp.concatenate([rows[:, None] * freqs, cols[:, None] * freqs], axis=-1)
    cos, sin = jnp.cos(ang), jnp.sin(ang)
    return jnp.concatenate([cos, cos], axis=-1), jnp.concatenate([-sin, sin], axis=-1)


def _ret_kernel(*refs, is_ctx):
    q_ref, k_ref, v_ref, g_ref, dsum_ref, dec_ref, cd_ref = refs[:7]
    if is_ctx:
        o_ref, sfin_ref = refs[7:]
    else:
        c2_ref, s2_ref, s0_ref, o_ref, sbin_scr = refs[7:]
    rows = q_ref.shape[0]
    nchunk = rows // SEG
    scale = D_HEAD ** -0.5

    def chunk_rows(c):
        return pl.ds(pl.multiple_of(c * SEG, SEG), SEG)

    heads = range(RET_GROUP)
    hsl = [slice(j * D_HEAD, (j + 1) * D_HEAD) for j in heads]

    def load_qk(rsl, j):
        q, k = q_ref[rsl, hsl[j]].astype(F32), k_ref[rsl, hsl[j]].astype(F32)
        if not is_ctx:
            c2, s2 = c2_ref[rsl, :], s2_ref[rsl, :]
            q = q * c2 + pltpu.roll(q, D_HEAD // 2, 1) * s2
            k = k * c2 + pltpu.roll(k, D_HEAD // 2, 1) * s2
        return q, k * scale

    def finish(rsl, j, r):
        o_ref[rsl, hsl[j]] = _rms(r) * _silu(g_ref[rsl, hsl[j]].astype(F32))

    if is_ctx:
        def body(c, carry):
            rsl = chunk_rows(c)
            qk = [load_qk(rsl, j) for j in heads]
            vs = [v_ref[rsl, hsl[j]] for j in heads]
            att = [_dot_nt(qk[j][0], qk[j][1]) * dsum_ref[j] for j in heads]
            inner = [_dot(att[j], vs[j]) for j in heads]
            for j in heads:
                finish(rsl, j, inner[j])
                sfin_ref[c, 0, j] = _dot_tn(qk[j][1] * dec_ref[j, 1], vs[j])
                sfin_ref[c, 1, j] = _dot_tn(qk[j][1] * dec_ref[j, 3], vs[j])
            return carry
        lax.fori_loop(0, nchunk, body, 0, unroll=2)
    else:
        def back(i, s_b):
            c = nchunk - 1 - i
            rsl = chunk_rows(c)
            ks = [load_qk(rsl, j)[1] for j in heads]
            sbin_scr[c] = jnp.stack(s_b)
            return tuple(s_b[j] * cd_ref[j, 1:2, :] + _dot_tn(ks[j] * dec_ref[j, 3], v_ref[rsl, hsl[j]])
                         for j in heads)
        lax.fori_loop(0, nchunk, back, tuple(s0_ref[1, j] for j in heads), unroll=2)

        def fwd(c, s_f):
            rsl = chunk_rows(c)
            qk = [load_qk(rsl, j) for j in heads]
            vs = [v_ref[rsl, hsl[j]] for j in heads]
            att = [_dot_nt(qk[j][0], qk[j][1]) * dsum_ref[j] for j in heads]
            cross = [_dot(qk[j][0] * dec_ref[j, 0], s_f[j]) + _dot(qk[j][0] * dec_ref[j, 2], sbin_scr[c, j])
                     for j in heads]
            inner = [_dot(att[j], vs[j]) for j in heads]
            for j in heads:
                finish(rsl, j, inner[j] + cross[j])
            return tuple(s_f[j] * cd_ref[j, 0:1, :] + _dot_tn(qk[j][1] * dec_ref[j, 1], vs[j]) for j in heads)
        lax.fori_loop(0, nchunk, fwd, tuple(s0_ref[0, j] for j in heads), unroll=2)


RET_GROUP = 2


def _retention(proj, tables, rope, s0, seq_len, step_rows):
    dsum, dec, cd = tables
    n = proj.shape[0]
    is_ctx = s0 is None
    gw = RET_GROUP * D_HEAD
    blk = lambda unit: pl.BlockSpec((step_rows, gw), lambda s, h, unit=unit: (s, unit // RET_GROUP + h))
    in_specs = [blk(U_QC), blk(U_KC), blk(U_VC), blk(U_GC),
                pl.BlockSpec((RET_GROUP, SEG, SEG), lambda s, h: (h, 0, 0)),
                pl.BlockSpec((RET_GROUP, 4, SEG, LANE), lambda s, h: (h, 0, 0, 0)),
                pl.BlockSpec((RET_GROUP, 8, LANE), lambda s, h: (h, 0, 0))]
    args = [proj, proj, proj, proj, dsum, dec, cd]
    out_specs = [pl.BlockSpec((step_rows, gw), lambda s, h: (s, h))]
    out_shape = [jax.ShapeDtypeStruct((n, W_MIX), F32)]
    scratch = []
    if is_ctx:
        assert seq_len == SEG
        out_specs.append(pl.BlockSpec((step_rows // SEG, 2, RET_GROUP, D_HEAD, D_HEAD),
                                      lambda s, h: (s, 0, h, 0, 0)))
        out_shape.append(jax.ShapeDtypeStruct((n // SEG, 2, N_HEADS, D_HEAD, D_HEAD), F32))
    else:
        assert step_rows == seq_len
        in_specs += [pl.BlockSpec((step_rows, LANE), lambda s, h: (0, 0)),
                     pl.BlockSpec((step_rows, LANE), lambda s, h: (0, 0)),
                     pl.BlockSpec((None, 2, RET_GROUP, D_HEAD, D_HEAD), lambda s, h: (s, 0, h, 0, 0))]
        args += [rope[0], rope[1], s0]
        scratch = [pltpu.VMEM((step_rows // SEG, RET_GROUP, D_HEAD, D_HEAD), F32)]
    res = pl.pallas_call(
        functools.partial(_ret_kernel, is_ctx=is_ctx),
        grid=(n // step_rows, N_HEADS // RET_GROUP),
        in_specs=in_specs, out_specs=out_specs, out_shape=out_shape, scratch_shapes=scratch,
        compiler_params=_cparams(("parallel", "parallel")),
        name="retention",
    )(*args)
    return res[0], (res[1] if is_ctx else None)


def _merge_kernel(x_ref, of_ref, ob_ref, z_ref, na_ref, yd_ref, u_ref, oc_ref, gates_ref, mod_ref, d_ref,
                  wglu_ref, bglu_ref, wa_ref, wb_ref, wc_ref, wo_ref, o_ref):
    tot = of_ref[...] + ob_ref[...]
    o_a = jnp.concatenate([_rms(tot[:, h * D_HEAD:(h + 1) * D_HEAD]) * na_ref[...] for h in range(N_HEADS)],
                          axis=1) * _silu(z_ref[...].astype(F32))
    y = _gelu_tanh(u_ref[...].astype(F32) * d_ref[...] + yd_ref[...])
    y = y * _sigmoid(_dot(y, wglu_ref[...]) + bglu_ref[...])
    gt = _sigmoid(gates_ref[...].astype(F32))
    merged = (gt[:, :D_MODEL] * _dot(o_a, wa_ref[...])
              + gt[:, D_MODEL:2 * D_MODEL] * _dot(y, wb_ref[...])
              + gt[:, 2 * D_MODEL:] * _dot(oc_ref[...], wc_ref[...]))
    o_ref[...] = x_ref[...] + mod_ref[2:3, :] * _dot(merged, wo_ref[...])


def _merge(x2d, o_f, o_b, yd, o_c, proj, mod6, lw, rows_per_cond):
    n = x2d.shape[0]
    tm = 512
    row = _mod_row_map(tm, rows_per_cond)
    full = lambda shape: pl.BlockSpec(shape, lambda i: (0,) * len(shape), pipeline_mode=pl.Buffered(1))
    mix = lambda cb: pl.BlockSpec((tm, W_MIX), lambda i, cb=cb: (i, cb))
    return pl.pallas_call(
        _merge_kernel,
        grid=(n // tm,),
        in_specs=[pl.BlockSpec((tm, D_MODEL), lambda i: (i, 0)),
                  mix(0), mix(0), mix(U_ZA // N_HEADS), full((1, D_HEAD)),
                  mix(0), mix(U_UB // N_HEADS), mix(0),
                  pl.BlockSpec((tm, 3 * D_MODEL), lambda i: (i, 0)),
                  pl.BlockSpec((None, 6, D_MODEL), lambda i: (row(i), 0, 0)),
                  full((1, W_MIX)), full((W_MIX, W_MIX)), full((1, W_MIX)),
                  full((W_MIX, D_MODEL)), full((W_MIX, D_MODEL)), full((W_MIX, D_MODEL)),
                  full((D_MODEL, D_MODEL))],
        out_specs=pl.BlockSpec((tm, D_MODEL), lambda i: (i, 0)),
        out_shape=jax.ShapeDtypeStruct((n, D_MODEL), F32),
        compiler_params=_cparams(("parallel",)),
        name="merge",
    )(x2d, o_f, o_b, proj, lw['norm_a'], yd, proj, o_c, proj, mod6, lw['ssm_d'], lw['w_glu'], lw['b_glu'],
      lw['w_br_a'], lw['w_br_b'], lw['w_br_c'], lw['w_o'])


FFN_TM = 256
FFN_CW = 256
FFN_UP_CW = 512


def _ffn_kernel(x_ref, xr_ref, mod_ref, modr_ref, nw_ref, wup_ref, cw_ref, cb_ref, wd_ref, fn_ref, o_ref,
                hu_a, hu_b, *, seq_len, final):
    i = pl.program_id(0)
    tm = x_ref.shape[0]
    blocks_per_seq = max(seq_len // tm, 1)

    @pl.when(i == 0)
    def _():
        hu_a[...] = jnp.zeros_like(hu_a)
        hu_b[...] = jnp.zeros_like(hu_b)

    p = (i + blocks_per_seq - 1) % blocks_per_seq
    has_prev = p > 0
    has_next = p < blocks_per_seq - 1
    rowi = lax.broadcasted_iota(jnp.int32, (tm, FFN_CW), 0) % min(seq_len, tm)
    last_row = min(seq_len, tm) - 1

    def step(new_ref, cur_ref):
        prev_row = jnp.where(has_prev, new_ref[tm - 1:tm, :], 0.0)
        h = _rms(x_ref[...]) * nw_ref[...]
        h = (h * (1.0 + mod_ref[4:5, :]) + mod_ref[3:4, :]).astype(BF16)
        up_done = set()

        def up(q):
            if q not in up_done:
                up_done.add(q)
                cs = slice(q * FFN_UP_CW, (q + 1) * FFN_UP_CW)
                new_ref[:, cs] = jnp.dot(h, wup_ref[:, cs], preferred_element_type=F32)

        def conv(lo):
            cs = slice(lo, lo + FFN_CW)
            up(lo // FFN_UP_CW)
            up((lo + FFN_CW - 1) // FFN_UP_CW)
            next_row = jnp.where(has_next, new_ref[0:1, cs], 0.0)
            hu = cur_ref[:, cs]
            hm = jnp.where(rowi == 0, prev_row[:, cs], pltpu.roll(hu, 1, 0))
            hp = jnp.where(rowi == last_row, next_row, pltpu.roll(hu, tm - 1, 0))
            return cw_ref[0:1, cs] * hm + cw_ref[1:2, cs] * hu + cw_ref[2:3, cs] * hp + cb_ref[:, cs]

        acc = None
        for cb in range(D_FF // FFN_CW):
            lo = cb * FFN_CW
            act = (_silu(conv(lo)) * conv(D_FF + lo)).astype(BF16)
            part = jnp.dot(act, wd_ref[lo:lo + FFN_CW, :], preferred_element_type=F32)
            acc = part if acc is None else acc + part
        for q in range(wup_ref.shape[1] // FFN_UP_CW):
            up(q)
        x2 = xr_ref[...] + modr_ref[5:6, :] * acc
        if final:
            x2 = _rms(x2) * fn_ref[...]
        o_ref[...] = x2

    @pl.when(i % 2 == 0)
    def _():
        step(hu_a, hu_b)

    @pl.when(i % 2 == 1)
    def _():
        step(hu_b, hu_a)


def _ffn(x2d, mod6, lw, final_norm, seq_len, rows_per_cond, final):
    n = x2d.shape[0]
    tm = FFN_TM
    nblk = n // tm
    row = _mod_row_map(tm, rows_per_cond)
    cur = lambda i: jnp.minimum(i, nblk - 1)
    done = lambda i: jnp.maximum(i - 1, 0)
    full = lambda shape: pl.BlockSpec(shape, lambda i: (0,) * len(shape), pipeline_mode=pl.Buffered(1))
    return pl.pallas_call(
        functools.partial(_ffn_kernel, seq_len=seq_len, final=final),
        grid=(nblk + 1,),
        in_specs=[pl.BlockSpec((tm, D_MODEL), lambda i: (cur(i), 0)),
                  pl.BlockSpec((tm, D_MODEL), lambda i: (done(i), 0)),
                  pl.BlockSpec((None, 6, D_MODEL), lambda i: (row(cur(i)), 0, 0)),
                  pl.BlockSpec((None, 6, D_MODEL), lambda i: (row(done(i)), 0, 0)),
                  full((1, D_MODEL)), full((D_MODEL, 2 * D_FF)), full((3, 2 * D_FF)), full((1, 2 * D_FF)),
                  full((D_FF, D_MODEL)), full((1, D_MODEL))],
        out_specs=pl.BlockSpec((tm, D_MODEL), lambda i: (done(i), 0)),
        out_shape=jax.ShapeDtypeStruct((n, D_MODEL), F32),
        scratch_shapes=[pltpu.VMEM((tm, 2 * D_FF), F32), pltpu.VMEM((tm, 2 * D_FF), F32)],
        compiler_params=_cparams(("arbitrary",)),
        name="ffn",
    )(x2d, x2d, mod6, mod6, lw['norm2'], lw['w_up'], lw['w_conv_ffn'], lw['b_conv_ffn'], lw['w_down'],
      final_norm)


def _run_pass(x, mod, layers, consts, init_states, final_norm):
    b, seq_len, _ = x.shape
    n = b * seq_len
    is_ctx = init_states is None
    rows_per_cond = None if is_ctx else seq_len
    step_rows = 4096
    x2d = x.reshape(n, D_MODEL)
    depth = len(layers)
    finals = []
    for li, lw in enumerate(layers):
        mod6 = mod[li]
        proj, ab, u4f = _in_projection(x2d, mod6, lw['norm1'], lw['w_in_main'], lw['w_in_ab'], rows_per_cond)
        pre = _gdn_precompute(proj, ab, lw['w_conv_qkv'], lw['gate_par'], consts['gdn_masks'], consts['gdn_tri'],
                              consts['gdn_levels'], seq_len)
        o_f, o_b, sd = _gdn_scan(pre, None if is_ctx else init_states[0][:, li], seq_len)
        yd, sfin = _s5_mixer(u4f, lw['s5_mats'], None if is_ctx else init_states[1][:, li], seq_len)
        o_c, sr = _retention(proj, consts['ret_tables'], consts.get('rope'),
                             None if is_ctx else init_states[2][:, li], seq_len, step_rows)
        x1 = _merge(x2d, o_f, o_b, yd, o_c, proj, mod6, lw, rows_per_cond)
        x2d = _ffn(x1, mod6, lw, final_norm, seq_len, rows_per_cond, final=(li == depth - 1))
        finals.append((sd, sfin, sr))
    return x2d.reshape(b, seq_len, D_MODEL), finals


def _s5_state_to_rows(s_re, s_im):
    shp = s_re.shape[:3]
    re = s_re.reshape(shp + (S5_BLOCKS, 4, LANE))
    im = s_im.reshape(shp + (S5_BLOCKS, 4, LANE))
    return jnp.concatenate([re, im], axis=-2)


def _s5_rows_to_state(fin):
    nb = fin.shape[2]
    re = fin[:, :, :, 0:4, :].reshape(2, S5_BLOCKS, nb, 8, P_B)
    im = fin[:, :, :, 4:8, :].reshape(2, S5_BLOCKS, nb, 8, P_B)
    perm = lambda t: jnp.transpose(t, (2, 0, 1, 3, 4)).reshape(nb, 2, G_B, P_B)
    return perm(re), perm(im)


def kernel(x_prompt, x_sample, state_delta, state_ssm_re, state_ssm_im, state_ret, c, c_ctx,
           final_norm, norm1, norm2, w_mod, b_mod, w_in, w_conv_qkv, a_log, dt_bias, norm_a, w_br_a,
           ssm_lam_re, ssm_lam_im, ssm_log_dt, ssm_b_re, ssm_b_im, ssm_c_re, ssm_c_im, ssm_d,
           w_glu, b_glu, w_br_b, w_br_c, w_o, w_up, w_conv_ffn, b_conv_ffn, w_down):
    depth = w_in.shape[0]
    dec_b = x_sample.shape[0]
    n_qkv = 3 * W_MIX
    gate_par = jnp.pad(jnp.stack([a_log.reshape(depth, 8), dt_bias.reshape(depth, 8)], axis=1),
                       ((0, 0), (0, 6), (0, LANE - 8)))
    compact, s5_a1, s5_a2 = jax.vmap(_s5_matrices)(ssm_lam_re, ssm_lam_im, ssm_log_dt, ssm_b_re, ssm_b_im,
                                                   ssm_c_re, ssm_c_im)
    expanded = _s5_expand(compact, _s5_expand_consts())
    layers = []
    for i in range(depth):
        wi = w_in[i]
        w_main = jnp.concatenate([wi[:, -3 * D_MODEL:], wi[:, :n_qkv], wi[:, n_qkv + 16:-3 * D_MODEL]],
                                 axis=1).astype(BF16)
        w_ab = jnp.pad(wi[:, n_qkv:n_qkv + 16], ((0, 0), (0, LANE - 16))).astype(BF16)
        layers.append(dict(
            norm1=norm1[i][None], norm2=norm2[i][None], w_in_main=w_main, w_in_ab=w_ab,
            w_conv_qkv=w_conv_qkv[i], gate_par=gate_par[i], norm_a=norm_a[i][None],
            s5_mats=(expanded, i, s5_a1[i], s5_a2[i]),
            ssm_d=ssm_d[i][None], w_glu=w_glu[i].astype(BF16), b_glu=b_glu[i][None],
            w_br_a=w_br_a[i].astype(BF16), w_br_b=w_br_b[i].astype(BF16), w_br_c=w_br_c[i].astype(BF16),
            w_o=w_o[i].astype(BF16), w_up=w_up[i].astype(BF16), w_conv_ffn=w_conv_ffn[i],
            b_conv_ffn=b_conv_ffn[i][None], w_down=w_down[i].astype(BF16)))
    masks = _gdn_masks()
    consts = dict(gdn_masks=masks, gdn_tri=masks[M_INCL:M_INCL + 2].astype(BF16),
                  gdn_levels=masks[M_LEVEL + 1:M_LEVEL + 7].astype(BF16), ret_tables=_ret_tables())
    fnorm = final_norm[None]

    cond8 = jnp.concatenate([c_ctx[None], c, jnp.zeros((8 - 1 - dec_b, D_MODEL), F32)], axis=0)
    mod = _modulation(cond8, w_mod, b_mod).reshape(depth, 8, 6, D_MODEL)

    y_prompt, ctx_finals = _run_pass(x_prompt, mod, layers, consts, None, fnorm)
    consts_lat = dict(consts, rope=_rope_tables(x_sample.shape[1]))
    x0_rows = _s5_state_to_rows(state_ssm_re, state_ssm_im)
    y_sample, _ = _run_pass(x_sample, mod, layers, consts_lat, (state_delta, x0_rows, state_ret), fnorm)

    new_delta = jnp.stack([f[0] for f in ctx_finals], axis=1)
    s5 = [_s5_rows_to_state(f[1]) for f in ctx_finals]
    new_re = jnp.stack([s[0] for s in s5], axis=1)
    new_im = jnp.stack([s[1] for s in s5], axis=1)
    new_ret = jnp.stack([f[2] for f in ctx_finals], axis=1)
    return (y_prompt, y_sample, new_delta, new_re, new_im, new_ret)
```

```python
import functools
import math

import jax
import jax.numpy as jnp
import numpy as np
from jax import lax
from jax.experimental import pallas as pl
from jax.experimental.pallas import tpu as pltpu

F32 = jnp.float32
BF16 = jnp.bfloat16

D_MODEL = 1024
EPS = 1e-6
N_HEADS = 4
D_HEAD = 128
W_MIX = N_HEADS * D_HEAD
GDN_CHUNK = 64
SEG = 256
GRID_W = 64
ROPE_BASE = 10000.0
G_B, P_B, GC_B = 32, 64, 16
S5_TILE = 8
S5_BLOCKS = 4
S5_W = S5_TILE * 128
D_FF = 2816
LANE = 128
VMEM_LIMIT = 56 * 1024 * 1024

U_GATES = 0
U_QA, U_KA, U_VA, U_ZA = 24, 28, 32, 36
U_UB = 40
U_QC, U_KC, U_VC, U_GC = 44, 48, 52, 56
N_PROJ = 60 * LANE
PROJ_TN = 2560
UB_BLOCK = (U_UB * LANE) // PROJ_TN
UB_LOCAL = U_UB * LANE - UB_BLOCK * PROJ_TN
PROJ_DTYPE = BF16
HALO_ROWS = 16


def _cparams(sem):
    return pltpu.CompilerParams(dimension_semantics=sem, vmem_limit_bytes=VMEM_LIMIT)


def _dot(a, b):
    return jnp.dot(a.astype(BF16), b.astype(BF16), preferred_element_type=F32)


def _dot_nt(a, b):
    return lax.dot_general(a.astype(BF16), b.astype(BF16), (((1,), (1,)), ((), ())),
                           preferred_element_type=F32)


def _dot_tn(a, b):
    return lax.dot_general(a.astype(BF16), b.astype(BF16), (((0,), (0,)), ((), ())),
                           preferred_element_type=F32)


def _sigmoid(x):
    return jax.nn.sigmoid(x)


def _silu(x):
    return x * _sigmoid(x)


def _softplus(x):
    return jnp.maximum(x, 0.0) + jnp.log1p(jnp.exp(-jnp.abs(x)))


def _gelu_tanh(x):
    return 0.5 * x * (1.0 + jnp.tanh(math.sqrt(2.0 / math.pi) * (x + 0.044715 * x * x * x)))


def _rms(x):
    return x * lax.rsqrt(jnp.mean(x * x, axis=-1, keepdims=True) + EPS)


def _split3(x):
    hi = x.astype(BF16).astype(F32)
    r1 = x - hi
    mid = r1.astype(BF16).astype(F32)
    lo = (r1 - mid).astype(BF16).astype(F32)
    return hi, mid, lo


def _mod_kernel(c_ref, w_ref, b_ref, o_ref):
    o_ref[...] = _dot(_silu(c_ref[...]), w_ref[...]) + b_ref[...]


def _modulation(cond8, w_mod, b_mod):
    depth, _, n = w_mod.shape
    tn = 1536
    return pl.pallas_call(
        _mod_kernel,
        grid=(depth, n // tn),
        in_specs=[pl.BlockSpec((8, D_MODEL), lambda l, j: (0, 0)),
                  pl.BlockSpec((None, D_MODEL, tn), lambda l, j: (l, 0, j)),
                  pl.BlockSpec((None, 1, tn), lambda l, j: (l, 0, j))],
        out_specs=pl.BlockSpec((None, 8, tn), lambda l, j: (l, 0, j)),
        out_shape=jax.ShapeDtypeStruct((depth, 8, n), F32),
        compiler_params=_cparams(("parallel", "parallel")),
        name="modulation",
    )(cond8, w_mod, b_mod.reshape(depth, 1, n))


def _mod_row_map(tm, rows_per_cond):
    if rows_per_cond is None:
        return lambda i: 0
    return lambda i: 1 + (i * tm) // rows_per_cond


def _inproj_kernel(x_ref, mod_ref, nw_ref, w_ref, wab_ref, proj_ref, ab_ref, u4_ref, h_scr, u_scr):
    j = pl.program_id(1)
    rows8 = u4_ref.shape[1]

    @pl.when(j == 0)
    def _():
        h = _rms(x_ref[...]) * nw_ref[...]
        h = h * (1.0 + mod_ref[1:2, :]) + mod_ref[0:1, :]
        hb = h.astype(BF16)
        h_scr[...] = hb
        ab_ref[...] = jnp.dot(hb, wab_ref[...], preferred_element_type=F32)

    acc = jnp.dot(h_scr[...], w_ref[...], preferred_element_type=F32)
    proj_ref[...] = acc.astype(PROJ_DTYPE)

    @pl.when(j == UB_BLOCK)
    def _():
        for b in range(S5_BLOCKS):
            u_scr[b] = acc[:, UB_LOCAL + LANE * b:UB_LOCAL + LANE * (b + 1)]
        for b in range(S5_BLOCKS):
            for t in range(S5_TILE):
                u4_ref[b, :, LANE * t:LANE * (t + 1)] = u_scr[b, pl.ds(t, rows8, stride=S5_TILE), :]


def _in_projection(x2d, mod6, norm_w, w_main, w_ab, rows_per_cond):
    n = x2d.shape[0]
    tm, tn = 1024, PROJ_TN
    row = _mod_row_map(tm, rows_per_cond)
    return pl.pallas_call(
        _inproj_kernel,
        grid=(n // tm, N_PROJ // tn),
        in_specs=[pl.BlockSpec((tm, D_MODEL), lambda i, j: (i, 0)),
                  pl.BlockSpec((None, 6, D_MODEL), lambda i, j: (row(i), 0, 0)),
                  pl.BlockSpec((1, D_MODEL), lambda i, j: (0, 0)),
                  pl.BlockSpec((D_MODEL, tn), lambda i, j: (0, j)),
                  pl.BlockSpec((D_MODEL, LANE), lambda i, j: (0, 0))],
        out_specs=[pl.BlockSpec((tm, tn), lambda i, j: (i, j)),
                   pl.BlockSpec((tm, LANE), lambda i, j: (i, 0)),
                   pl.BlockSpec((S5_BLOCKS, tm // S5_TILE, S5_W), lambda i, j: (0, i, 0))],
        out_shape=[jax.ShapeDtypeStruct((n, N_PROJ), PROJ_DTYPE),
                   jax.ShapeDtypeStruct((n, LANE), F32),
                   jax.ShapeDtypeStruct((S5_BLOCKS, n // S5_TILE, S5_W), F32)],
        scratch_shapes=[pltpu.VMEM((tm, D_MODEL), BF16), pltpu.VMEM((S5_BLOCKS, tm, LANE), F32)],
        compiler_params=_cparams(("parallel", "arbitrary")),
        name="in_projection",
    )(x2d, mod6, norm_w, w_main, w_ab)


M_INCL, M_STRICT, M_EYE, M_LEVEL = 0, 2, 4, 4
N_MASKS = 11


def _gdn_masks():
    r = np.arange(SEG)[:, None]
    c = np.arange(SEG)[None, :]
    same = (r // GDN_CHUNK) == (c // GDN_CHUNK)
    incl = [same & (r >= c), same & (r <= c)]
    strict = [same & (r > c), same & (r < c)]
    eye = [r == c]
    levels = [((r >> k) == (c >> k)) & ((r >> (k - 1)) != (c >> (k - 1))) for k in range(1, 7)]
    return np.stack(incl + strict + eye + levels).astype(np.float32)


def _gdn_pre_kernel(*refs, segs_per_seq):
    has_halo = segs_per_seq > 1
    q_ref, k_ref, v_ref = refs[:3]
    pos = 3
    halos = None
    if has_halo:
        halos = refs[3:9]
        pos = 9
    (ab_ref, wq_ref, wk_ref, wv_ref, gp_ref, msk_ref, tri_ref, lvl_ref,
     u_ref, w_ref, qg_ref, kg_ref, qkm_ref, eg_ref) = refs[pos:]
    seg = pl.program_id(0)
    rowi = lax.broadcasted_iota(jnp.int32, (SEG, W_MIX), 0)
    lane = lax.broadcasted_iota(jnp.int32, (SEG, LANE), 1)

    def conv_act(x_ref, cw_ref, prev_ref, next_ref):
        x = x_ref[...].astype(F32)
        xm = pltpu.roll(x, 1, 0)
        xp = pltpu.roll(x, SEG - 1, 0)
        if has_halo:
            p = seg % segs_per_seq
            prev = jnp.where(p > 0, prev_ref[HALO_ROWS - 1:HALO_ROWS, :].astype(F32), 0.0)
            nxt = jnp.where(p < segs_per_seq - 1, next_ref[0:1, :].astype(F32), 0.0)
        else:
            prev = jnp.zeros((1, W_MIX), F32)
            nxt = prev
        xm = jnp.where(rowi == 0, prev, xm)
        xp = jnp.where(rowi == SEG - 1, nxt, xp)
        return _silu(cw_ref[0:1, :] * xm + cw_ref[1:2, :] * x + cw_ref[2:3, :] * xp)

    hq = halos[0:2] if has_halo else (None, None)
    hk = halos[2:4] if has_halo else (None, None)
    hv = halos[4:6] if has_halo else (None, None)
    qa_all = conv_act(q_ref, wq_ref, *hq)
    ka_all = conv_act(k_ref, wk_ref, *hk)
    va_all = conv_act(v_ref, wv_ref, *hv)

    abv = ab_ref[...]
    log_a = -jnp.exp(gp_ref[0:1, :]) * _softplus(abv + gp_ref[1:2, :])
    beta_all = _sigmoid(abv)

    def col(x, idx):
        return jnp.sum(jnp.where(lane == idx, x, 0.0), axis=1, keepdims=True)

    heads = []
    for h in range(N_HEADS):
        hs = slice(h * D_HEAD, (h + 1) * D_HEAD)
        qa, ka = qa_all[:, hs], ka_all[:, hs]
        qn = qa * lax.rsqrt(jnp.sum(qa * qa, axis=-1, keepdims=True) + EPS) * (D_HEAD ** -0.5)
        kn = ka * lax.rsqrt(jnp.sum(ka * ka, axis=-1, keepdims=True) + EPS)
        heads.append(dict(qn=qn, kn=kn, va=va_all[:, hs], kk=_dot_nt(kn, kn), qk=_dot_nt(qn, kn)))

    chains = []
    for h in range(N_HEADS):
        for d in range(2):
            hd = heads[h]
            g = jnp.broadcast_to(col(log_a, d * N_HEADS + h), (SEG, LANE))
            beta = col(beta_all, 2 * N_HEADS + d * N_HEADS + h)
            ghi, gmid, glo = _split3(g)
            pieces = jnp.where(lane == 0, ghi, jnp.where(lane == 1, gmid, jnp.where(lane == 2, glo, 0.0)))
            cs = jnp.dot(tri_ref[d], pieces.astype(BF16), preferred_element_type=F32)
            gc = jnp.broadcast_to(jnp.sum(jnp.where(lane < 3, cs, 0.0), axis=1, keepdims=True), (SEG, LANE))
            chi, cmid, clo = _split3(gc)
            a_mat = jnp.where(lane == 0, chi, jnp.where(lane == 1, cmid, jnp.where(lane == 2, clo,
                              jnp.where(lane < 6, 1.0, 0.0))))
            b_mat = jnp.where(lane < 3, 1.0, jnp.where(lane == 3, -chi, jnp.where(lane == 4, -cmid,
                              jnp.where(lane == 5, -clo, 0.0))))
            e = jnp.exp(jnp.minimum(_dot_nt(a_mat, b_mat), 0.0))
            lmat = (beta * hd['kk']) * (e * msk_ref[M_STRICT + d])
            chains.append(dict(h=h, d=d, beta=beta, gc=gc, lmat=lmat.astype(BF16),
                               qkm=hd['qk'] * (e * msk_ref[M_INCL + d]),
                               x=(msk_ref[M_EYE] - lmat * msk_ref[M_LEVEL + 1]).astype(BF16)))

    for lvl in range(2, 7):
        for ch in chains:
            c = ch['lmat'] * lvl_ref[lvl - 1]
            t = jnp.dot(c, ch['x'], preferred_element_type=F32).astype(BF16)
            ch['x'] = ch['x'] - jnp.dot(ch['x'], t, preferred_element_type=F32).astype(BF16)

    for ch in chains:
        h, d, beta, gc = ch['h'], ch['d'], ch['beta'], ch['gc']
        hd = heads[h]
        qn, kn, va = hd['qn'], hd['kn'], hd['va']
        hs = slice(h * D_HEAD, (h + 1) * D_HEAD)
        egc = jnp.exp(gc)
        rhs = jnp.concatenate([va * beta, kn * (beta * egc)], axis=1).astype(BF16)
        uw = jnp.dot(ch['x'], rhs, preferred_element_type=F32)
        u_ref[d, :, hs] = uw[:, :D_HEAD]
        w_ref[d, :, hs] = uw[:, D_HEAD:].astype(BF16)
        qg_ref[d, :, hs] = (qn * egc).astype(BF16)
        edge = GDN_CHUNK - 1 if d == 0 else 0
        for c4 in range(SEG // GDN_CHUNK):
            lo_r = c4 * GDN_CHUNK
            rs = slice(lo_r, lo_r + GDN_CHUNK)
            g_edge = gc[lo_r + edge:lo_r + edge + 1, :]
            kg_ref[d, rs, hs] = (kn[rs, :] * jnp.exp(g_edge - gc[rs, :])).astype(BF16)
            eg_ref[d, c4, h:h + 1, :] = jnp.exp(g_edge)
            qkm_ref[d, h, rs, :] = ch['qkm'][rs, rs].astype(BF16)


def _gdn_precompute(proj, ab, wconv, gate_par, masks, tri, levels, seq_len):
    n = proj.shape[0]
    segs_per_seq = seq_len // SEG
    nseg = n // SEG
    units = (U_QA // N_HEADS, U_KA // N_HEADS, U_VA // N_HEADS)
    in_specs = [pl.BlockSpec((SEG, W_MIX), lambda s, cb=cb: (s, cb)) for cb in units]
    args = [proj, proj, proj]
    if segs_per_seq > 1:
        r8 = SEG // HALO_ROWS
        last8 = n // HALO_ROWS - 1
        for cb in units:
            in_specs.append(pl.BlockSpec((HALO_ROWS, W_MIX), lambda s, cb=cb: (jnp.maximum(s * r8 - 1, 0), cb)))
            in_specs.append(pl.BlockSpec((HALO_ROWS, W_MIX),
                                         lambda s, cb=cb: (jnp.minimum((s + 1) * r8, last8), cb)))
            args += [proj, proj]
    in_specs += [pl.BlockSpec((SEG, LANE), lambda s: (s, 0))]
    in_specs += [pl.BlockSpec((3, W_MIX), lambda s, o=o: (0, o)) for o in range(3)]
    in_specs += [pl.BlockSpec((8, LANE), lambda s: (0, 0)),
                 pl.BlockSpec((N_MASKS, SEG, SEG), lambda s: (0, 0, 0)),
                 pl.BlockSpec((2, SEG, SEG), lambda s: (0, 0, 0)),
                 pl.BlockSpec((6, SEG, SEG), lambda s: (0, 0, 0))]
    args += [ab, wconv, wconv, wconv, gate_par, masks, tri, levels]
    big = pl.BlockSpec((2, SEG, W_MIX), lambda s: (0, s, 0))
    cpseg = SEG // GDN_CHUNK
    return pl.pallas_call(
        functools.partial(_gdn_pre_kernel, segs_per_seq=segs_per_seq),
        grid=(nseg,),
        in_specs=in_specs,
        out_specs=[big, big, big, big,
                   pl.BlockSpec((2, N_HEADS, SEG, GDN_CHUNK), lambda s: (0, 0, s, 0)),
                   pl.BlockSpec((2, cpseg, N_HEADS, LANE), lambda s: (0, s, 0, 0))],
        out_shape=[jax.ShapeDtypeStruct((2, n, W_MIX), F32),
                   jax.ShapeDtypeStruct((2, n, W_MIX), BF16),
                   jax.ShapeDtypeStruct((2, n, W_MIX), BF16),
                   jax.ShapeDtypeStruct((2, n, W_MIX), BF16),
                   jax.ShapeDtypeStruct((2, N_HEADS, n, GDN_CHUNK), BF16),
                   jax.ShapeDtypeStruct((2, n // GDN_CHUNK, N_HEADS, LANE), F32)],
        compiler_params=_cparams(("parallel",)),
        name="gdn_precompute",
    )(*args)


GDN_SCAN_ROWS = 1024


def _gdn_scan_kernel(*refs, chunks_per_seq, has_init):
    ins = refs[:12]
    dir_refs = [ins[0::2], ins[1::2]]
    if has_init:
        s0_ref, of_ref, ob_ref, s_scr = refs[12:]
        fin_refs = None
    else:
        of_ref, ob_ref, sff_ref, sfb_ref, s_scr = refs[12:]
        fin_refs = (sff_ref, sfb_ref)
    o_refs = (of_ref, ob_ref)
    nchunk = of_ref.shape[0] // GDN_CHUNK
    cps = chunks_per_seq

    @pl.when(pl.program_id(1) == 0)
    def _():
        if has_init:
            s_scr[...] = s0_ref[...]
        else:
            s_scr[...] = jnp.zeros_like(s_scr)

    def body(i, carry):
        chains = []
        for h in range(N_HEADS):
            for d in range(2):
                c = i if d == 0 else nchunk - 1 - i
                chains.append(dict(h=h, d=d, c=c, hs=slice(h * D_HEAD, (h + 1) * D_HEAD),
                                   rsl=pl.ds(pl.multiple_of(c * GDN_CHUNK, GDN_CHUNK), GDN_CHUNK)))
        for ch in chains:
            s = s_scr[ch['d'], ch['h']]
            if not has_init:
                s = jnp.where(i % cps == 0, 0.0, s)
            ch['s'] = s
            ch['sb'] = s.astype(BF16)
        for ch in chains:
            u_ref, w_ref, qg_ref = dir_refs[ch['d']][:3]
            rsl, hs = ch['rsl'], ch['hs']
            ch['vb'] = (u_ref[rsl, hs] - jnp.dot(w_ref[rsl, hs], ch['sb'], preferred_element_type=F32)).astype(BF16)
            ch['o'] = jnp.dot(qg_ref[rsl, hs], ch['sb'], preferred_element_type=F32)
        for ch in chains:
            kg_ref, qkm_ref, eg_ref = dir_refs[ch['d']][3:]
            rsl, hs, h = ch['rsl'], ch['hs'], ch['h']
            ch['o'] = ch['o'] + jnp.dot(qkm_ref[h, rsl, :], ch['vb'], preferred_element_type=F32)
            ch['s_new'] = ch['s'] * eg_ref[ch['c']][h:h + 1, :] + _dot_tn(kg_ref[rsl, hs], ch['vb'])
        for ch in chains:
            s_scr[ch['d'], ch['h']] = ch['s_new']
            o_refs[ch['d']][ch['rsl'], ch['hs']] = ch['o']
        if not has_init:
            @pl.when(i % cps == cps - 1)
            def _():
                for ch in chains:
                    fin_refs[ch['d']][ch['c'] // cps, ch['h']] = ch['s_new']
        return carry

    lax.fori_loop(0, nchunk, body, 0)


def _gdn_scan(pre, s0, seq_len):
    n = pre[0].shape[1]
    br = GDN_SCAN_ROWS
    has_init = s0 is not None
    group_rows = seq_len if has_init else n
    ngroups, nblk = n // group_rows, group_rows // br
    fwd = lambda g, j: g * nblk + j
    bwd = lambda g, j: g * nblk + (nblk - 1 - j)
    in_specs, args = [], []
    for arr, kind in zip(pre, ("row", "row", "row", "row", "qkm", "eg")):
        for d, pos in ((0, fwd), (1, bwd)):
            if kind == "row":
                in_specs.append(pl.BlockSpec((None, br, W_MIX), lambda g, j, d=d, pos=pos: (d, pos(g, j), 0)))
            elif kind == "qkm":
                in_specs.append(pl.BlockSpec((None, N_HEADS, br, GDN_CHUNK),
                                             lambda g, j, d=d, pos=pos: (d, 0, pos(g, j), 0)))
            else:
                in_specs.append(pl.BlockSpec((None, br // GDN_CHUNK, N_HEADS, LANE),
                                             lambda g, j, d=d, pos=pos: (d, pos(g, j), 0, 0)))
            args.append(arr)
    out_specs = [pl.BlockSpec((br, W_MIX), lambda g, j: (fwd(g, j), 0)),
                 pl.BlockSpec((br, W_MIX), lambda g, j: (bwd(g, j), 0))]
    out_shape = [jax.ShapeDtypeStruct((n, W_MIX), F32), jax.ShapeDtypeStruct((n, W_MIX), F32)]
    if has_init:
        in_specs.append(pl.BlockSpec((None, 2, N_HEADS, D_HEAD, D_HEAD), lambda g, j: (g, 0, 0, 0, 0)))
        args.append(s0)
    else:
        spb = br // seq_len
        fin_shape = jax.ShapeDtypeStruct((n // seq_len, N_HEADS, D_HEAD, D_HEAD), F32)
        out_specs += [pl.BlockSpec((spb, N_HEADS, D_HEAD, D_HEAD), lambda g, j: (fwd(g, j), 0, 0, 0)),
                      pl.BlockSpec((spb, N_HEADS, D_HEAD, D_HEAD), lambda g, j: (bwd(g, j), 0, 0, 0))]
        out_shape += [fin_shape, fin_shape]
    res = pl.pallas_call(
        functools.partial(_gdn_scan_kernel, chunks_per_seq=seq_len // GDN_CHUNK, has_init=has_init),
        grid=(ngroups, nblk),
        in_specs=in_specs, out_specs=out_specs, out_shape=out_shape,
        scratch_shapes=[pltpu.VMEM((2, N_HEADS, D_HEAD, D_HEAD), F32)],
        compiler_params=_cparams(("parallel", "arbitrary")),
        name="gdn_scan",
    )(*args)
    fin = None if has_init else jnp.stack([res[2], res[3]], axis=1)
    return res[0], res[1], fin


def _s5_matrices(lam_re, lam_im, log_dt, b_re, b_im, c_re, c_im):
    t = S5_TILE
    step = jnp.exp(log_dt)[:, :, None]
    lr, li = lam_re * step, lam_im * step
    js = jnp.arange(t + 1, dtype=F32)[:, None, None, None]
    mag = jnp.exp(lr[None] * js)
    pw_re, pw_im = mag * jnp.cos(li[None] * js), mag * jnp.sin(li[None] * js)
    a_re, a_im = pw_re[1], pw_im[1]
    den = lam_re * lam_re + lam_im * lam_im
    z_re = ((a_re - 1.0) * lam_re + a_im * lam_im) / den
    z_im = (a_im * lam_re - (a_re - 1.0) * lam_im) / den
    zb_re = z_re[..., None] * b_re[None] - z_im[..., None] * b_im[None]
    zb_im = z_re[..., None] * b_im[None] + z_im[..., None] * b_re[None]
    w_re = pw_re[..., None] * zb_re[None] - pw_im[..., None] * zb_im[None]
    w_im = pw_re[..., None] * zb_im[None] + pw_im[..., None] * zb_re[None]
    taps = (jnp.einsum('gcp,jdgpi->jdgic', c_re, w_re[:t]) - jnp.einsum('gcp,jdgpi->jdgic', c_im, w_im[:t]))
    nb = S5_BLOCKS
    hi = lax.Precision.HIGHEST

    def blocked(x, axis):
        return x.reshape(x.shape[:axis] + (nb, 8) + x.shape[axis + 1:])

    ss, tt, jj = np.meshgrid(np.arange(t), np.arange(t), np.arange(t), indexing='ij')
    sel_tap = np.stack([tt - ss == jj, ss - tt == jj]).astype(np.float32)
    s2, e2 = np.meshgrid(np.arange(t), np.arange(t + 1), indexing='ij')
    sel_in = np.stack([e2 == t - 1 - s2, e2 == s2]).astype(np.float32)
    sel_out = np.stack([e2 == s2 + 1, e2 == t - s2]).astype(np.float32)

    kst = blocked(jnp.einsum('dstj,jdgic->stgic', sel_tap, taps, precision=hi), 2)
    taps_sum = jnp.transpose(kst, (2, 0, 3, 4, 1, 5)).reshape(nb, S5_W, LANE)
    ws = jnp.stack([jnp.einsum('dse,edgpc->dsgpc', sel_in, w_re, precision=hi),
                    jnp.einsum('dse,edgpc->dsgpc', sel_in, w_im, precision=hi)], axis=3)
    gm = jnp.transpose(blocked(ws, 2), (0, 2, 1, 3, 6, 4, 5)).reshape(2, nb, S5_W, LANE)
    pr = jnp.einsum('dte,edgp->dtgp', sel_out, pw_re, precision=hi)
    pi = jnp.einsum('dte,edgp->dtgp', sel_out, pw_im, precision=hi)
    wo_re = c_re * pr[:, :, :, None, :] - c_im * pi[:, :, :, None, :]
    wo_im = c_re * pi[:, :, :, None, :] + c_im * pr[:, :, :, None, :]
    wo = blocked(jnp.stack([wo_re, -wo_im], axis=1), 3)
    cm = jnp.transpose(wo, (0, 3, 1, 4, 6, 2, 5)).reshape(2, nb, S5_W, LANE)
    a8r = pw_re[t].reshape(2, nb, 4, LANE)
    a8i = pw_im[t].reshape(2, nb, 4, LANE)
    a1 = jnp.concatenate([a8r, a8r], axis=2)
    a2 = jnp.concatenate([-a8i, a8i], axis=2)
    taps_both = jnp.stack([taps_sum, jnp.zeros_like(taps_sum)])
    return jnp.stack([taps_both, gm, cm]), a1, a2


def _s5_expand_consts():
    col = np.arange(S5_W)
    src = np.arange(LANE)
    ex_tc = (src[:, None] == ((col // LANE) * GC_B + col % GC_B)[None, :])
    ex_rp = (src[:, None] == ((col // (S5_W // 2)) * P_B + col % P_B)[None, :])
    g_tc = (col // GC_B) % 8
    g_rp = (col // P_B) % 8
    masks = [g_tc[:, None] == g_tc[None, :], g_tc[:, None] == g_rp[None, :], g_rp[:, None] == g_tc[None, :]]
    return (np.stack([ex_tc, ex_rp, ex_tc]).astype(BF16), np.stack(masks).astype(BF16))


def _s5_expand_kernel(c_ref, ex_ref, m_ref, o_ref):
    full = jnp.dot(c_ref[...].astype(BF16), ex_ref[...], preferred_element_type=F32)
    o_ref[...] = (full * m_ref[...].astype(F32)).astype(BF16)


def _s5_expand(compact, consts):
    ex, masks = consts
    nl, _, _, nb = compact.shape[:4]
    return pl.pallas_call(
        _s5_expand_kernel,
        grid=(3, nl, 2, nb),
        in_specs=[pl.BlockSpec((None, None, None, None, S5_W, LANE), lambda k, l, d, b: (l, k, d, b, 0, 0)),
                  pl.BlockSpec((None, LANE, S5_W), lambda k, l, d, b: (k, 0, 0)),
                  pl.BlockSpec((None, S5_W, S5_W), lambda k, l, d, b: (k, 0, 0))],
        out_specs=pl.BlockSpec((None, None, None, None, S5_W, S5_W), lambda k, l, d, b: (l, k, d, b, 0, 0)),
        out_shape=jax.ShapeDtypeStruct(compact.shape[:4] + (S5_W, S5_W), BF16),
        compiler_params=_cparams(("parallel",) * 4),
        name="s5_expand",
    )(compact, ex, masks)


def _s5_in_kernel(u_ref, gm_ref, g_ref):
    rows = u_ref.shape[0]
    g = _dot(u_ref[...], gm_ref[...])
    for s in range(8):
        g_ref[pl.ds(s, rows, stride=8), :] = g[:, LANE * s:LANE * (s + 1)]


S5_KIND_TAP, S5_KIND_IN, S5_KIND_OUT = 0, 1, 2


def _s5_tile_inputs(u4f, mats, layer):
    nb, rows, _ = u4f.shape
    return pl.pallas_call(
        _s5_in_kernel,
        grid=(nb, 2),
        in_specs=[pl.BlockSpec((None, rows, S5_W), lambda b, d: (b, 0, 0)),
                  pl.BlockSpec((None, None, None, None, S5_W, S5_W),
                               lambda b, d: (layer, S5_KIND_IN, d, b, 0, 0))],
        out_specs=pl.BlockSpec((None, None, rows * 8, LANE), lambda b, d: (d, b, 0, 0)),
        out_shape=jax.ShapeDtypeStruct((2, nb, rows * 8, LANE), F32),
        compiler_params=_cparams(("parallel", "parallel")),
        name="s5_tile_inputs",
    )(u4f, mats)


def _s5_scan_kernel(*refs, nseq, rows_per_seq, has_init):
    g_ref, a1_ref, a2_ref = refs[:3]
    if has_init:
        x0_ref, xp_ref = refs[3:]
        fin_ref = None
    else:
        xp_ref, fin_ref = refs[3:]
    nblk = g_ref.shape[0]
    backward = pl.program_id(0) == 1
    chains = [(b, s) for b in range(nblk) for s in range(nseq)]

    def body(i, xs):
        n = jnp.where(backward, rows_per_seq - 1 - i, i)
        out = []
        for (b, s), x in zip(chains, xs):
            r = s * rows_per_seq + n
            xp_ref[b, r] = x
            out.append(a1_ref[b] * x + a2_ref[b] * pltpu.roll(x, 4, 0) + g_ref[b, r])
        return tuple(out)

    if has_init:
        init = tuple(x0_ref[b] for (b, s) in chains)
    else:
        init = tuple(jnp.zeros((8, LANE), F32) for _ in chains)
    xs = lax.fori_loop(0, rows_per_seq, body, init)
    if not has_init:
        for (b, s), x in zip(chains, xs):
            fin_ref[s] = x


def _s5_scan(g5, a1, a2, x0, seq_rows):
    _, nb, rows, _, _ = g5.shape
    nseq = rows // seq_rows
    if x0 is None:
        return pl.pallas_call(
            functools.partial(_s5_scan_kernel, nseq=nseq, rows_per_seq=seq_rows, has_init=False),
            grid=(2, nb),
            in_specs=[pl.BlockSpec((None, 1, rows, 8, LANE), lambda d, b: (d, b, 0, 0, 0)),
                      pl.BlockSpec((None, 1, 8, LANE), lambda d, b: (d, b, 0, 0)),
                      pl.BlockSpec((None, 1, 8, LANE), lambda d, b: (d, b, 0, 0))],
            out_specs=[pl.BlockSpec((None, 1, rows, 8, LANE), lambda d, b: (d, b, 0, 0, 0)),
                       pl.BlockSpec((None, None, nseq, 8, LANE), lambda d, b: (d, b, 0, 0, 0))],
            out_shape=[jax.ShapeDtypeStruct(g5.shape, F32),
                       jax.ShapeDtypeStruct((2, nb, nseq, 8, LANE), F32)],
            compiler_params=_cparams(("parallel", "parallel")),
            name="s5_scan_ctx",
        )(g5, a1, a2)
    xprev = pl.pallas_call(
        functools.partial(_s5_scan_kernel, nseq=1, rows_per_seq=seq_rows, has_init=True),
        grid=(2, nseq),
        in_specs=[pl.BlockSpec((None, nb, seq_rows, 8, LANE), lambda d, s: (d, 0, s, 0, 0)),
                  pl.BlockSpec((None, nb, 8, LANE), lambda d, s: (d, 0, 0, 0)),
                  pl.BlockSpec((None, nb, 8, LANE), lambda d, s: (d, 0, 0, 0)),
                  pl.BlockSpec((None, None, nb, 8, LANE), lambda d, s: (s, d, 0, 0, 0))],
        out_specs=pl.BlockSpec((None, nb, seq_rows, 8, LANE), lambda d, s: (d, 0, s, 0, 0)),
        out_shape=jax.ShapeDtypeStruct(g5.shape, F32),
        compiler_params=_cparams(("parallel", "parallel")),
        name="s5_scan_latent",
    )(g5, a1, a2, x0)
    return xprev, None


def _s5_out_kernel(u_ref, xp_ref, tm_ref, cm_ref, y_ref):
    tr = u_ref.shape[0]
    acc = jnp.dot(u_ref[...].astype(BF16), tm_ref[...], preferred_element_type=F32)
    for d in range(2):
        xp = jnp.concatenate([xp_ref[d, pl.ds(s, tr, stride=8), :] for s in range(8)], axis=1)
        acc += _dot(xp, cm_ref[d])
    for t in range(S5_TILE):
        y_ref[pl.ds(t, tr, stride=S5_TILE), :] = acc[:, LANE * t:LANE * (t + 1)]


def _s5_outputs(u4f, xprev, mats, layer):
    nb, rows, _ = u4f.shape
    tr = 512
    return pl.pallas_call(
        _s5_out_kernel,
        grid=(nb, rows // tr),
        in_specs=[pl.BlockSpec((None, tr, S5_W), lambda b, r: (b, r, 0)),
                  pl.BlockSpec((2, None, tr * 8, LANE), lambda b, r: (0, b, r, 0)),
                  pl.BlockSpec((None, None, None, None, S5_W, S5_W),
                               lambda b, r: (layer, S5_KIND_TAP, 0, b, 0, 0)),
                  pl.BlockSpec((None, None, 2, None, S5_W, S5_W),
                               lambda b, r: (layer, S5_KIND_OUT, 0, b, 0, 0))],
        out_specs=pl.BlockSpec((tr * S5_TILE, LANE), lambda b, r: (r, b)),
        out_shape=jax.ShapeDtypeStruct((rows * S5_TILE, S5_BLOCKS * LANE), F32),
        compiler_params=_cparams(("parallel", "parallel")),
        name="s5_outputs",
    )(u4f, xprev, mats, mats)


def _s5_mixer(u4f, s5, x0, seq_len):
    mats, layer, a1, a2 = s5
    nb, rows, _ = u4f.shape
    g = _s5_tile_inputs(u4f, mats, layer)
    xprev, fin = _s5_scan(g.reshape(2, nb, rows, 8, LANE), a1, a2, x0, seq_len // S5_TILE)
    yd = _s5_outputs(u4f, xprev.reshape(2, nb, rows * 8, LANE), mats, layer)
    return yd, fin


def _ret_tables():
    h = np.arange(N_HEADS, dtype=np.float64)
    lg = np.log1p(-np.exp2(-5.0 - h))
    lgf, lgb = lg[:, None], lg[::-1][:, None]
    i = np.arange(SEG, dtype=np.float64)
    dist = i[:, None] - i[None, :]
    d_f = np.where(dist >= 0, np.exp(lgf[:, :, None] * np.maximum(dist, 0.0)), 0.0)
    d_b = np.where(dist <= 0, np.exp(lgb[:, :, None] * np.maximum(-dist, 0.0)), 0.0)
    dsum = d_f + d_b
    dec = np.stack([np.exp(lgf * (i + 1.0)), np.exp(lgf * (SEG - 1.0 - i)),
                    np.exp(lgb * (SEG - i)), np.exp(lgb * i)], axis=1)
    dec = np.broadcast_to(dec[..., None], (N_HEADS, 4, SEG, LANE))
    cd = np.stack([np.exp(lgf[:, 0] * SEG), np.exp(lgb[:, 0] * SEG)], axis=1)
    cd = np.broadcast_to(np.pad(cd, ((0, 0), (0, 6)))[..., None], (N_HEADS, 8, LANE))
    f32 = lambda a: np.ascontiguousarray(a, dtype=np.float32)
    return f32(dsum), f32(dec), f32(cd)


def _rope_tables(n_tokens):
    n_rows = n_tokens // GRID_W
    rows = np.repeat(np.arange(n_rows), GRID_W).astype(np.float32)
    cols = np.tile(np.arange(GRID_W), n_rows).astype(np.float32)
    pairs = D_HEAD // 4
    freqs = (np.float32(ROPE_BASE) ** (-np.arange(pairs, dtype=np.float32) / np.float32(pairs))).astype(np.float32)
    ang = np.concatenate([rows[:, None] * freqs, cols[:, None] * freqs], axis=-1).astype(np.float64)
    cos, sin = np.cos(ang).astype(np.float32), np.sin(ang).astype(np.float32)
    return np.concatenate([cos, cos], axis=-1), np.concatenate([-sin, sin], axis=-1)


def _ret_kernel(*refs, is_ctx):
    q_ref, k_ref, v_ref, g_ref, dsum_ref, dec_ref, cd_ref = refs[:7]
    if is_ctx:
        o_ref, sfin_ref = refs[7:]
    else:
        c2_ref, s2_ref, s0_ref, o_ref, sbin_scr = refs[7:]
    rows = q_ref.shape[0]
    nchunk = rows // SEG
    scale = D_HEAD ** -0.5

    def chunk_rows(c):
        return pl.ds(pl.multiple_of(c * SEG, SEG), SEG)

    heads = range(RET_GROUP)
    hsl = [slice(j * D_HEAD, (j + 1) * D_HEAD) for j in heads]

    def load_qk(rsl, j):
        q, k = q_ref[rsl, hsl[j]].astype(F32), k_ref[rsl, hsl[j]].astype(F32)
        if not is_ctx:
            c2, s2 = c2_ref[rsl, :], s2_ref[rsl, :]
            q = q * c2 + pltpu.roll(q, D_HEAD // 2, 1) * s2
            k = k * c2 + pltpu.roll(k, D_HEAD // 2, 1) * s2
        return q, k * scale

    def finish(rsl, j, r):
        o_ref[rsl, hsl[j]] = _rms(r) * _silu(g_ref[rsl, hsl[j]].astype(F32))

    if is_ctx:
        def body(c, carry):
            rsl = chunk_rows(c)
            qk = [load_qk(rsl, j) for j in heads]
            vs = [v_ref[rsl, hsl[j]] for j in heads]
            att = [_dot_nt(qk[j][0], qk[j][1]) * dsum_ref[j] for j in heads]
            inner = [_dot(att[j], vs[j]) for j in heads]
            for j in heads:
                finish(rsl, j, inner[j])
                sfin_ref[c, 0, j] = _dot_tn(qk[j][1] * dec_ref[j, 1], vs[j])
                sfin_ref[c, 1, j] = _dot_tn(qk[j][1] * dec_ref[j, 3], vs[j])
            return carry
        lax.fori_loop(0, nchunk, body, 0, unroll=2)
    else:
        def back(i, s_b):
            c = nchunk - 1 - i
            rsl = chunk_rows(c)
            ks = [load_qk(rsl, j)[1] for j in heads]
            sbin_scr[c] = jnp.stack(s_b)
            return tuple(s_b[j] * cd_ref[j, 1:2, :] + _dot_tn(ks[j] * dec_ref[j, 3], v_ref[rsl, hsl[j]])
                         for j in heads)
        lax.fori_loop(0, nchunk, back, tuple(s0_ref[1, j] for j in heads), unroll=2)

        def fwd(c, s_f):
            rsl = chunk_rows(c)
            qk = [load_qk(rsl, j) for j in heads]
            vs = [v_ref[rsl, hsl[j]] for j in heads]
            att = [_dot_nt(qk[j][0], qk[j][1]) * dsum_ref[j] for j in heads]
            cross = [_dot(qk[j][0] * dec_ref[j, 0], s_f[j]) + _dot(qk[j][0] * dec_ref[j, 2], sbin_scr[c, j])
                     for j in heads]
            inner = [_dot(att[j], vs[j]) for j in heads]
            for j in heads:
                finish(rsl, j, inner[j] + cross[j])
            return tuple(s_f[j] * cd_ref[j, 0:1, :] + _dot_tn(qk[j][1] * dec_ref[j, 1], vs[j]) for j in heads)
        lax.fori_loop(0, nchunk, fwd, tuple(s0_ref[0, j] for j in heads), unroll=2)


RET_GROUP = 2


def _retention(proj, tables, rope, s0, seq_len, step_rows):
    dsum, dec, cd = tables
    n = proj.shape[0]
    is_ctx = s0 is None
    gw = RET_GROUP * D_HEAD
    blk = lambda unit: pl.BlockSpec((step_rows, gw), lambda s, h, unit=unit: (s, unit // RET_GROUP + h))
    in_specs = [blk(U_QC), blk(U_KC), blk(U_VC), blk(U_GC),
                pl.BlockSpec((RET_GROUP, SEG, SEG), lambda s, h: (h, 0, 0)),
                pl.BlockSpec((RET_GROUP, 4, SEG, LANE), lambda s, h: (h, 0, 0, 0)),
                pl.BlockSpec((RET_GROUP, 8, LANE), lambda s, h: (h, 0, 0))]
    args = [proj, proj, proj, proj, dsum, dec, cd]
    out_specs = [pl.BlockSpec((step_rows, gw), lambda s, h: (s, h))]
    out_shape = [jax.ShapeDtypeStruct((n, W_MIX), F32)]
    scratch = []
    if is_ctx:
        assert seq_len == SEG
        out_specs.append(pl.BlockSpec((step_rows // SEG, 2, RET_GROUP, D_HEAD, D_HEAD),
                                      lambda s, h: (s, 0, h, 0, 0)))
        out_shape.append(jax.ShapeDtypeStruct((n // SEG, 2, N_HEADS, D_HEAD, D_HEAD), F32))
    else:
        assert step_rows == seq_len
        in_specs += [pl.BlockSpec((step_rows, LANE), lambda s, h: (0, 0)),
                     pl.BlockSpec((step_rows, LANE), lambda s, h: (0, 0)),
                     pl.BlockSpec((None, 2, RET_GROUP, D_HEAD, D_HEAD), lambda s, h: (s, 0, h, 0, 0))]
        args += [rope[0], rope[1], s0]
        scratch = [pltpu.VMEM((step_rows // SEG, RET_GROUP, D_HEAD, D_HEAD), F32)]
    res = pl.pallas_call(
        functools.partial(_ret_kernel, is_ctx=is_ctx),
        grid=(n // step_rows, N_HEADS // RET_GROUP),
        in_specs=in_specs, out_specs=out_specs, out_shape=out_shape, scratch_shapes=scratch,
        compiler_params=_cparams(("parallel", "parallel")),
        name="retention",
    )(*args)
    return res[0], (res[1] if is_ctx else None)


def _merge_kernel(x_ref, of_ref, ob_ref, z_ref, na_ref, yd_ref, u_ref, oc_ref, gates_ref, mod_ref, d_ref,
                  wglu_ref, bglu_ref, wa_ref, wb_ref, wc_ref, wo_ref, o_ref):
    tot = of_ref[...] + ob_ref[...]
    o_a = jnp.concatenate([_rms(tot[:, h * D_HEAD:(h + 1) * D_HEAD]) * na_ref[...] for h in range(N_HEADS)],
                          axis=1) * _silu(z_ref[...].astype(F32))
    y = _gelu_tanh(u_ref[...].astype(F32) * d_ref[...] + yd_ref[...])
    y = y * _sigmoid(_dot(y, wglu_ref[...]) + bglu_ref[...])
    gt = _sigmoid(gates_ref[...].astype(F32))
    merged = (gt[:, :D_MODEL] * _dot(o_a, wa_ref[...])
              + gt[:, D_MODEL:2 * D_MODEL] * _dot(y, wb_ref[...])
              + gt[:, 2 * D_MODEL:] * _dot(oc_ref[...], wc_ref[...]))
    o_ref[...] = x_ref[...] + mod_ref[2:3, :] * _dot(merged, wo_ref[...])


def _merge(x2d, o_f, o_b, yd, o_c, proj, mod6, lw, rows_per_cond):
    n = x2d.shape[0]
    tm = 512
    row = _mod_row_map(tm, rows_per_cond)
    full = lambda shape: pl.BlockSpec(shape, lambda i: (0,) * len(shape), pipeline_mode=pl.Buffered(1))
    mix = lambda cb: pl.BlockSpec((tm, W_MIX), lambda i, cb=cb: (i, cb))
    return pl.pallas_call(
        _merge_kernel,
        grid=(n // tm,),
        in_specs=[pl.BlockSpec((tm, D_MODEL), lambda i: (i, 0)),
                  mix(0), mix(0), mix(U_ZA // N_HEADS), full((1, D_HEAD)),
                  mix(0), mix(U_UB // N_HEADS), mix(0),
                  pl.BlockSpec((tm, 3 * D_MODEL), lambda i: (i, 0)),
                  pl.BlockSpec((None, 6, D_MODEL), lambda i: (row(i), 0, 0)),
                  full((1, W_MIX)), full((W_MIX, W_MIX)), full((1, W_MIX)),
                  full((W_MIX, D_MODEL)), full((W_MIX, D_MODEL)), full((W_MIX, D_MODEL)),
                  full((D_MODEL, D_MODEL))],
        out_specs=pl.BlockSpec((tm, D_MODEL), lambda i: (i, 0)),
        out_shape=jax.ShapeDtypeStruct((n, D_MODEL), F32),
        compiler_params=_cparams(("parallel",)),
        name="merge",
    )(x2d, o_f, o_b, proj, lw['norm_a'], yd, proj, o_c, proj, mod6, lw['ssm_d'], lw['w_glu'], lw['b_glu'],
      lw['w_br_a'], lw['w_br_b'], lw['w_br_c'], lw['w_o'])


FFN_TM = 256
FFN_CW = 256
FFN_UP_CW = 512


def _ffn_kernel(x_ref, xr_ref, mod_ref, modr_ref, nw_ref, wup_ref, cw_ref, cb_ref, wd_ref, fn_ref, o_ref,
                hu_a, hu_b, *, seq_len, final):
    i = pl.program_id(0)
    tm = x_ref.shape[0]
    blocks_per_seq = max(seq_len // tm, 1)

    @pl.when(i == 0)
    def _():
        hu_a[...] = jnp.zeros_like(hu_a)
        hu_b[...] = jnp.zeros_like(hu_b)

    p = (i + blocks_per_seq - 1) % blocks_per_seq
    has_prev = p > 0
    has_next = p < blocks_per_seq - 1
    rowi = lax.broadcasted_iota(jnp.int32, (tm, FFN_CW), 0)

    def step(new_ref, cur_ref):
        prev_row = jnp.where(has_prev, new_ref[tm - 1:tm, :], 0.0)
        h = _rms(x_ref[...]) * nw_ref[...]
        h = (h * (1.0 + mod_ref[4:5, :]) + mod_ref[3:4, :]).astype(BF16)
        up_done = set()

        def up(q):
            if q not in up_done:
                up_done.add(q)
                cs = slice(q * FFN_UP_CW, (q + 1) * FFN_UP_CW)
                new_ref[:, cs] = jnp.dot(h, wup_ref[:, cs], preferred_element_type=F32)

        def conv(lo):
            cs = slice(lo, lo + FFN_CW)
            up(lo // FFN_UP_CW)
            up((lo + FFN_CW - 1) // FFN_UP_CW)
            next_row = jnp.where(has_next, new_ref[0:1, cs], 0.0)
            hu = cur_ref[:, cs]
            hm = jnp.where(rowi == 0, prev_row[:, cs], pltpu.roll(hu, 1, 0))
            hp = jnp.where(rowi == tm - 1, next_row, pltpu.roll(hu, tm - 1, 0))
            return cw_ref[0:1, cs] * hm + cw_ref[1:2, cs] * hu + cw_ref[2:3, cs] * hp + cb_ref[:, cs]

        acc = None
        for cb in range(D_FF // FFN_CW):
            lo = cb * FFN_CW
            act = (_silu(conv(lo)) * conv(D_FF + lo)).astype(BF16)
            part = jnp.dot(act, wd_ref[lo:lo + FFN_CW, :], preferred_element_type=F32)
            acc = part if acc is None else acc + part
        for q in range(wup_ref.shape[1] // FFN_UP_CW):
            up(q)
        x2 = xr_ref[...] + modr_ref[5:6, :] * acc
        if final:
            x2 = _rms(x2) * fn_ref[...]
        o_ref[...] = x2

    @pl.when(i % 2 == 0)
    def _():
        step(hu_a, hu_b)

    @pl.when(i % 2 == 1)
    def _():
        step(hu_b, hu_a)


def _ffn(x2d, mod6, lw, final_norm, seq_len, rows_per_cond, final):
    n = x2d.shape[0]
    tm = FFN_TM
    assert seq_len % tm == 0
    nblk = n // tm
    row = _mod_row_map(tm, rows_per_cond)
    cur = lambda i: jnp.minimum(i, nblk - 1)
    done = lambda i: jnp.maximum(i - 1, 0)
    full = lambda shape: pl.BlockSpec(shape, lambda i: (0,) * len(shape), pipeline_mode=pl.Buffered(1))
    return pl.pallas_call(
        functools.partial(_ffn_kernel, seq_len=seq_len, final=final),
        grid=(nblk + 1,),
        in_specs=[pl.BlockSpec((tm, D_MODEL), lambda i: (cur(i), 0)),
                  pl.BlockSpec((tm, D_MODEL), lambda i: (done(i), 0)),
                  pl.BlockSpec((None, 6, D_MODEL), lambda i: (row(cur(i)), 0, 0)),
                  pl.BlockSpec((None, 6, D_MODEL), lambda i: (row(done(i)), 0, 0)),
                  full((1, D_MODEL)), full((D_MODEL, 2 * D_FF)), full((3, 2 * D_FF)), full((1, 2 * D_FF)),
                  full((D_FF, D_MODEL)), full((1, D_MODEL))],
        out_specs=pl.BlockSpec((tm, D_MODEL), lambda i: (done(i), 0)),
        out_shape=jax.ShapeDtypeStruct((n, D_MODEL), F32),
        scratch_shapes=[pltpu.VMEM((tm, 2 * D_FF), F32), pltpu.VMEM((tm, 2 * D_FF), F32)],
        compiler_params=_cparams(("arbitrary",)),
        name="ffn",
    )(x2d, x2d, mod6, mod6, lw['norm2'], lw['w_up'], lw['w_conv_ffn'], lw['b_conv_ffn'], lw['w_down'],
      final_norm)


def _run_pass(x, mod, layers, consts, init_states, final_norm):
    b, seq_len, _ = x.shape
    n = b * seq_len
    is_ctx = init_states is None
    rows_per_cond = None if is_ctx else seq_len
    step_rows = 4096
    x2d = x.reshape(n, D_MODEL)
    depth = len(layers)
    finals = []
    for li, lw in enumerate(layers):
        mod6 = mod[li]
        proj, ab, u4f = _in_projection(x2d, mod6, lw['norm1'], lw['w_in_main'], lw['w_in_ab'], rows_per_cond)
        pre = _gdn_precompute(proj, ab, lw['w_conv_qkv'], lw['gate_par'], consts['gdn_masks'], consts['gdn_tri'],
                              consts['gdn_levels'], seq_len)
        o_f, o_b, sd = _gdn_scan(pre, None if is_ctx else init_states[0][:, li], seq_len)
        yd, sfin = _s5_mixer(u4f, lw['s5_mats'], None if is_ctx else init_states[1][:, li], seq_len)
        o_c, sr = _retention(proj, consts['ret_tables'], consts.get('rope'),
                             None if is_ctx else init_states[2][:, li], seq_len, step_rows)
        x1 = _merge(x2d, o_f, o_b, yd, o_c, proj, mod6, lw, rows_per_cond)
        x2d = _ffn(x1, mod6, lw, final_norm, seq_len, rows_per_cond, final=(li == depth - 1))
        finals.append((sd, sfin, sr))
    return x2d.reshape(b, seq_len, D_MODEL), finals


def _s5_state_to_rows(s_re, s_im):
    shp = s_re.shape[:3]
    re = s_re.reshape(shp + (S5_BLOCKS, 4, LANE))
    im = s_im.reshape(shp + (S5_BLOCKS, 4, LANE))
    return jnp.concatenate([re, im], axis=-2)


def _s5_rows_to_state(fin):
    nb = fin.shape[2]
    re = fin[:, :, :, 0:4, :].reshape(2, S5_BLOCKS, nb, 8, P_B)
    im = fin[:, :, :, 4:8, :].reshape(2, S5_BLOCKS, nb, 8, P_B)
    perm = lambda t: jnp.transpose(t, (2, 0, 1, 3, 4)).reshape(nb, 2, G_B, P_B)
    return perm(re), perm(im)


def kernel(x_prompt, x_sample, state_delta, state_ssm_re, state_ssm_im, state_ret, c, c_ctx,
           final_norm, norm1, norm2, w_mod, b_mod, w_in, w_conv_qkv, a_log, dt_bias, norm_a, w_br_a,
           ssm_lam_re, ssm_lam_im, ssm_log_dt, ssm_b_re, ssm_b_im, ssm_c_re, ssm_c_im, ssm_d,
           w_glu, b_glu, w_br_b, w_br_c, w_o, w_up, w_conv_ffn, b_conv_ffn, w_down):
    depth = w_in.shape[0]
    dec_b = x_sample.shape[0]
    n_qkv = 3 * W_MIX
    gate_par = jnp.pad(jnp.stack([a_log.reshape(depth, 8), dt_bias.reshape(depth, 8)], axis=1),
                       ((0, 0), (0, 6), (0, LANE - 8)))
    compact, s5_a1, s5_a2 = jax.vmap(_s5_matrices)(ssm_lam_re, ssm_lam_im, ssm_log_dt, ssm_b_re, ssm_b_im,
                                                   ssm_c_re, ssm_c_im)
    expanded = _s5_expand(compact, _s5_expand_consts())
    layers = []
    for i in range(depth):
        wi = w_in[i]
        w_main = jnp.concatenate([wi[:, -3 * D_MODEL:], wi[:, :n_qkv], wi[:, n_qkv + 16:-3 * D_MODEL]],
                                 axis=1).astype(BF16)
        w_ab = jnp.pad(wi[:, n_qkv:n_qkv + 16], ((0, 0), (0, LANE - 16))).astype(BF16)
        layers.append(dict(
            norm1=norm1[i][None], norm2=norm2[i][None], w_in_main=w_main, w_in_ab=w_ab,
            w_conv_qkv=w_conv_qkv[i], gate_par=gate_par[i], norm_a=norm_a[i][None],
            s5_mats=(expanded, i, s5_a1[i], s5_a2[i]),
            ssm_d=ssm_d[i][None], w_glu=w_glu[i].astype(BF16), b_glu=b_glu[i][None],
            w_br_a=w_br_a[i].astype(BF16), w_br_b=w_br_b[i].astype(BF16), w_br_c=w_br_c[i].astype(BF16),
            w_o=w_o[i].astype(BF16), w_up=w_up[i].astype(BF16), w_conv_ffn=w_conv_ffn[i],
            b_conv_ffn=b_conv_ffn[i][None], w_down=w_down[i].astype(BF16)))
    masks = _gdn_masks()
    consts = dict(gdn_masks=masks, gdn_tri=masks[M_INCL:M_INCL + 2].astype(BF16),
                  gdn_levels=masks[M_LEVEL + 1:M_LEVEL + 7].astype(BF16), ret_tables=_ret_tables())
    fnorm = final_norm[None]

    cond8 = jnp.concatenate([c_ctx[None], c, jnp.zeros((8 - 1 - dec_b, D_MODEL), F32)], axis=0)
    mod = _modulation(cond8, w_mod, b_mod).reshape(depth, 8, 6, D_MODEL)

    y_prompt, ctx_finals = _run_pass(x_prompt, mod, layers, consts, None, fnorm)
    consts_lat = dict(consts, rope=_rope_tables(x_sample.shape[1]))
    x0_rows = _s5_state_to_rows(state_ssm_re, state_ssm_im)
    y_sample, _ = _run_pass(x_sample, mod, layers, consts_lat, (state_delta, x0_rows, state_ret), fnorm)

    new_delta = jnp.stack([f[0] for f in ctx_finals], axis=1)
    s5 = [_s5_rows_to_state(f[1]) for f in ctx_finals]
    new_re = jnp.stack([s[0] for s in s5], axis=1)
    new_im = jnp.stack([s[1] for s in s5], axis=1)
    new_ret = jnp.stack([f[2] for f in ctx_finals], axis=1)
    return (y_prompt, y_sample, new_delta, new_re, new_im, new_ret)
```

```python
import functools
import math

import jax
import jax.numpy as jnp
import numpy as np
from jax import lax
from jax.experimental import pallas as pl
from jax.experimental.pallas import tpu as pltpu

F32 = jnp.float32
BF16 = jnp.bfloat16

D_MODEL = 1024
EPS = 1e-6
N_HEADS = 4
D_HEAD = 128
W_MIX = N_HEADS * D_HEAD
GDN_CHUNK = 64
SEG = 256
GRID_W = 64
ROPE_BASE = 10000.0
G_B, P_B, GC_B = 32, 64, 16
S5_TILE = 8
S5_BLOCKS = 4
S5_W = S5_TILE * 128
D_FF = 2816
LANE = 128
VMEM_LIMIT = 56 * 1024 * 1024

U_GATES = 0
U_QA, U_KA, U_VA, U_ZA = 24, 28, 32, 36
U_UB = 40
U_QC, U_KC, U_VC, U_GC = 44, 48, 52, 56
N_PROJ = 60 * LANE
PROJ_TN = 2560
UB_BLOCK = (U_UB * LANE) // PROJ_TN
UB_LOCAL = U_UB * LANE - UB_BLOCK * PROJ_TN
PROJ_DTYPE = BF16
HALO_ROWS = 16


def _cparams(sem):
    return pltpu.CompilerParams(dimension_semantics=sem, vmem_limit_bytes=VMEM_LIMIT)


def _dot(a, b):
    return jnp.dot(a.astype(BF16), b.astype(BF16), preferred_element_type=F32)


def _dot_nt(a, b):
    return lax.dot_general(a.astype(BF16), b.astype(BF16), (((1,), (1,)), ((), ())),
                           preferred_element_type=F32)


def _dot_tn(a, b):
    return lax.dot_general(a.astype(BF16), b.astype(BF16), (((0,), (0,)), ((), ())),
                           preferred_element_type=F32)


def _sigmoid(x):
    return jax.nn.sigmoid(x)


def _silu(x):
    return x * _sigmoid(x)


def _softplus(x):
    return jnp.maximum(x, 0.0) + jnp.log1p(jnp.exp(-jnp.abs(x)))


def _gelu_tanh(x):
    return 0.5 * x * (1.0 + jnp.tanh(math.sqrt(2.0 / math.pi) * (x + 0.044715 * x * x * x)))


def _rms(x):
    return x * lax.rsqrt(jnp.mean(x * x, axis=-1, keepdims=True) + EPS)


def _split3(x):
    hi = x.astype(BF16).astype(F32)
    r1 = x - hi
    mid = r1.astype(BF16).astype(F32)
    lo = (r1 - mid).astype(BF16).astype(F32)
    return hi, mid, lo


def _mod_kernel(c_ref, w_ref, b_ref, o_ref):
    o_ref[...] = _dot(_silu(c_ref[...]), w_ref[...]) + b_ref[...]


def _modulation(cond8, w_mod, b_mod):
    depth, _, n = w_mod.shape
    tn = 1536
    return pl.pallas_call(
        _mod_kernel,
        grid=(depth, n // tn),
        in_specs=[pl.BlockSpec((8, D_MODEL), lambda l, j: (0, 0)),
                  pl.BlockSpec((None, D_MODEL, tn), lambda l, j: (l, 0, j)),
                  pl.BlockSpec((None, 1, tn), lambda l, j: (l, 0, j))],
        out_specs=pl.BlockSpec((None, 8, tn), lambda l, j: (l, 0, j)),
        out_shape=jax.ShapeDtypeStruct((depth, 8, n), F32),
        compiler_params=_cparams(("parallel", "parallel")),
        name="modulation",
    )(cond8, w_mod, b_mod.reshape(depth, 1, n))


def _mod_row_map(tm, rows_per_cond):
    if rows_per_cond is None:
        return lambda i: 0
    return lambda i: 1 + (i * tm) // rows_per_cond


def _inproj_kernel(x_ref, mod_ref, nw_ref, w_ref, wab_ref, proj_ref, ab_ref, u4_ref, h_scr, u_scr):
    j = pl.program_id(1)
    rows8 = u4_ref.shape[1]

    @pl.when(j == 0)
    def _():
        h = _rms(x_ref[...]) * nw_ref[...]
        h = h * (1.0 + mod_ref[1:2, :]) + mod_ref[0:1, :]
        hb = h.astype(BF16)
        h_scr[...] = hb
        ab_ref[...] = jnp.dot(hb, wab_ref[...], preferred_element_type=F32)

    acc = jnp.dot(h_scr[...], w_ref[...], preferred_element_type=F32)
    proj_ref[...] = acc.astype(PROJ_DTYPE)

    @pl.when(j == UB_BLOCK)
    def _():
        for b in range(S5_BLOCKS):
            u_scr[b] = acc[:, UB_LOCAL + LANE * b:UB_LOCAL + LANE * (b + 1)]
        for b in range(S5_BLOCKS):
            for t in range(S5_TILE):
                u4_ref[b, :, LANE * t:LANE * (t + 1)] = u_scr[b, pl.ds(t, rows8, stride=S5_TILE), :]


def _in_projection(x2d, mod6, norm_w, w_main, w_ab, rows_per_cond):
    n = x2d.shape[0]
    tm, tn = 1024, PROJ_TN
    row = _mod_row_map(tm, rows_per_cond)
    return pl.pallas_call(
        _inproj_kernel,
        grid=(n // tm, N_PROJ // tn),
        in_specs=[pl.BlockSpec((tm, D_MODEL), lambda i, j: (i, 0)),
                  pl.BlockSpec((None, 6, D_MODEL), lambda i, j: (row(i), 0, 0)),
                  pl.BlockSpec((1, D_MODEL), lambda i, j: (0, 0)),
                  pl.BlockSpec((D_MODEL, tn), lambda i, j: (0, j)),
                  pl.BlockSpec((D_MODEL, LANE), lambda i, j: (0, 0))],
        out_specs=[pl.BlockSpec((tm, tn), lambda i, j: (i, j)),
                   pl.BlockSpec((tm, LANE), lambda i, j: (i, 0)),
                   pl.BlockSpec((S5_BLOCKS, tm // S5_TILE, S5_W), lambda i, j: (0, i, 0))],
        out_shape=[jax.ShapeDtypeStruct((n, N_PROJ), PROJ_DTYPE),
                   jax.ShapeDtypeStruct((n, LANE), F32),
                   jax.ShapeDtypeStruct((S5_BLOCKS, n // S5_TILE, S5_W), F32)],
        scratch_shapes=[pltpu.VMEM((tm, D_MODEL), BF16), pltpu.VMEM((S5_BLOCKS, tm, LANE), F32)],
        compiler_params=_cparams(("parallel", "arbitrary")),
        name="in_projection",
    )(x2d, mod6, norm_w, w_main, w_ab)


M_INCL, M_STRICT, M_EYE, M_LEVEL = 0, 2, 4, 4
N_MASKS = 11


def _gdn_masks():
    r = np.arange(SEG)[:, None]
    c = np.arange(SEG)[None, :]
    same = (r // GDN_CHUNK) == (c // GDN_CHUNK)
    incl = [same & (r >= c), same & (r <= c)]
    strict = [same & (r > c), same & (r < c)]
    eye = [r == c]
    levels = [((r >> k) == (c >> k)) & ((r >> (k - 1)) != (c >> (k - 1))) for k in range(1, 7)]
    return np.stack(incl + strict + eye + levels).astype(np.float32)


def _gdn_pre_kernel(*refs, segs_per_seq):
    has_halo = segs_per_seq > 1
    q_ref, k_ref, v_ref = refs[:3]
    pos = 3
    halos = None
    if has_halo:
        halos = refs[3:9]
        pos = 9
    (ab_ref, wq_ref, wk_ref, wv_ref, gp_ref, msk_ref, tri_ref, lvl_ref,
     u_ref, w_ref, qg_ref, kg_ref, qkm_ref, eg_ref) = refs[pos:]
    seg = pl.program_id(0)
    rowi = lax.broadcasted_iota(jnp.int32, (SEG, W_MIX), 0)
    lane = lax.broadcasted_iota(jnp.int32, (SEG, LANE), 1)

    def conv_act(x_ref, cw_ref, prev_ref, next_ref):
        x = x_ref[...].astype(F32)
        xm = pltpu.roll(x, 1, 0)
        xp = pltpu.roll(x, SEG - 1, 0)
        if has_halo:
            p = seg % segs_per_seq
            prev = jnp.where(p > 0, prev_ref[HALO_ROWS - 1:HALO_ROWS, :].astype(F32), 0.0)
            nxt = jnp.where(p < segs_per_seq - 1, next_ref[0:1, :].astype(F32), 0.0)
        else:
            prev = jnp.zeros((1, W_MIX), F32)
            nxt = prev
        xm = jnp.where(rowi == 0, prev, xm)
        xp = jnp.where(rowi == SEG - 1, nxt, xp)
        return _silu(cw_ref[0:1, :] * xm + cw_ref[1:2, :] * x + cw_ref[2:3, :] * xp)

    hq = halos[0:2] if has_halo else (None, None)
    hk = halos[2:4] if has_halo else (None, None)
    hv = halos[4:6] if has_halo else (None, None)
    qa_all = conv_act(q_ref, wq_ref, *hq)
    ka_all = conv_act(k_ref, wk_ref, *hk)
    va_all = conv_act(v_ref, wv_ref, *hv)

    abv = ab_ref[...]
    log_a = -jnp.exp(gp_ref[0:1, :]) * _softplus(abv + gp_ref[1:2, :])
    beta_all = _sigmoid(abv)

    def col(x, idx):
        return jnp.sum(jnp.where(lane == idx, x, 0.0), axis=1, keepdims=True)

    heads = []
    for h in range(N_HEADS):
        hs = slice(h * D_HEAD, (h + 1) * D_HEAD)
        qa, ka = qa_all[:, hs], ka_all[:, hs]
        qn = qa * lax.rsqrt(jnp.sum(qa * qa, axis=-1, keepdims=True) + EPS) * (D_HEAD ** -0.5)
        kn = ka * lax.rsqrt(jnp.sum(ka * ka, axis=-1, keepdims=True) + EPS)
        heads.append(dict(qn=qn, kn=kn, va=va_all[:, hs], kk=_dot_nt(kn, kn), qk=_dot_nt(qn, kn)))

    chains = []
    for h in range(N_HEADS):
        for d in range(2):
            hd = heads[h]
            g = jnp.broadcast_to(col(log_a, d * N_HEADS + h), (SEG, LANE))
            beta = col(beta_all, 2 * N_HEADS + d * N_HEADS + h)
            ghi, gmid, glo = _split3(g)
            pieces = jnp.where(lane == 0, ghi, jnp.where(lane == 1, gmid, jnp.where(lane == 2, glo, 0.0)))
            cs = jnp.dot(tri_ref[d], pieces.astype(BF16), preferred_element_type=F32)
            gc = jnp.broadcast_to(jnp.sum(jnp.where(lane < 3, cs, 0.0), axis=1, keepdims=True), (SEG, LANE))
            chi, cmid, clo = _split3(gc)
            a_mat = jnp.where(lane == 0, chi, jnp.where(lane == 1, cmid, jnp.where(lane == 2, clo,
                              jnp.where(lane < 6, 1.0, 0.0))))
            b_mat = jnp.where(lane < 3, 1.0, jnp.where(lane == 3, -chi, jnp.where(lane == 4, -cmid,
                              jnp.where(lane == 5, -clo, 0.0))))
            e = jnp.exp(jnp.minimum(_dot_nt(a_mat, b_mat), 0.0))
            decay = e * msk_ref[M_INCL + d]
            lmat = (beta * hd['kk']) * decay
            chains.append(dict(h=h, d=d, beta=beta, gc=gc, lmat=lmat.astype(BF16),
                               qkm=hd['qk'] * decay,
                               x=(msk_ref[M_EYE] - lmat * msk_ref[M_LEVEL + 1]).astype(BF16)))

    for lvl in range(2, 7):
        for ch in chains:
            c = ch['lmat'] * lvl_ref[lvl - 1]
            t = jnp.dot(c, ch['x'], preferred_element_type=F32).astype(BF16)
            ch['x'] = ch['x'] - jnp.dot(ch['x'], t, preferred_element_type=F32).astype(BF16)

    for ch in chains:
        h, d, beta, gc = ch['h'], ch['d'], ch['beta'], ch['gc']
        hd = heads[h]
        qn, kn, va = hd['qn'], hd['kn'], hd['va']
        hs = slice(h * D_HEAD, (h + 1) * D_HEAD)
        egc = jnp.exp(gc)
        rhs = jnp.concatenate([va * beta, kn * (beta * egc)], axis=1).astype(BF16)
        uw = jnp.dot(ch['x'], rhs, preferred_element_type=F32)
        u_ref[d, :, hs] = uw[:, :D_HEAD]
        w_ref[d, :, hs] = uw[:, D_HEAD:].astype(BF16)
        qg_ref[d, :, hs] = (qn * egc).astype(BF16)
        edge = GDN_CHUNK - 1 if d == 0 else 0
        for c4 in range(SEG // GDN_CHUNK):
            lo_r = c4 * GDN_CHUNK
            rs = slice(lo_r, lo_r + GDN_CHUNK)
            g_edge = gc[lo_r + edge:lo_r + edge + 1, :]
            kg_ref[d, rs, hs] = (kn[rs, :] * jnp.exp(g_edge - gc[rs, :])).astype(BF16)
            eg_ref[d, c4, h:h + 1, :] = jnp.exp(g_edge)
            qkm_ref[d, h, rs, :] = ch['qkm'][rs, rs].astype(BF16)


def _gdn_precompute(proj, ab, wconv, gate_par, masks, tri, levels, seq_len):
    n = proj.shape[0]
    segs_per_seq = seq_len // SEG
    nseg = n // SEG
    units = (U_QA // N_HEADS, U_KA // N_HEADS, U_VA // N_HEADS)
    in_specs = [pl.BlockSpec((SEG, W_MIX), lambda s, cb=cb: (s, cb)) for cb in units]
    args = [proj, proj, proj]
    if segs_per_seq > 1:
        r8 = SEG // HALO_ROWS
        last8 = n // HALO_ROWS - 1
        for cb in units:
            in_specs.append(pl.BlockSpec((HALO_ROWS, W_MIX), lambda s, cb=cb: (jnp.maximum(s * r8 - 1, 0), cb)))
            in_specs.append(pl.BlockSpec((HALO_ROWS, W_MIX),
                                         lambda s, cb=cb: (jnp.minimum((s + 1) * r8, last8), cb)))
            args += [proj, proj]
    in_specs += [pl.BlockSpec((SEG, LANE), lambda s: (s, 0))]
    in_specs += [pl.BlockSpec((3, W_MIX), lambda s, o=o: (0, o)) for o in range(3)]
    in_specs += [pl.BlockSpec((8, LANE), lambda s: (0, 0)),
                 pl.BlockSpec((N_MASKS, SEG, SEG), lambda s: (0, 0, 0)),
                 pl.BlockSpec((2, SEG, SEG), lambda s: (0, 0, 0)),
                 pl.BlockSpec((6, SEG, SEG), lambda s: (0, 0, 0))]
    args += [ab, wconv, wconv, wconv, gate_par, masks, tri, levels]
    big = pl.BlockSpec((2, SEG, W_MIX), lambda s: (0, s, 0))
    cpseg = SEG // GDN_CHUNK
    return pl.pallas_call(
        functools.partial(_gdn_pre_kernel, segs_per_seq=segs_per_seq),
        grid=(nseg,),
        in_specs=in_specs,
        out_specs=[big, big, big, big,
                   pl.BlockSpec((2, N_HEADS, SEG, GDN_CHUNK), lambda s: (0, 0, s, 0)),
                   pl.BlockSpec((2, cpseg, N_HEADS, LANE), lambda s: (0, s, 0, 0))],
        out_shape=[jax.ShapeDtypeStruct((2, n, W_MIX), F32),
                   jax.ShapeDtypeStruct((2, n, W_MIX), BF16),
                   jax.ShapeDtypeStruct((2, n, W_MIX), BF16),
                   jax.ShapeDtypeStruct((2, n, W_MIX), BF16),
                   jax.ShapeDtypeStruct((2, N_HEADS, n, GDN_CHUNK), BF16),
                   jax.ShapeDtypeStruct((2, n // GDN_CHUNK, N_HEADS, LANE), F32)],
        compiler_params=_cparams(("parallel",)),
        name="gdn_precompute",
    )(*args)


GDN_SCAN_ROWS = 1024


def _gdn_scan_kernel(*refs, chunks_per_seq, has_init):
    ins = refs[:12]
    dir_refs = [ins[0::2], ins[1::2]]
    if has_init:
        s0_ref, of_ref, ob_ref, s_scr = refs[12:]
        fin_refs = None
    else:
        of_ref, ob_ref, sff_ref, sfb_ref, s_scr = refs[12:]
        fin_refs = (sff_ref, sfb_ref)
    o_refs = (of_ref, ob_ref)
    nchunk = of_ref.shape[0] // GDN_CHUNK
    cps = chunks_per_seq

    @pl.when(pl.program_id(1) == 0)
    def _():
        if has_init:
            s_scr[...] = s0_ref[...]
        else:
            s_scr[...] = jnp.zeros_like(s_scr)

    def body(i, carry):
        chains = []
        for h in range(N_HEADS):
            for d in range(2):
                c = i if d == 0 else nchunk - 1 - i
                chains.append(dict(h=h, d=d, c=c, hs=slice(h * D_HEAD, (h + 1) * D_HEAD),
                                   rsl=pl.ds(pl.multiple_of(c * GDN_CHUNK, GDN_CHUNK), GDN_CHUNK)))
        for ch in chains:
            s = s_scr[ch['d'], ch['h']]
            if not has_init:
                s = jnp.where(i % cps == 0, 0.0, s)
            ch['s'] = s
            ch['sb'] = s.astype(BF16)
        for ch in chains:
            u_ref, w_ref, qg_ref = dir_refs[ch['d']][:3]
            rsl, hs = ch['rsl'], ch['hs']
            both = jnp.dot(jnp.concatenate([w_ref[rsl, hs], qg_ref[rsl, hs]], axis=0), ch['sb'],
                           preferred_element_type=F32)
            ch['vb'] = (u_ref[rsl, hs] - both[:GDN_CHUNK]).astype(BF16)
            ch['o'] = both[GDN_CHUNK:]
        for ch in chains:
            kg_ref, qkm_ref, eg_ref = dir_refs[ch['d']][3:]
            rsl, hs, h = ch['rsl'], ch['hs'], ch['h']
            ch['o'] = ch['o'] + jnp.dot(qkm_ref[h, rsl, :], ch['vb'], preferred_element_type=F32)
            ch['s_new'] = ch['s'] * eg_ref[ch['c']][h:h + 1, :] + _dot_tn(kg_ref[rsl, hs], ch['vb'])
        for ch in chains:
            s_scr[ch['d'], ch['h']] = ch['s_new']
            o_refs[ch['d']][ch['rsl'], ch['hs']] = ch['o']
        if not has_init:
            @pl.when(i % cps == cps - 1)
            def _():
                for ch in chains:
                    fin_refs[ch['d']][ch['c'] // cps, ch['h']] = ch['s_new']
        return carry

    lax.fori_loop(0, nchunk, body, 0)


def _gdn_scan(pre, s0, seq_len):
    n = pre[0].shape[1]
    br = GDN_SCAN_ROWS
    has_init = s0 is not None
    group_rows = seq_len if has_init else n
    ngroups, nblk = n // group_rows, group_rows // br
    fwd = lambda g, j: g * nblk + j
    bwd = lambda g, j: g * nblk + (nblk - 1 - j)
    in_specs, args = [], []
    for arr, kind in zip(pre, ("row", "row", "row", "row", "qkm", "eg")):
        for d, pos in ((0, fwd), (1, bwd)):
            if kind == "row":
                in_specs.append(pl.BlockSpec((None, br, W_MIX), lambda g, j, d=d, pos=pos: (d, pos(g, j), 0)))
            elif kind == "qkm":
                in_specs.append(pl.BlockSpec((None, N_HEADS, br, GDN_CHUNK),
                                             lambda g, j, d=d, pos=pos: (d, 0, pos(g, j), 0)))
            else:
                in_specs.append(pl.BlockSpec((None, br // GDN_CHUNK, N_HEADS, LANE),
                                             lambda g, j, d=d, pos=pos: (d, pos(g, j), 0, 0)))
            args.append(arr)
    out_specs = [pl.BlockSpec((br, W_MIX), lambda g, j: (fwd(g, j), 0)),
                 pl.BlockSpec((br, W_MIX), lambda g, j: (bwd(g, j), 0))]
    out_shape = [jax.ShapeDtypeStruct((n, W_MIX), F32), jax.ShapeDtypeStruct((n, W_MIX), F32)]
    if has_init:
        in_specs.append(pl.BlockSpec((None, 2, N_HEADS, D_HEAD, D_HEAD), lambda g, j: (g, 0, 0, 0, 0)))
        args.append(s0)
    else:
        spb = br // seq_len
        fin_shape = jax.ShapeDtypeStruct((n // seq_len, N_HEADS, D_HEAD, D_HEAD), F32)
        out_specs += [pl.BlockSpec((spb, N_HEADS, D_HEAD, D_HEAD), lambda g, j: (fwd(g, j), 0, 0, 0)),
                      pl.BlockSpec((spb, N_HEADS, D_HEAD, D_HEAD), lambda g, j: (bwd(g, j), 0, 0, 0))]
        out_shape += [fin_shape, fin_shape]
    res = pl.pallas_call(
        functools.partial(_gdn_scan_kernel, chunks_per_seq=seq_len // GDN_CHUNK, has_init=has_init),
        grid=(ngroups, nblk),
        in_specs=in_specs, out_specs=out_specs, out_shape=out_shape,
        scratch_shapes=[pltpu.VMEM((2, N_HEADS, D_HEAD, D_HEAD), F32)],
        compiler_params=_cparams(("parallel", "arbitrary")),
        name="gdn_scan",
    )(*args)
    fin = None if has_init else jnp.stack([res[2], res[3]], axis=1)
    return res[0], res[1], fin


def _s5_matrices(lam_re, lam_im, log_dt, b_re, b_im, c_re, c_im):
    t = S5_TILE
    step = jnp.exp(log_dt)[:, :, None]
    lr, li = lam_re * step, lam_im * step
    js = jnp.arange(t + 1, dtype=F32)[:, None, None, None]
    mag = jnp.exp(lr[None] * js)
    pw_re, pw_im = mag * jnp.cos(li[None] * js), mag * jnp.sin(li[None] * js)
    a_re, a_im = pw_re[1], pw_im[1]
    den = lam_re * lam_re + lam_im * lam_im
    z_re = ((a_re - 1.0) * lam_re + a_im * lam_im) / den
    z_im = (a_im * lam_re - (a_re - 1.0) * lam_im) / den
    zb_re = z_re[..., None] * b_re[None] - z_im[..., None] * b_im[None]
    zb_im = z_re[..., None] * b_im[None] + z_im[..., None] * b_re[None]
    w_re = pw_re[..., None] * zb_re[None] - pw_im[..., None] * zb_im[None]
    w_im = pw_re[..., None] * zb_im[None] + pw_im[..., None] * zb_re[None]
    taps = (jnp.einsum('gcp,jdgpi->jdgic', c_re, w_re[:t]) - jnp.einsum('gcp,jdgpi->jdgic', c_im, w_im[:t]))
    nb = S5_BLOCKS
    hi = lax.Precision.HIGHEST

    def blocked(x, axis):
        return x.reshape(x.shape[:axis] + (nb, 8) + x.shape[axis + 1:])

    ss, tt, jj = np.meshgrid(np.arange(t), np.arange(t), np.arange(t), indexing='ij')
    sel_tap = np.stack([tt - ss == jj, ss - tt == jj]).astype(np.float32)
    s2, e2 = np.meshgrid(np.arange(t), np.arange(t + 1), indexing='ij')
    sel_in = np.stack([e2 == t - 1 - s2, e2 == s2]).astype(np.float32)
    sel_out = np.stack([e2 == s2 + 1, e2 == t - s2]).astype(np.float32)

    kst = blocked(jnp.einsum('dstj,jdgic->stgic', sel_tap, taps, precision=hi), 2)
    taps_sum = jnp.transpose(kst, (2, 0, 3, 4, 1, 5)).reshape(nb, S5_W, LANE)
    ws = jnp.stack([jnp.einsum('dse,edgpc->dsgpc', sel_in, w_re, precision=hi),
                    jnp.einsum('dse,edgpc->dsgpc', sel_in, w_im, precision=hi)], axis=3)
    gm = jnp.transpose(blocked(ws, 2), (0, 2, 1, 3, 6, 4, 5)).reshape(2, nb, S5_W, LANE)
    pr = jnp.einsum('dte,edgp->dtgp', sel_out, pw_re, precision=hi)
    pi = jnp.einsum('dte,edgp->dtgp', sel_out, pw_im, precision=hi)
    wo_re = c_re * pr[:, :, :, None, :] - c_im * pi[:, :, :, None, :]
    wo_im = c_re * pi[:, :, :, None, :] + c_im * pr[:, :, :, None, :]
    wo = blocked(jnp.stack([wo_re, -wo_im], axis=1), 3)
    cm = jnp.transpose(wo, (0, 3, 1, 4, 6, 2, 5)).reshape(2, nb, S5_W, LANE)
    a8r = pw_re[t].reshape(2, nb, 4, LANE)
    a8i = pw_im[t].reshape(2, nb, 4, LANE)
    a1 = jnp.concatenate([a8r, a8r], axis=2)
    a2 = jnp.concatenate([-a8i, a8i], axis=2)
    taps_both = jnp.stack([taps_sum, jnp.zeros_like(taps_sum)])
    return jnp.stack([taps_both, gm, cm]), a1, a2


def _s5_expand_consts():
    col = np.arange(S5_W)
    src = np.arange(LANE)
    ex_tc = (src[:, None] == ((col // LANE) * GC_B + col % GC_B)[None, :])
    ex_rp = (src[:, None] == ((col // (S5_W // 2)) * P_B + col % P_B)[None, :])
    g_tc = (col // GC_B) % 8
    g_rp = (col // P_B) % 8
    masks = [g_tc[:, None] == g_tc[None, :], g_tc[:, None] == g_rp[None, :], g_rp[:, None] == g_tc[None, :]]
    return (np.stack([ex_tc, ex_rp, ex_tc]).astype(BF16), np.stack(masks).astype(BF16))


def _s5_expand_kernel(c_ref, ex_ref, m_ref, o_ref):
    full = jnp.dot(c_ref[...].astype(BF16), ex_ref[...], preferred_element_type=F32)
    o_ref[...] = (full * m_ref[...].astype(F32)).astype(BF16)


def _s5_expand(compact, consts):
    ex, masks = consts
    nl, _, _, nb = compact.shape[:4]
    return pl.pallas_call(
        _s5_expand_kernel,
        grid=(3, nl, 2, nb),
        in_specs=[pl.BlockSpec((None, None, None, None, S5_W, LANE), lambda k, l, d, b: (l, k, d, b, 0, 0)),
                  pl.BlockSpec((None, LANE, S5_W), lambda k, l, d, b: (k, 0, 0)),
                  pl.BlockSpec((None, S5_W, S5_W), lambda k, l, d, b: (k, 0, 0))],
        out_specs=pl.BlockSpec((None, None, None, None, S5_W, S5_W), lambda k, l, d, b: (l, k, d, b, 0, 0)),
        out_shape=jax.ShapeDtypeStruct(compact.shape[:4] + (S5_W, S5_W), BF16),
        compiler_params=_cparams(("parallel",) * 4),
        name="s5_expand",
    )(compact, ex, masks)


def _s5_in_kernel(u_ref, gm_ref, g_ref):
    rows = u_ref.shape[0]
    g = _dot(u_ref[...], gm_ref[...])
    for s in range(8):
        g_ref[pl.ds(s, rows, stride=8), :] = g[:, LANE * s:LANE * (s + 1)]


S5_KIND_TAP, S5_KIND_IN, S5_KIND_OUT = 0, 1, 2


def _s5_tile_inputs(u4f, mats, layer):
    nb, rows, _ = u4f.shape
    return pl.pallas_call(
        _s5_in_kernel,
        grid=(nb, 2),
        in_specs=[pl.BlockSpec((None, rows, S5_W), lambda b, d: (b, 0, 0)),
                  pl.BlockSpec((None, None, None, None, S5_W, S5_W),
                               lambda b, d: (layer, S5_KIND_IN, d, b, 0, 0))],
        out_specs=pl.BlockSpec((None, None, rows * 8, LANE), lambda b, d: (d, b, 0, 0)),
        out_shape=jax.ShapeDtypeStruct((2, nb, rows * 8, LANE), F32),
        compiler_params=_cparams(("parallel", "parallel")),
        name="s5_tile_inputs",
    )(u4f, mats)


def _s5_scan_kernel(*refs, nseq, rows_per_seq, has_init):
    g_ref, a1_ref, a2_ref = refs[:3]
    if has_init:
        x0_ref, xp_ref = refs[3:]
        fin_ref = None
    else:
        xp_ref, fin_ref = refs[3:]
    nblk = g_ref.shape[0]
    backward = pl.program_id(0) == 1
    chains = [(b, s) for b in range(nblk) for s in range(nseq)]

    def body(i, xs):
        n = jnp.where(backward, rows_per_seq - 1 - i, i)
        out = []
        for (b, s), x in zip(chains, xs):
            r = s * rows_per_seq + n
            xp_ref[b, r] = x
            out.append(a1_ref[b] * x + a2_ref[b] * pltpu.roll(x, 4, 0) + g_ref[b, r])
        return tuple(out)

    if has_init:
        init = tuple(x0_ref[b] for (b, s) in chains)
    else:
        init = tuple(jnp.zeros((8, LANE), F32) for _ in chains)
    xs = lax.fori_loop(0, rows_per_seq, body, init)
    if not has_init:
        for (b, s), x in zip(chains, xs):
            fin_ref[s] = x


def _s5_scan(g5, a1, a2, x0, seq_rows):
    _, nb, rows, _, _ = g5.shape
    nseq = rows // seq_rows
    if x0 is None:
        return pl.pallas_call(
            functools.partial(_s5_scan_kernel, nseq=nseq, rows_per_seq=seq_rows, has_init=False),
            grid=(2, nb),
            in_specs=[pl.BlockSpec((None, 1, rows, 8, LANE), lambda d, b: (d, b, 0, 0, 0)),
                      pl.BlockSpec((None, 1, 8, LANE), lambda d, b: (d, b, 0, 0)),
                      pl.BlockSpec((None, 1, 8, LANE), lambda d, b: (d, b, 0, 0))],
            out_specs=[pl.BlockSpec((None, 1, rows, 8, LANE), lambda d, b: (d, b, 0, 0, 0)),
                       pl.BlockSpec((None, None, nseq, 8, LANE), lambda d, b: (d, b, 0, 0, 0))],
            out_shape=[jax.ShapeDtypeStruct(g5.shape, F32),
                       jax.ShapeDtypeStruct((2, nb, nseq, 8, LANE), F32)],
            compiler_params=_cparams(("parallel", "parallel")),
            name="s5_scan_ctx",
        )(g5, a1, a2)
    xprev = pl.pallas_call(
        functools.partial(_s5_scan_kernel, nseq=1, rows_per_seq=seq_rows, has_init=True),
        grid=(2, nseq),
        in_specs=[pl.BlockSpec((None, nb, seq_rows, 8, LANE), lambda d, s: (d, 0, s, 0, 0)),
                  pl.BlockSpec((None, nb, 8, LANE), lambda d, s: (d, 0, 0, 0)),
                  pl.BlockSpec((None, nb, 8, LANE), lambda d, s: (d, 0, 0, 0)),
                  pl.BlockSpec((None, None, nb, 8, LANE), lambda d, s: (s, d, 0, 0, 0))],
        out_specs=pl.BlockSpec((None, nb, seq_rows, 8, LANE), lambda d, s: (d, 0, s, 0, 0)),
        out_shape=jax.ShapeDtypeStruct(g5.shape, F32),
        compiler_params=_cparams(("parallel", "parallel")),
        name="s5_scan_latent",
    )(g5, a1, a2, x0)
    return xprev, None


def _s5_out_kernel(u_ref, xp_ref, tm_ref, cm_ref, y_ref):
    tr = u_ref.shape[0]
    acc = jnp.dot(u_ref[...].astype(BF16), tm_ref[...], preferred_element_type=F32)
    for d in range(2):
        xp = jnp.concatenate([xp_ref[d, pl.ds(s, tr, stride=8), :] for s in range(8)], axis=1)
        acc += _dot(xp, cm_ref[d])
    for t in range(S5_TILE):
        y_ref[pl.ds(t, tr, stride=S5_TILE), :] = acc[:, LANE * t:LANE * (t + 1)]


def _s5_outputs(u4f, xprev, mats, layer):
    nb, rows, _ = u4f.shape
    tr = 512
    return pl.pallas_call(
        _s5_out_kernel,
        grid=(nb, rows // tr),
        in_specs=[pl.BlockSpec((None, tr, S5_W), lambda b, r: (b, r, 0)),
                  pl.BlockSpec((2, None, tr * 8, LANE), lambda b, r: (0, b, r, 0)),
                  pl.BlockSpec((None, None, None, None, S5_W, S5_W),
                               lambda b, r: (layer, S5_KIND_TAP, 0, b, 0, 0)),
                  pl.BlockSpec((None, None, 2, None, S5_W, S5_W),
                               lambda b, r: (layer, S5_KIND_OUT, 0, b, 0, 0))],
        out_specs=pl.BlockSpec((tr * S5_TILE, LANE), lambda b, r: (r, b)),
        out_shape=jax.ShapeDtypeStruct((rows * S5_TILE, S5_BLOCKS * LANE), F32),
        compiler_params=_cparams(("parallel", "parallel")),
        name="s5_outputs",
    )(u4f, xprev, mats, mats)


def _s5_mixer(u4f, s5, x0, seq_len):
    mats, layer, a1, a2 = s5
    nb, rows, _ = u4f.shape
    g = _s5_tile_inputs(u4f, mats, layer)
    xprev, fin = _s5_scan(g.reshape(2, nb, rows, 8, LANE), a1, a2, x0, seq_len // S5_TILE)
    yd = _s5_outputs(u4f, xprev.reshape(2, nb, rows * 8, LANE), mats, layer)
    return yd, fin


def _ret_tables():
    h = np.arange(N_HEADS, dtype=np.float64)
    lg = np.log1p(-np.exp2(-5.0 - h))
    lgf, lgb = lg[:, None], lg[::-1][:, None]
    i = np.arange(SEG, dtype=np.float64)
    dist = i[:, None] - i[None, :]
    d_f = np.where(dist >= 0, np.exp(lgf[:, :, None] * np.maximum(dist, 0.0)), 0.0)
    d_b = np.where(dist <= 0, np.exp(lgb[:, :, None] * np.maximum(-dist, 0.0)), 0.0)
    dsum = d_f + d_b
    dec = np.stack([np.exp(lgf * (i + 1.0)), np.exp(lgf * (SEG - 1.0 - i)),
                    np.exp(lgb * (SEG - i)), np.exp(lgb * i)], axis=1)
    dec = np.broadcast_to(dec[..., None], (N_HEADS, 4, SEG, LANE))
    cd = np.stack([np.exp(lgf[:, 0] * SEG), np.exp(lgb[:, 0] * SEG)], axis=1)
    cd = np.broadcast_to(np.pad(cd, ((0, 0), (0, 6)))[..., None], (N_HEADS, 8, LANE))
    f32 = lambda a: np.ascontiguousarray(a, dtype=np.float32)
    return f32(dsum), f32(dec), f32(cd)


def _rope_tables(n_tokens):
    n_rows = n_tokens // GRID_W
    rows = np.repeat(np.arange(n_rows), GRID_W).astype(np.float32)
    cols = np.tile(np.arange(GRID_W), n_rows).astype(np.float32)
    pairs = D_HEAD // 4
    freqs = (np.float32(ROPE_BASE) ** (-np.arange(pairs, dtype=np.float32) / np.float32(pairs))).astype(np.float32)
    ang = np.concatenate([rows[:, None] * freqs, cols[:, None] * freqs], axis=-1).astype(np.float64)
    cos, sin = np.cos(ang).astype(np.float32), np.sin(ang).astype(np.float32)
    return np.concatenate([cos, cos], axis=-1), np.concatenate([-sin, sin], axis=-1)


def _ret_kernel(*refs, is_ctx):
    q_ref, k_ref, v_ref, g_ref, dsum_ref, dec_ref, cd_ref = refs[:7]
    if is_ctx:
        o_ref, sfin_ref = refs[7:]
    else:
        c2_ref, s2_ref, s0_ref, o_ref, sbin_scr = refs[7:]
    rows = q_ref.shape[0]
    nchunk = rows // SEG
    scale = D_HEAD ** -0.5

    def chunk_rows(c):
        return pl.ds(pl.multiple_of(c * SEG, SEG), SEG)

    heads = range(RET_GROUP)
    hsl = [slice(j * D_HEAD, (j + 1) * D_HEAD) for j in heads]

    def load_qk(rsl, j):
        q, k = q_ref[rsl, hsl[j]].astype(F32), k_ref[rsl, hsl[j]].astype(F32)
        if not is_ctx:
            c2, s2 = c2_ref[rsl, :], s2_ref[rsl, :]
            q = q * c2 + pltpu.roll(q, D_HEAD // 2, 1) * s2
            k = k * c2 + pltpu.roll(k, D_HEAD // 2, 1) * s2
        return q, k * scale

    def finish(rsl, j, r):
        o_ref[rsl, hsl[j]] = _rms(r) * _silu(g_ref[rsl, hsl[j]].astype(F32))

    if is_ctx:
        def body(c, carry):
            rsl = chunk_rows(c)
            qk = [load_qk(rsl, j) for j in heads]
            vs = [v_ref[rsl, hsl[j]] for j in heads]
            att = [_dot_nt(qk[j][0], qk[j][1]) * dsum_ref[j] for j in heads]
            inner = [_dot(att[j], vs[j]) for j in heads]
            for j in heads:
                finish(rsl, j, inner[j])
                sfin_ref[c, 0, j] = _dot_tn(qk[j][1] * dec_ref[j, 1], vs[j])
                sfin_ref[c, 1, j] = _dot_tn(qk[j][1] * dec_ref[j, 3], vs[j])
            return carry
        lax.fori_loop(0, nchunk, body, 0, unroll=2)
    else:
        def back(i, s_b):
            c = nchunk - 1 - i
            rsl = chunk_rows(c)
            ks = [load_qk(rsl, j)[1] for j in heads]
            sbin_scr[c] = jnp.stack(s_b)
            return tuple(s_b[j] * cd_ref[j, 1:2, :] + _dot_tn(ks[j] * dec_ref[j, 3], v_ref[rsl, hsl[j]])
                         for j in heads)
        lax.fori_loop(0, nchunk, back, tuple(s0_ref[1, j] for j in heads), unroll=2)

        def fwd(c, s_f):
            rsl = chunk_rows(c)
            qk = [load_qk(rsl, j) for j in heads]
            vs = [v_ref[rsl, hsl[j]] for j in heads]
            att = [_dot_nt(qk[j][0], qk[j][1]) * dsum_ref[j] for j in heads]
            cross = [_dot(qk[j][0] * dec_ref[j, 0], s_f[j]) + _dot(qk[j][0] * dec_ref[j, 2], sbin_scr[c, j])
                     for j in heads]
            inner = [_dot(att[j], vs[j]) for j in heads]
            for j in heads:
                finish(rsl, j, inner[j] + cross[j])
            return tuple(s_f[j] * cd_ref[j, 0:1, :] + _dot_tn(qk[j][1] * dec_ref[j, 1], vs[j]) for j in heads)
        lax.fori_loop(0, nchunk, fwd, tuple(s0_ref[0, j] for j in heads), unroll=2)


RET_GROUP = 2


def _retention(proj, tables, rope, s0, seq_len, step_rows):
    dsum, dec, cd = tables
    n = proj.shape[0]
    is_ctx = s0 is None
    gw = RET_GROUP * D_HEAD
    blk = lambda unit: pl.BlockSpec((step_rows, gw), lambda s, h, unit=unit: (s, unit // RET_GROUP + h))
    in_specs = [blk(U_QC), blk(U_KC), blk(U_VC), blk(U_GC),
                pl.BlockSpec((RET_GROUP, SEG, SEG), lambda s, h: (h, 0, 0)),
                pl.BlockSpec((RET_GROUP, 4, SEG, LANE), lambda s, h: (h, 0, 0, 0)),
                pl.BlockSpec((RET_GROUP, 8, LANE), lambda s, h: (h, 0, 0))]
    args = [proj, proj, proj, proj, dsum, dec, cd]
    out_specs = [pl.BlockSpec((step_rows, gw), lambda s, h: (s, h))]
    out_shape = [jax.ShapeDtypeStruct((n, W_MIX), F32)]
    scratch = []
    if is_ctx:
        assert seq_len == SEG
        out_specs.append(pl.BlockSpec((step_rows // SEG, 2, RET_GROUP, D_HEAD, D_HEAD),
                                      lambda s, h: (s, 0, h, 0, 0)))
        out_shape.append(jax.ShapeDtypeStruct((n // SEG, 2, N_HEADS, D_HEAD, D_HEAD), F32))
    else:
        assert step_rows == seq_len
        in_specs += [pl.BlockSpec((step_rows, LANE), lambda s, h: (0, 0)),
                     pl.BlockSpec((step_rows, LANE), lambda s, h: (0, 0)),
                     pl.BlockSpec((None, 2, RET_GROUP, D_HEAD, D_HEAD), lambda s, h: (s, 0, h, 0, 0))]
        args += [rope[0], rope[1], s0]
        scratch = [pltpu.VMEM((step_rows // SEG, RET_GROUP, D_HEAD, D_HEAD), F32)]
    res = pl.pallas_call(
        functools.partial(_ret_kernel, is_ctx=is_ctx),
        grid=(n // step_rows, N_HEADS // RET_GROUP),
        in_specs=in_specs, out_specs=out_specs, out_shape=out_shape, scratch_shapes=scratch,
        compiler_params=_cparams(("parallel", "parallel")),
        name="retention",
    )(*args)
    return res[0], (res[1] if is_ctx else None)


def _merge_kernel(x_ref, of_ref, ob_ref, z_ref, na_ref, yd_ref, u_ref, oc_ref, gates_ref, mod_ref, d_ref,
                  wglu_ref, bglu_ref, wa_ref, wb_ref, wc_ref, wo_ref, o_ref):
    tot = of_ref[...] + ob_ref[...]
    o_a = jnp.concatenate([_rms(tot[:, h * D_HEAD:(h + 1) * D_HEAD]) * na_ref[...] for h in range(N_HEADS)],
                          axis=1) * _silu(z_ref[...].astype(F32))
    y = _gelu_tanh(u_ref[...].astype(F32) * d_ref[...] + yd_ref[...])
    y = y * _sigmoid(_dot(y, wglu_ref[...]) + bglu_ref[...])
    gt = _sigmoid(gates_ref[...].astype(F32))
    merged = (gt[:, :D_MODEL] * _dot(o_a, wa_ref[...])
              + gt[:, D_MODEL:2 * D_MODEL] * _dot(y, wb_ref[...])
              + gt[:, 2 * D_MODEL:] * _dot(oc_ref[...], wc_ref[...]))
    o_ref[...] = x_ref[...] + mod_ref[2:3, :] * _dot(merged, wo_ref[...])


def _merge(x2d, o_f, o_b, yd, o_c, proj, mod6, lw, rows_per_cond):
    n = x2d.shape[0]
    tm = 512
    row = _mod_row_map(tm, rows_per_cond)
    full = lambda shape: pl.BlockSpec(shape, lambda i: (0,) * len(shape), pipeline_mode=pl.Buffered(1))
    mix = lambda cb: pl.BlockSpec((tm, W_MIX), lambda i, cb=cb: (i, cb))
    return pl.pallas_call(
        _merge_kernel,
        grid=(n // tm,),
        in_specs=[pl.BlockSpec((tm, D_MODEL), lambda i: (i, 0)),
                  mix(0), mix(0), mix(U_ZA // N_HEADS), full((1, D_HEAD)),
                  mix(0), mix(U_UB // N_HEADS), mix(0),
                  pl.BlockSpec((tm, 3 * D_MODEL), lambda i: (i, 0)),
                  pl.BlockSpec((None, 6, D_MODEL), lambda i: (row(i), 0, 0)),
                  full((1, W_MIX)), full((W_MIX, W_MIX)), full((1, W_MIX)),
                  full((W_MIX, D_MODEL)), full((W_MIX, D_MODEL)), full((W_MIX, D_MODEL)),
                  full((D_MODEL, D_MODEL))],
        out_specs=pl.BlockSpec((tm, D_MODEL), lambda i: (i, 0)),
        out_shape=jax.ShapeDtypeStruct((n, D_MODEL), F32),
        compiler_params=_cparams(("parallel",)),
        name="merge",
    )(x2d, o_f, o_b, proj, lw['norm_a'], yd, proj, o_c, proj, mod6, lw['ssm_d'], lw['w_glu'], lw['b_glu'],
      lw['w_br_a'], lw['w_br_b'], lw['w_br_c'], lw['w_o'])


FFN_TM = 256
FFN_CW = 256
FFN_UP_CW = 512


def _ffn_kernel(x_ref, xr_ref, mod_ref, modr_ref, nw_ref, wup_ref, cw_ref, cb_ref, wd_ref, fn_ref, o_ref,
                hu_a, hu_b, *, seq_len, final):
    i = pl.program_id(0)
    tm = x_ref.shape[0]
    blocks_per_seq = max(seq_len // tm, 1)

    @pl.when(i == 0)
    def _():
        hu_a[...] = jnp.zeros_like(hu_a)
        hu_b[...] = jnp.zeros_like(hu_b)

    p = (i + blocks_per_seq - 1) % blocks_per_seq
    has_prev = p > 0
    has_next = p < blocks_per_seq - 1
    rowi = lax.broadcasted_iota(jnp.int32, (tm, FFN_CW), 0)

    def step(new_ref, cur_ref):
        prev_row = jnp.where(has_prev, new_ref[tm - 1:tm, :], 0.0)
        h = _rms(x_ref[...]) * nw_ref[...]
        h = (h * (1.0 + mod_ref[4:5, :]) + mod_ref[3:4, :]).astype(BF16)
        up_done = set()

        def up(q):
            if q not in up_done:
                up_done.add(q)
                cs = slice(q * FFN_UP_CW, (q + 1) * FFN_UP_CW)
                new_ref[:, cs] = jnp.dot(h, wup_ref[:, cs], preferred_element_type=F32)

        def conv(lo):
            cs = slice(lo, lo + FFN_CW)
            up(lo // FFN_UP_CW)
            up((lo + FFN_CW - 1) // FFN_UP_CW)
            next_row = jnp.where(has_next, new_ref[0:1, cs], 0.0)
            hu = cur_ref[:, cs]
            hm = jnp.where(rowi == 0, prev_row[:, cs], pltpu.roll(hu, 1, 0))
            hp = jnp.where(rowi == tm - 1, next_row, pltpu.roll(hu, tm - 1, 0))
            return cw_ref[0:1, cs] * hm + cw_ref[1:2, cs] * hu + cw_ref[2:3, cs] * hp + cb_ref[:, cs]

        acc = None
        for cb in range(D_FF // FFN_CW):
            lo = cb * FFN_CW
            act = (_silu(conv(lo)) * conv(D_FF + lo)).astype(BF16)
            part = jnp.dot(act, wd_ref[lo:lo + FFN_CW, :], preferred_element_type=F32)
            acc = part if acc is None else acc + part
        for q in range(wup_ref.shape[1] // FFN_UP_CW):
            up(q)
        x2 = xr_ref[...] + modr_ref[5:6, :] * acc
        if final:
            x2 = _rms(x2) * fn_ref[...]
        o_ref[...] = x2

    @pl.when(i % 2 == 0)
    def _():
        step(hu_a, hu_b)

    @pl.when(i % 2 == 1)
    def _():
        step(hu_b, hu_a)


def _ffn(x2d, mod6, lw, final_norm, seq_len, rows_per_cond, final):
    n = x2d.shape[0]
    tm = FFN_TM
    assert seq_len % tm == 0
    nblk = n // tm
    row = _mod_row_map(tm, rows_per_cond)
    cur = lambda i: jnp.minimum(i, nblk - 1)
    done = lambda i: jnp.maximum(i - 1, 0)
    full = lambda shape: pl.BlockSpec(shape, lambda i: (0,) * len(shape), pipeline_mode=pl.Buffered(1))
    return pl.pallas_call(
        functools.partial(_ffn_kernel, seq_len=seq_len, final=final),
        grid=(nblk + 1,),
        in_specs=[pl.BlockSpec((tm, D_MODEL), lambda i: (cur(i), 0)),
                  pl.BlockSpec((tm, D_MODEL), lambda i: (done(i), 0)),
                  pl.BlockSpec((None, 6, D_MODEL), lambda i: (row(cur(i)), 0, 0)),
                  pl.BlockSpec((None, 6, D_MODEL), lambda i: (row(done(i)), 0, 0)),
                  full((1, D_MODEL)), full((D_MODEL, 2 * D_FF)), full((3, 2 * D_FF)), full((1, 2 * D_FF)),
                  full((D_FF, D_MODEL)), full((1, D_MODEL))],
        out_specs=pl.BlockSpec((tm, D_MODEL), lambda i: (done(i), 0)),
        out_shape=jax.ShapeDtypeStruct((n, D_MODEL), F32),
        scratch_shapes=[pltpu.VMEM((tm, 2 * D_FF), F32), pltpu.VMEM((tm, 2 * D_FF), F32)],
        compiler_params=_cparams(("arbitrary",)),
        name="ffn",
    )(x2d, x2d, mod6, mod6, lw['norm2'], lw['w_up'], lw['w_conv_ffn'], lw['b_conv_ffn'], lw['w_down'],
      final_norm)


def _run_pass(x, mod, layers, consts, init_states, final_norm):
    b, seq_len, _ = x.shape
    n = b * seq_len
    is_ctx = init_states is None
    rows_per_cond = None if is_ctx else seq_len
    step_rows = 4096
    x2d = x.reshape(n, D_MODEL)
    depth = len(layers)
    finals = []
    for li, lw in enumerate(layers):
        mod6 = mod[li]
        proj, ab, u4f = _in_projection(x2d, mod6, lw['norm1'], lw['w_in_main'], lw['w_in_ab'], rows_per_cond)
        pre = _gdn_precompute(proj, ab, lw['w_conv_qkv'], lw['gate_par'], consts['gdn_masks'], consts['gdn_tri'],
                              consts['gdn_levels'], seq_len)
        o_f, o_b, sd = _gdn_scan(pre, None if is_ctx else init_states[0][:, li], seq_len)
        yd, sfin = _s5_mixer(u4f, lw['s5_mats'], None if is_ctx else init_states[1][:, li], seq_len)
        o_c, sr = _retention(proj, consts['ret_tables'], consts.get('rope'),
                             None if is_ctx else init_states[2][:, li], seq_len, step_rows)
        x1 = _merge(x2d, o_f, o_b, yd, o_c, proj, mod6, lw, rows_per_cond)
        x2d = _ffn(x1, mod6, lw, final_norm, seq_len, rows_per_cond, final=(li == depth - 1))
        finals.append((sd, sfin, sr))
    return x2d.reshape(b, seq_len, D_MODEL), finals


def _s5_state_to_rows(s_re, s_im):
    shp = s_re.shape[:3]
    re = s_re.reshape(shp + (S5_BLOCKS, 4, LANE))
    im = s_im.reshape(shp + (S5_BLOCKS, 4, LANE))
    return jnp.concatenate([re, im], axis=-2)


def _s5_rows_to_state(fin):
    nb = fin.shape[2]
    re = fin[:, :, :, 0:4, :].reshape(2, S5_BLOCKS, nb, 8, P_B)
    im = fin[:, :, :, 4:8, :].reshape(2, S5_BLOCKS, nb, 8, P_B)
    perm = lambda t: jnp.transpose(t, (2, 0, 1, 3, 4)).reshape(nb, 2, G_B, P_B)
    return perm(re), perm(im)


def kernel(x_prompt, x_sample, state_delta, state_ssm_re, state_ssm_im, state_ret, c, c_ctx,
           final_norm, norm1, norm2, w_mod, b_mod, w_in, w_conv_qkv, a_log, dt_bias, norm_a, w_br_a,
           ssm_lam_re, ssm_lam_im, ssm_log_dt, ssm_b_re, ssm_b_im, ssm_c_re, ssm_c_im, ssm_d,
           w_glu, b_glu, w_br_b, w_br_c, w_o, w_up, w_conv_ffn, b_conv_ffn, w_down):
    depth = w_in.shape[0]
    dec_b = x_sample.shape[0]
    n_qkv = 3 * W_MIX
    gate_par = jnp.pad(jnp.stack([a_log.reshape(depth, 8), dt_bias.reshape(depth, 8)], axis=1),
                       ((0, 0), (0, 6), (0, LANE - 8)))
    compact, s5_a1, s5_a2 = jax.vmap(_s5_matrices)(ssm_lam_re, ssm_lam_im, ssm_log_dt, ssm_b_re, ssm_b_im,
                                                   ssm_c_re, ssm_c_im)
    expanded = _s5_expand(compact, _s5_expand_consts())
    layers = []
    for i in range(depth):
        wi = w_in[i]
        w_main = jnp.concatenate([wi[:, -3 * D_MODEL:], wi[:, :n_qkv], wi[:, n_qkv + 16:-3 * D_MODEL]],
                                 axis=1).astype(BF16)
        w_ab = jnp.pad(wi[:, n_qkv:n_qkv + 16], ((0, 0), (0, LANE - 16))).astype(BF16)
        layers.append(dict(
            norm1=norm1[i][None], norm2=norm2[i][None], w_in_main=w_main, w_in_ab=w_ab,
            w_conv_qkv=w_conv_qkv[i], gate_par=gate_par[i], norm_a=norm_a[i][None],
            s5_mats=(expanded, i, s5_a1[i], s5_a2[i]),
            ssm_d=ssm_d[i][None], w_glu=w_glu[i].astype(BF16), b_glu=b_glu[i][None],
            w_br_a=w_br_a[i].astype(BF16), w_br_b=w_br_b[i].astype(BF16), w_br_c=w_br_c[i].astype(BF16),
            w_o=w_o[i].astype(BF16), w_up=w_up[i].astype(BF16), w_conv_ffn=w_conv_ffn[i],
            b_conv_ffn=b_conv_ffn[i][None], w_down=w_down[i].astype(BF16)))
    masks = _gdn_masks()
    consts = dict(gdn_masks=masks, gdn_tri=masks[M_INCL:M_INCL + 2].astype(BF16),
                  gdn_levels=masks[M_LEVEL + 1:M_LEVEL + 7].astype(BF16), ret_tables=_ret_tables())
    fnorm = final_norm[None]

    cond8 = jnp.concatenate([c_ctx[None], c, jnp.zeros((8 - 1 - dec_b, D_MODEL), F32)], axis=0)
    mod = _modulation(cond8, w_mod, b_mod).reshape(depth, 8, 6, D_MODEL)

    y_prompt, ctx_finals = _run_pass(x_prompt, mod, layers, consts, None, fnorm)
    consts_lat = dict(consts, rope=_rope_tables(x_sample.shape[1]))
    x0_rows = _s5_state_to_rows(state_ssm_re, state_ssm_im)
    y_sample, _ = _run_pass(x_sample, mod, layers, consts_lat, (state_delta, x0_rows, state_ret), fnorm)

    new_delta = jnp.stack([f[0] for f in ctx_finals], axis=1)
    s5 = [_s5_rows_to_state(f[1]) for f in ctx_finals]
    new_re = jnp.stack([s[0] for s in s5], axis=1)
    new_im = jnp.stack([s[1] for s in s5], axis=1)
    new_ret = jnp.stack([f[2] for f in ctx_finals], axis=1)
    return (y_prompt, y_sample, new_delta, new_re, new_im, new_ret)
```
